```python
import math
import jax
import jax.numpy as jnp
from jax import lax
import numpy as np

D_MODEL = 1024
BATCH = 4
SEQ = 8192
DEPTH = 4

CTX_LEN = 256
GRID_W = 64
EPS = 1e-6
N_BRANCH = 3
F32 = jnp.float32

SSD_HEADS = 16
SSD_HEAD_DIM = 64
SSD_INNER = SSD_HEADS * SSD_HEAD_DIM
SSD_GROUPS = 2
SSD_STATE = 128
SSD_CONV = 5
SSD_CHUNK = 128
SSD_CONV_DIM = SSD_INNER + 2 * SSD_GROUPS * SSD_STATE

RET_HEADS = 4
RET_KEY_DIM = 64
RET_VAL_DIM = 128
RET_QK = RET_HEADS * RET_KEY_DIM
RET_INNER = RET_HEADS * RET_VAL_DIM
RET_CHUNK = 128
ROPE_BASE = 10000.0

RWKV_HEADS = 8
RWKV_HEAD_DIM = 64
RWKV_INNER = RWKV_HEADS * RWKV_HEAD_DIM
RWKV_DECAY_LORA = 64
RWKV_AAA_LORA = 64
RWKV_GATE_LORA = 128
RWKV_IN = 3 * RWKV_INNER + RWKV_DECAY_LORA + RWKV_AAA_LORA + RWKV_GATE_LORA
RWKV_LN_EPS = 64e-5

D_FF = 4 * D_MODEL

IN_SIZES = (N_BRANCH * D_MODEL, SSD_INNER, SSD_CONV_DIM, 2 * SSD_HEADS, RET_QK, RET_QK, RET_INNER, RET_INNER, RWKV_IN)
D_IN = N_BRANCH * D_MODEL + SSD_INNER + SSD_CONV_DIM + 2 * SSD_HEADS + 2 * RET_QK + 2 * RET_INNER + RWKV_IN

kernel_name = 'hybrid_ssd_retention_rwkv7_prefix_dit'


def split_last(u, sizes):
    idx = np.cumsum(np.array(sizes))[:-1].tolist()
    return jnp.split(u, idx, axis=-1)


def rmsnorm(x, w):
    xf = x.astype(F32)
    y = xf * lax.rsqrt(jnp.mean(xf * xf, axis=-1, keepdims=True) + EPS)
    return (y * w.astype(F32)).astype(x.dtype)


def head_layernorm(y, eps):
    yc = y - jnp.mean(y, axis=-1, keepdims=True)
    return yc * lax.rsqrt(jnp.mean(yc * yc, axis=-1, keepdims=True) + eps)


def flip_t(z):
    return jnp.flip(z, axis=1)


def modulation(cond, lp):
    m = jax.nn.silu(cond) @ lp['ada_w'] + lp['ada_b']
    return jnp.split(m, 6, axis=-1)


def modulate(h, shift, scale):
    return h * (1.0 + scale) + shift


def grid_rope(t):
    rows = t // GRID_W
    row = jnp.repeat(jnp.arange(rows), GRID_W).astype(F32)
    col = jnp.tile(jnp.arange(GRID_W), rows).astype(F32)
    n_freq = RET_KEY_DIM // 4
    inv = jnp.power(ROPE_BASE, -jnp.arange(n_freq, dtype=F32) / n_freq)
    ang = jnp.concatenate([row[:, None] * inv, col[:, None] * inv], axis=-1)
    return jnp.cos(ang), jnp.sin(ang)


def apply_rope(u, cos, sin):
    u2 = u.reshape(u.shape[:-1] + (u.shape[-1] // 2, 2))
    x1, x2 = u2[..., 0], u2[..., 1]
    cc, ss = cos[None, :, None, :], sin[None, :, None, :]
    return jnp.stack([x1 * cc - x2 * ss, x1 * ss + x2 * cc], axis=-1).reshape(u.shape)


def centred_dwconv(u, w, bias):
    k = w.shape[0]
    pad = k // 2
    t = u.shape[1]
    up = jnp.pad(u, ((0, 0), (pad, pad), (0, 0)))
    out = up[:, 0:t] * w[0]
    for j in range(1, k):
        out = out + up[:, j:j + t] * w[j]
    return out + bias


def chunked_scan(q, k, v, log_a, s0, chunk):
    b, t, g, n = q.shape
    hg, p = v.shape[-2], v.shape[-1]
    nc = t // chunk
    q = q.reshape(b, nc, chunk, g, n)
    k = k.reshape(b, nc, chunk, g, n)
    v = v.reshape(b, nc, chunk, g, hg, p)
    a_cum = jnp.cumsum(log_a.reshape(b, nc, chunk, g, hg), axis=2)
    seg = jnp.moveaxis(a_cum, 2, -1)
    lower = jnp.tril(jnp.ones((chunk, chunk), dtype=bool))
    decay_in = jnp.exp(jnp.where(lower, seg[..., :, None] - seg[..., None, :], -jnp.inf))
    scores = jnp.einsum('bclgn,bcsgn->bcgls', q, k)
    y_diag = jnp.einsum('bcghls,bcsghp->bclghp', scores[:, :, :, None] * decay_in, v)
    a_last = a_cum[:, :, -1]
    v_end = v * jnp.exp(a_last[:, :, None] - a_cum)[..., None]
    chunk_states = jnp.einsum('bclgn,bclghp->bcghnp', k, v_end)

    def carry(s, inp):
        st, dec = inp
        return s * dec[..., None, None] + st, s

    s_final, s_prev = lax.scan(carry, s0, (jnp.moveaxis(chunk_states, 1, 0), jnp.moveaxis(jnp.exp(a_last), 1, 0)))
    y_off = jnp.einsum('bclgn,cbghnp->bclghp', q, s_prev) * jnp.exp(a_cum)[..., None]
    return (y_diag + y_off).reshape(b, t, g, hg, p), s_final


def bidir_scan(q, k, v_f, v_b, la_f, la_b, s0, chunk):
    y_f, s_f = chunked_scan(q, k, v_f, la_f, s0[0], chunk)
    y_b, s_b = chunked_scan(flip_t(q), flip_t(k), flip_t(v_b), flip_t(la_b), s0[1], chunk)
    return y_f + flip_t(y_b), s_f, s_b


def ssd_branch(z, xbc, dt_raw, lp, s0):
    b, t, _ = z.shape
    hg = SSD_HEADS // SSD_GROUPS
    xbc = jax.nn.silu(centred_dwconv(xbc, lp['ssd_conv_w'], lp['ssd_conv_b'])).astype(F32)
    xs, bm, cm = split_last(xbc, (SSD_INNER, SSD_GROUPS * SSD_STATE, SSD_GROUPS * SSD_STATE))
    xs = xs.reshape(b, t, SSD_GROUPS, hg, SSD_HEAD_DIM)
    bm = bm.reshape(b, t, SSD_GROUPS, SSD_STATE)
    cm = cm.reshape(b, t, SSD_GROUPS, SSD_STATE)
    dt = jax.nn.softplus(dt_raw.astype(F32).reshape(b, t, 2, SSD_HEADS) + lp['ssd_dt_bias'].astype(F32))
    dt = dt.reshape(b, t, 2, SSD_GROUPS, hg)
    a = -jnp.exp(lp['ssd_a_log'].astype(F32)).reshape(2, SSD_GROUPS, hg)
    dt_f, dt_b = dt[:, :, 0], dt[:, :, 1]
    y, s_f, s_b = bidir_scan(cm, bm, xs * dt_f[..., None], xs * dt_b[..., None],
                             dt_f * a[0], dt_b * a[1], s0, SSD_CHUNK)
    y = y + xs * lp['ssd_d'].astype(F32).reshape(SSD_GROUPS, hg, 1)
    y = y.reshape(b, t, SSD_INNER) * jax.nn.silu(z.astype(F32))
    yg = y.reshape(b, t, SSD_GROUPS, SSD_INNER // SSD_GROUPS)
    yg = yg * lax.rsqrt(jnp.mean(yg * yg, axis=-1, keepdims=True) + EPS)
    y = yg.reshape(b, t, SSD_INNER) * lp['ssd_norm_w'].astype(F32)
    return y.astype(z.dtype), (s_f, s_b)


def retention_branch(q, k, v, g, lp, s0, rope):
    b, t, _ = q.shape
    q = q.astype(F32).reshape(b, t, RET_HEADS, RET_KEY_DIM)
    k = k.astype(F32).reshape(b, t, RET_HEADS, RET_KEY_DIM) * (RET_KEY_DIM ** -0.5)
    if rope is not None:
        q = apply_rope(q, rope[0], rope[1])
        k = apply_rope(k, rope[0], rope[1])
    v = v.astype(F32).reshape(b, t, RET_HEADS, 1, RET_VAL_DIM)
    log_gamma = -jax.nn.softplus(-lp['ret_decay'].astype(F32))
    la_f = jnp.broadcast_to(log_gamma[0][:, None], (b, t, RET_HEADS, 1))
    la_b = jnp.broadcast_to(log_gamma[1][:, None], (b, t, RET_HEADS, 1))
    y, s_f, s_b = bidir_scan(q, k, v, v, la_f, la_b, s0, RET_CHUNK)
    y = head_layernorm(y.reshape(b, t, RET_HEADS, RET_VAL_DIM), EPS)
    y = y.reshape(b, t, RET_INNER) * jax.nn.silu(g.astype(F32))
    return y.astype(g.dtype), (s_f, s_b)


def rwkv7_scan(r, w, k, v, a, bb, s0):
    def step(s, inp):
        r_t, w_t, k_t, v_t, a_t, b_t = inp
        sa = jnp.einsum('bhvk,bhk->bhv', s, a_t)
        s = s * w_t[:, :, None, :] + sa[..., None] * b_t[:, :, None, :] + v_t[..., None] * k_t[:, :, None, :]
        return s, jnp.einsum('bhvk,bhk->bhv', s, r_t)

    xs = tuple(jnp.moveaxis(z, 1, 0) for z in (r, w, k, v, a, bb))
    s_fin, y = lax.scan(step, s0, xs)
    return jnp.moveaxis(y, 0, 1), s_fin


def rwkv_branch(u, lp, s0):
    b, t, _ = u.shape
    out_dtype = u.dtype
    u = u.astype(F32)
    prev = jnp.pad(u, ((0, 0), (1, 0), (0, 0)))[:, :-1]
    nxt = jnp.pad(u, ((0, 0), (0, 1), (0, 0)))[:, 1:]
    u = u + lp['rwkv_mix'].astype(F32) * (0.5 * (prev + nxt) - u)
    r, k, v, w_lo, a_lo, g_lo = split_last(u, (RWKV_INNER, RWKV_INNER, RWKV_INNER, RWKV_DECAY_LORA, RWKV_AAA_LORA, RWKV_GATE_LORA))

    def heads(z):
        return z.reshape(b, t, RWKV_HEADS, RWKV_HEAD_DIM)

    w0, w2 = lp['rwkv_w0'].astype(F32), lp['rwkv_w2'].astype(F32)
    a0, a2 = lp['rwkv_a0'].astype(F32), lp['rwkv_a2'].astype(F32)
    k_a = lp['rwkv_k_a'].astype(F32)
    tw = jnp.tanh(w_lo)
    a_shared = a_lo @ a2
    kk = heads(k * lp['rwkv_k_k'].astype(F32))
    kk = kk / jnp.maximum(jnp.sqrt(jnp.sum(kk * kk, axis=-1, keepdims=True)), 1e-12)

    def direction_inputs(d):
        w_log = -jax.nn.softplus(-(w0[d] + tw @ w2[d])) - 0.5
        decay = jnp.exp(-jnp.exp(w_log))
        a = jax.nn.sigmoid(a0[d] + a_shared)
        kd = k * (1.0 + (a - 1.0) * k_a)
        return heads(decay), heads(kd), kk * heads(a)

    dec_f, k_f, b_f = direction_inputs(0)
    dec_b, k_b, b_b = direction_inputs(1)
    rh, vh = heads(r), heads(v)
    y_f, s_f = rwkv7_scan(rh, dec_f, k_f, vh, -kk, b_f, s0[0])
    y_b, s_b = rwkv7_scan(flip_t(rh), flip_t(dec_b), flip_t(k_b), flip_t(vh), flip_t(-kk), flip_t(b_b), s0[1])
    y = head_layernorm(y_f + flip_t(y_b), RWKV_LN_EPS)
    y = y * lp['rwkv_lnx_w'].astype(F32).reshape(RWKV_HEADS, RWKV_HEAD_DIM) + lp['rwkv_lnx_b'].astype(F32).reshape(RWKV_HEADS, RWKV_HEAD_DIM)
    bonus = jnp.sum(rh * (k_f + k_b) * lp['rwkv_r_k'].astype(F32), axis=-1, keepdims=True) * vh
    g = jax.nn.sigmoid(g_lo) @ lp['rwkv_g2'].astype(F32)
    out = (y + bonus).reshape(b, t, RWKV_INNER) * g
    return out.astype(out_dtype), (s_f, s_b)


def zero_states(b):
    ssd = jnp.zeros((b, SSD_GROUPS, SSD_HEADS // SSD_GROUPS, SSD_STATE, SSD_HEAD_DIM), F32)
    ret = jnp.zeros((b, RET_HEADS, 1, RET_KEY_DIM, RET_VAL_DIM), F32)
    rw = jnp.zeros((b, RWKV_HEADS, RWKV_HEAD_DIM, RWKV_HEAD_DIM), F32)
    return ((ssd, ssd), (ret, ret), (rw, rw))


def token_mixer(h, lp, init_states, rope, with_output):
    b, t, _ = h.shape
    u = h @ lp['w_in']
    gate, ssd_z, ssd_xbc, ssd_dt, ret_q, ret_k, ret_v, ret_g, rwkv_u = split_last(u, IN_SIZES)
    y_ssd, st_ssd = ssd_branch(ssd_z, ssd_xbc, ssd_dt, lp, init_states[0])
    y_ret, st_ret = retention_branch(ret_q, ret_k, ret_v, ret_g, lp, init_states[1], rope)
    y_rwkv, st_rwkv = rwkv_branch(rwkv_u, lp, init_states[2])
    states = (st_ssd, st_ret, st_rwkv)
    if not with_output:
        return None, states
    gates = jax.nn.sigmoid(gate.reshape(b, t, N_BRANCH, D_MODEL))
    merged = (gates[:, :, 0] * (y_ssd @ lp['w_ssd_out'])
              + gates[:, :, 1] * (y_ret @ lp['w_ret_out'])
              + gates[:, :, 2] * (y_rwkv @ lp['w_rwkv_out']))
    return merged @ lp['w_out'], states


def squared_relu_mlp(h, lp):
    return jnp.square(jax.nn.relu(h @ lp['mlp_w1'])) @ lp['mlp_w2']


def setup_inputs(seed: int = 0) -> dict:
    key = jax.random.key(seed)
    ks = jax.random.split(key, 32)
    L = DEPTH

    def nrm(i, shape, scale):
        return scale * jax.random.normal(ks[i], shape, F32)

    x = nrm(0, (BATCH, SEQ, D_MODEL), 1.0)
    c = nrm(1, (BATCH, D_MODEL), 1.0)
    ctx = nrm(2, (BATCH, CTX_LEN, D_MODEL), 1.0)
    c_ctx = nrm(3, (D_MODEL,), 1.0)
    norm_w = 1.0 + nrm(4, (L, 4, D_MODEL), 0.05)
    ada_w = nrm(5, (L, D_MODEL, 6 * D_MODEL), 0.5 * D_MODEL ** -0.5)
    ada_b = nrm(6, (L, 6 * D_MODEL), 0.02)
    w_in = nrm(7, (L, D_MODEL, D_IN), D_MODEL ** -0.5)
    ssd_conv_w = nrm(8, (L, SSD_CONV, SSD_CONV_DIM), SSD_CONV ** -0.5)
    ssd_conv_b = nrm(9, (L, SSD_CONV_DIM), 0.02)
    dt0 = jnp.exp(jax.random.uniform(ks[10], (L, 2, SSD_HEADS), F32, math.log(1e-3), math.log(1e-1)))
    ssd_dt_bias = dt0 + jnp.log(-jnp.expm1(-dt0))
    ssd_a_log = jnp.log(jax.random.uniform(ks[11], (L, 2, SSD_HEADS), F32, 1.0, 16.0))
    ssd_d = 1.0 + nrm(12, (L, SSD_HEADS), 0.1)
    ssd_norm_w = 1.0 + nrm(13, (L, SSD_INNER), 0.05)
    gamma = 1.0 - jnp.power(2.0, -5.0 - jnp.arange(RET_HEADS, dtype=F32))
    ret_decay = jnp.log(gamma / (1.0 - gamma)) + nrm(14, (L, 2, RET_HEADS), 0.1)
    rwkv_mix = jax.random.uniform(ks[15], (L, RWKV_IN), F32)
    rwkv_w0 = jnp.linspace(-6.0, -1.0, RWKV_INNER, dtype=F32) + nrm(16, (L, 2, RWKV_INNER), 0.1)
    rwkv_w2 = nrm(17, (L, 2, RWKV_DECAY_LORA, RWKV_INNER), 0.5 * RWKV_DECAY_LORA ** -0.5)
    rwkv_a0 = nrm(18, (L, 2, RWKV_INNER), 0.1)
    rwkv_a2 = nrm(19, (L, RWKV_AAA_LORA, RWKV_INNER), 0.5 * RWKV_AAA_LORA ** -0.5)
    rwkv_g2 = nrm(20, (L, RWKV_GATE_LORA, RWKV_INNER), RWKV_GATE_LORA ** -0.5)
    rwkv_k_k = 0.85 + nrm(21, (L, RWKV_INNER), 0.05)
    rwkv_k_a = 1.0 + nrm(22, (L, RWKV_INNER), 0.05)
    rwkv_r_k = nrm(23, (L, RWKV_HEADS, RWKV_HEAD_DIM), 0.1)
    rwkv_lnx_w = 1.0 + nrm(24, (L, RWKV_INNER), 0.05)
    rwkv_lnx_b = nrm(25, (L, RWKV_INNER), 0.02)
    w_ssd_out = nrm(26, (L, SSD_INNER, D_MODEL), SSD_INNER ** -0.5)
    w_ret_out = nrm(27, (L, RET_INNER, D_MODEL), RET_INNER ** -0.5)
    w_rwkv_out = nrm(28, (L, RWKV_INNER, D_MODEL), RWKV_INNER ** -0.5)
    w_out = nrm(29, (L, D_MODEL, D_MODEL), D_MODEL ** -0.5)
    mlp_w1 = nrm(30, (L, D_MODEL, D_FF), D_MODEL ** -0.5)
    mlp_w2 = nrm(31, (L, D_FF, D_MODEL), D_FF ** -0.5)
    return {'x': x, 'c': c, 'ctx': ctx, 'c_ctx': c_ctx, 'norm_w': norm_w, 'ada_w': ada_w, 'ada_b': ada_b,
            'w_in': w_in, 'ssd_conv_w': ssd_conv_w, 'ssd_conv_b': ssd_conv_b, 'ssd_dt_bias': ssd_dt_bias,
            'ssd_a_log': ssd_a_log, 'ssd_d': ssd_d, 'ssd_norm_w': ssd_norm_w, 'ret_decay': ret_decay,
            'rwkv_mix': rwkv_mix, 'rwkv_w0': rwkv_w0, 'rwkv_w2': rwkv_w2, 'rwkv_a0': rwkv_a0, 'rwkv_a2': rwkv_a2,
            'rwkv_g2': rwkv_g2, 'rwkv_k_k': rwkv_k_k, 'rwkv_k_a': rwkv_k_a, 'rwkv_r_k': rwkv_r_k,
            'rwkv_lnx_w': rwkv_lnx_w, 'rwkv_lnx_b': rwkv_lnx_b, 'w_ssd_out': w_ssd_out, 'w_ret_out': w_ret_out,
            'w_rwkv_out': w_rwkv_out, 'w_out': w_out, 'mlp_w1': mlp_w1, 'mlp_w2': mlp_w2}


def reference(x, c, ctx, c_ctx, norm_w, ada_w, ada_b, w_in, ssd_conv_w, ssd_conv_b, ssd_dt_bias, ssd_a_log,
              ssd_d, ssd_norm_w, ret_decay, rwkv_mix, rwkv_w0, rwkv_w2, rwkv_a0, rwkv_a2, rwkv_g2, rwkv_k_k,
              rwkv_k_a, rwkv_r_k, rwkv_lnx_w, rwkv_lnx_b, w_ssd_out, w_ret_out, w_rwkv_out, w_out, mlp_w1, mlp_w2):
    layer_params = {'norm_w': norm_w, 'ada_w': ada_w, 'ada_b': ada_b, 'w_in': w_in,
                    'ssd_conv_w': ssd_conv_w, 'ssd_conv_b': ssd_conv_b, 'ssd_dt_bias': ssd_dt_bias,
                    'ssd_a_log': ssd_a_log, 'ssd_d': ssd_d, 'ssd_norm_w': ssd_norm_w, 'ret_decay': ret_decay,
                    'rwkv_mix': rwkv_mix, 'rwkv_w0': rwkv_w0, 'rwkv_w2': rwkv_w2, 'rwkv_a0': rwkv_a0,
                    'rwkv_a2': rwkv_a2, 'rwkv_g2': rwkv_g2, 'rwkv_k_k': rwkv_k_k, 'rwkv_k_a': rwkv_k_a,
                    'rwkv_r_k': rwkv_r_k, 'rwkv_lnx_w': rwkv_lnx_w, 'rwkv_lnx_b': rwkv_lnx_b,
                    'w_ssd_out': w_ssd_out, 'w_ret_out': w_ret_out, 'w_rwkv_out': w_rwkv_out, 'w_out': w_out,
                    'mlp_w1': mlp_w1, 'mlp_w2': mlp_w2}
    b = x.shape[0]
    rope = grid_rope(x.shape[1])
    c_lat = c[:, None, :]
    c_con = c_ctx[None, None, :]
    for l in range(DEPTH):
        lp = {name: arr[l] for name, arr in layer_params.items()}
        last = l == DEPTH - 1
        nw = lp['norm_w']
        sh1_x, sc1_x, g1_x, sh2_x, sc2_x, g2_x = modulation(c_lat, lp)
        sh1_c, sc1_c, g1_c, sh2_c, sc2_c, g2_c = modulation(c_con, lp)
        hc = modulate(rmsnorm(ctx, nw[0]), sh1_c, sc1_c)
        yc, ctx_states = token_mixer(hc, lp, zero_states(b), None, not last)
        hx = modulate(rmsnorm(x, nw[0]), sh1_x, sc1_x)
        yx, _ = token_mixer(hx, lp, ctx_states, rope, True)
        x = x + g1_x * rmsnorm(yx, nw[1])
        x = x + g2_x * rmsnorm(squared_relu_mlp(modulate(rmsnorm(x, nw[2]), sh2_x, sc2_x), lp), nw[3])
        if not last:
            ctx = ctx + g1_c * rmsnorm(yc, nw[1])
            ctx = ctx + g2_c * rmsnorm(squared_relu_mlp(modulate(rmsnorm(ctx, nw[2]), sh2_c, sc2_c), lp), nw[3])
    return x
```

```python
import functools
import math

import numpy as np
import jax
import jax.numpy as jnp
from jax import lax
from jax.experimental import pallas as pl
from jax.experimental.pallas import tpu as pltpu

F32 = jnp.float32
BF16 = jnp.bfloat16
HIGHEST = lax.Precision.HIGHEST

D_MODEL = 1024
DEPTH = 4
GRID_W = 64
EPS = 1e-6
N_BRANCH = 3

SSD_HEADS = 16
SSD_HEAD_DIM = 64
SSD_INNER = SSD_HEADS * SSD_HEAD_DIM
SSD_GROUPS = 2
SSD_STATE = 128
SSD_CONV = 5
SSD_CHUNK = 128
SSD_CONV_DIM = SSD_INNER + 2 * SSD_GROUPS * SSD_STATE
SSD_GROUP_W = SSD_INNER // SSD_GROUPS

RET_HEADS = 4
RET_KEY_DIM = 64
RET_VAL_DIM = 128
RET_QK = RET_HEADS * RET_KEY_DIM
RET_INNER = RET_HEADS * RET_VAL_DIM
RET_CHUNK = 128
ROPE_BASE = 10000.0

RWKV_HEADS = 8
RWKV_HEAD_DIM = 64
RWKV_INNER = RWKV_HEADS * RWKV_HEAD_DIM
RWKV_DECAY_LORA = 64
RWKV_AAA_LORA = 64
RWKV_GATE_LORA = 128
RWKV_IN = 3 * RWKV_INNER + RWKV_DECAY_LORA + RWKV_AAA_LORA + RWKV_GATE_LORA
RWKV_LN_EPS = 64e-5
RWKV_CHUNK = 64

D_FF = 4 * D_MODEL

SUBLANES = 8
LANES = 128
VMEM_LIMIT_BYTES = 56 * 1024 * 1024

ROW_TILE = 256
NEG_BIG = -1e30


def _cparams(n_axes):
    return pltpu.CompilerParams(dimension_semantics=("arbitrary",) * n_axes,
                                vmem_limit_bytes=VMEM_LIMIT_BYTES)


def _silu(x):
    return x * jax.nn.sigmoid(x)


def _softplus(x):
    return jnp.maximum(x, 0.0) + jnp.log1p(jnp.exp(-jnp.abs(x)))


def _split3(x):
    x1 = x.astype(BF16)
    r1 = x - x1.astype(F32)
    x2 = r1.astype(BF16)
    r2 = r1 - x2.astype(F32)
    return x1, x2, r2.astype(BF16)


def _dot_sel_rhs(x, sel):
    x1, x2, x3 = _split3(x)
    return (jnp.dot(x1, sel, preferred_element_type=F32)
            + jnp.dot(x2, sel, preferred_element_type=F32)
            + jnp.dot(x3, sel, preferred_element_type=F32))


def _dot_sel_lhs(sel, x):
    x1, x2, x3 = _split3(x)
    return (jnp.dot(sel, x1, preferred_element_type=F32)
            + jnp.dot(sel, x2, preferred_element_type=F32)
            + jnp.dot(sel, x3, preferred_element_type=F32))


def _dot_nt(a, b, precision=None):
    return lax.dot_general(a, b, (((1,), (1,)), ((), ())), preferred_element_type=F32, precision=precision)


def _dot_tn(a, b, precision=None):
    return lax.dot_general(a, b, (((0,), (0,)), ((), ())), preferred_element_type=F32, precision=precision)


def _chunk_of(d, i, ncx, nc):
    bwd = jnp.where(i < ncx, ncx - 1 - i, ncx + nc - 1 - i)
    return jnp.where(d == 0, i, bwd)


def _seg_edges(c, ncx, nc):
    seg_start = jnp.logical_or(c == 0, c == ncx)
    seg_end = jnp.logical_or(c == ncx - 1, c == nc - 1)
    return seg_start, seg_end


def _tri_mask(n, d, strict, reps=1):
    row = lax.broadcasted_iota(jnp.int32, (n, n * reps), 0)
    col = lax.broadcasted_iota(jnp.int32, (n, n * reps), 1) % n
    lead = jnp.where(d == 0, row - col, col - row)
    return lead > 0 if strict else lead >= 0


def _mod_kernel(c_ref, w_ref, b_ref, o_ref):
    s = _silu(c_ref[...])
    o_ref[0] = jnp.dot(s, w_ref[0], preferred_element_type=F32, precision=HIGHEST) + b_ref[0]


def modulation_all(cond, ada_w, ada_b):
    depth = ada_w.shape[0]
    tn = 1536
    return pl.pallas_call(
        _mod_kernel,
        out_shape=jax.ShapeDtypeStruct((depth, SUBLANES, 6 * D_MODEL), F32),
        grid=(depth, 6 * D_MODEL // tn),
        in_specs=[pl.BlockSpec((SUBLANES, D_MODEL), lambda l, j: (0, 0)),
                  pl.BlockSpec((1, D_MODEL, tn), lambda l, j: (l, 0, j)),
                  pl.BlockSpec((1, 1, tn), lambda l, j: (l, 0, j))],
        out_specs=pl.BlockSpec((1, SUBLANES, tn), lambda l, j: (l, 0, j)),
        compiler_params=_cparams(2),
        name="modulation",
    )(cond, ada_w, ada_b.reshape(depth, 1, 6 * D_MODEL))


def _rms(x, w):
    return x * lax.rsqrt(jnp.mean(x * x, axis=-1, keepdims=True) + EPS) * w


def _normmod_kernel(x_ref, nw_ref, mod_ref, h_ref):
    y = _rms(x_ref[0], nw_ref[...])
    shift = mod_ref[0, 0, 0:1, :]
    scale = mod_ref[0, 0, 1:2, :]
    h_ref[0] = (y * (1.0 + scale) + shift).astype(BF16)


def norm_modulate(x, nw, mod, n_ctx_tiles):
    b, ta, dm = x.shape
    return pl.pallas_call(
        _normmod_kernel,
        out_shape=jax.ShapeDtypeStruct((b, ta, dm), BF16),
        grid=(b, ta // ROW_TILE),
        in_specs=[pl.BlockSpec((1, ROW_TILE, dm), lambda bi, i: (bi, i, 0)),
                  pl.BlockSpec((1, dm), lambda bi, i: (0, 0)),
                  pl.BlockSpec((1, 1, SUBLANES, dm), lambda bi, i: (bi, (i >= n_ctx_tiles).astype(jnp.int32), 0, 0))],
        out_specs=pl.BlockSpec((1, ROW_TILE, dm), lambda bi, i: (bi, i, 0)),
        compiler_params=_cparams(2),
        name="norm_modulate",
    )(x, nw, mod)


def _mm_kernel(a_ref, w_ref, o_ref):
    o_ref[...] = jnp.dot(a_ref[...], w_ref[...], preferred_element_type=F32).astype(o_ref.dtype)


def matmul(a, w, tm, tn, out_dtype=F32):
    r, k = a.shape
    n = w.shape[1]
    assert r % tm == 0 and n % tn == 0
    return pl.pallas_call(
        _mm_kernel,
        out_shape=jax.ShapeDtypeStruct((r, n), out_dtype),
        grid=(n // tn, r // tm),
        in_specs=[pl.BlockSpec((tm, k), lambda j, i: (i, 0)),
                  pl.BlockSpec((k, tn), lambda j, i: (0, j))],
        out_specs=pl.BlockSpec((tm, tn), lambda j, i: (i, j)),
        compiler_params=_cparams(2),
        name="in_proj",
    )(a, w)


def _ssd_kernel(xbc_ref, xp_ref, xn_ref, dt_ref, cw_ref, cb_ref, dtb_ref, alog_ref, dskip_ref,
                hexp_ref, y_ref, s_ref, ext_ref, *, ncx, nc):
    d = pl.program_id(1)
    i = pl.program_id(2)
    c = _chunk_of(d, i, ncx, nc)
    seg_start, seg_end = _seg_edges(c, ncx, nc)
    lc = SSD_CHUNK
    hb = SUBLANES

    @pl.when(i == 0)
    def _():
        s_ref[...] = jnp.zeros_like(s_ref)

    ext_ref[0:hb, :] = jnp.where(seg_start, 0.0, xp_ref[0])
    ext_ref[hb:hb + lc, :] = xbc_ref[0]
    ext_ref[hb + lc:hb + lc + hb, :] = jnp.where(seg_end, 0.0, xn_ref[0])
    pad = SSD_CONV // 2
    acc = ext_ref[hb - pad:hb - pad + lc, :] * cw_ref[0:1, :]
    for j in range(1, SSD_CONV):
        acc = acc + ext_ref[hb - pad + j:hb - pad + j + lc, :] * cw_ref[j:j + 1, :]
    xbc = _silu(acc + cb_ref[...])
    xs = xbc[:, :SSD_INNER]
    bm = xbc[:, SSD_INNER:SSD_INNER + SSD_GROUPS * SSD_STATE]
    cm = xbc[:, SSD_INNER + SSD_GROUPS * SSD_STATE:]

    dt = _softplus(dt_ref[0] + dtb_ref[0])
    la = dt * (-jnp.exp(alog_ref[0]))
    tri = _tri_mask(lc, d, strict=False)
    acum = _dot_sel_lhs(tri.astype(BF16), la)
    atot = jnp.sum(la, axis=0, keepdims=True)
    acum_t = acum.T

    hexp = hexp_ref[...]
    dtx = _dot_sel_rhs(dt, hexp)
    eacx = _dot_sel_rhs(jnp.exp(acum), hexp)
    eendx = _dot_sel_rhs(jnp.exp(atot - acum), hexp)
    etotx = _dot_sel_rhs(jnp.broadcast_to(jnp.exp(atot), (SUBLANES, LANES)), hexp)[0:1, :]

    v = xs * dtx
    vb = v.astype(BF16)
    vend = (v * eendx).astype(BF16)
    skip = jnp.where(d == 0, 1.0, 0.0) * dskip_ref[...]
    hg = SSD_HEADS // SSD_GROUPS
    for g in range(SSD_GROUPS):
        gs = slice(g * SSD_STATE, (g + 1) * SSD_STATE)
        gw = slice(g * SSD_GROUP_W, (g + 1) * SSD_GROUP_W)
        cg = cm[:, gs].astype(BF16)
        bg = bm[:, gs].astype(BF16)
        scores = _dot_nt(cg, bg)
        s_prev = s_ref[g]
        y_off = jnp.dot(cg, s_prev.astype(BF16), preferred_element_type=F32) * eacx[:, gw]
        y_ref[0, 0, :, gw] = y_off + xs[:, gw] * skip[:, gw]
        for hh in range(hg):
            h = g * hg + hh
            diff = acum[:, h:h + 1] - acum_t[h:h + 1, :]
            dm = jnp.exp(jnp.where(tri, diff, NEG_BIG))
            hs = slice(h * SSD_HEAD_DIM, (h + 1) * SSD_HEAD_DIM)
            y_ref[0, 0, :, hs] += jnp.dot((scores * dm).astype(BF16), vb[:, hs], preferred_element_type=F32)
        s_ref[g] = s_prev * etotx[:, gw] + _dot_tn(bg, vend[:, gw])


def ssd_scan(u_xbc, u_zdt, conv_w, conv_b, dt_bias, a_log, dskip, hexp, ncx_tokens):
    b, ta, _ = u_xbc.shape
    lc = SSD_CHUNK
    nc = ta // lc
    ncx = ncx_tokens // lc
    nb = lc // SUBLANES
    nblk = ta // SUBLANES
    ch = functools.partial(_chunk_of, ncx=ncx, nc=nc)
    kern = functools.partial(_ssd_kernel, ncx=ncx, nc=nc)
    zdt_blk0 = SSD_INNER // LANES
    return pl.pallas_call(
        kern,
        out_shape=jax.ShapeDtypeStruct((2, b, ta, SSD_INNER), F32),
        grid=(b, 2, nc),
        in_specs=[
            pl.BlockSpec((1, lc, SSD_CONV_DIM), lambda bi, d, i: (bi, ch(d, i), 0)),
            pl.BlockSpec((1, SUBLANES, SSD_CONV_DIM), lambda bi, d, i: (bi, jnp.maximum(ch(d, i) * nb - 1, 0), 0)),
            pl.BlockSpec((1, SUBLANES, SSD_CONV_DIM), lambda bi, d, i: (bi, jnp.minimum((ch(d, i) + 1) * nb, nblk - 1), 0)),
            pl.BlockSpec((1, lc, LANES), lambda bi, d, i: (bi, ch(d, i), zdt_blk0 + d)),
            pl.BlockSpec((SUBLANES, SSD_CONV_DIM), lambda bi, d, i: (0, 0)),
            pl.BlockSpec((1, SSD_CONV_DIM), lambda bi, d, i: (0, 0)),
            pl.BlockSpec((1, 1, LANES), lambda bi, d, i: (d, 0, 0)),
            pl.BlockSpec((1, 1, LANES), lambda bi, d, i: (d, 0, 0)),
            pl.BlockSpec((1, SSD_INNER), lambda bi, d, i: (0, 0)),
            pl.BlockSpec((LANES, SSD_INNER), lambda bi, d, i: (0, 0)),
        ],
        out_specs=pl.BlockSpec((1, 1, lc, SSD_INNER), lambda bi, d, i: (d, bi, ch(d, i), 0)),
        scratch_shapes=[pltpu.VMEM((SSD_GROUPS, SSD_STATE, SSD_GROUP_W), F32),
                        pltpu.VMEM((lc + 2 * SUBLANES, SSD_CONV_DIM), F32)],
        compiler_params=_cparams(3),
        name="ssd_scan",
    )(u_xbc, u_xbc, u_xbc, u_zdt, conv_w, conv_b, dt_bias, a_log, dskip, hexp)


def _ret_kernel(qk_ref, v_ref, cos_ref, sin_ref, dec_ref, y_ref, s_ref, *, ncx, nc):
    d = pl.program_id(1)
    i = pl.program_id(2)
    lc = RET_CHUNK

    @pl.when(i == 0)
    def _():
        s_ref[...] = jnp.zeros_like(s_ref)

    qk = qk_ref[0]
    lane = lax.broadcasted_iota(jnp.int32, qk.shape, 1)
    half = RET_KEY_DIM // 2
    width = qk.shape[1]
    swapped = jnp.where((lane % RET_KEY_DIM) < half,
                        pltpu.roll(qk, width - half, axis=1), pltpu.roll(qk, half, axis=1))
    qk = qk * cos_ref[...] + swapped * sin_ref[...]
    q = qk[:, :RET_QK].astype(BF16)
    k = (qk[:, RET_QK:] * (RET_KEY_DIM ** -0.5)).astype(BF16)
    v = v_ref[0]

    lg = -_softplus(-dec_ref[0])
    row = lax.broadcasted_iota(jnp.int32, (lc, lc), 0)
    col = lax.broadcasted_iota(jnp.int32, (lc, lc), 1)
    tri = jnp.where(d == 0, row - col, col - row) >= 0
    dist = jnp.abs(row - col).astype(F32)
    t_idx = lax.broadcasted_iota(jnp.int32, (lc, 1), 0)
    pos = jnp.where(d == 0, t_idx, lc - 1 - t_idx).astype(F32)
    for h in range(RET_HEADS):
        lgh = lg[:, h:h + 1]
        ks = slice(h * RET_KEY_DIM, (h + 1) * RET_KEY_DIM)
        vs = slice(h * RET_VAL_DIM, (h + 1) * RET_VAL_DIM)
        qh, kh, vh = q[:, ks], k[:, ks], v[:, vs]
        dm = jnp.exp(jnp.where(tri, dist * lgh, NEG_BIG))
        scores = _dot_nt(qh, kh) * dm
        s_prev = s_ref[h]
        y = jnp.dot(scores.astype(BF16), vh.astype(BF16), preferred_element_type=F32)
        y = y + jnp.dot(qh, s_prev.astype(BF16), preferred_element_type=F32) * jnp.exp((pos + 1.0) * lgh)
        y_ref[0, 0, :, vs] = y
        vend = (vh * jnp.exp((lc - 1.0 - pos) * lgh)).astype(BF16)
        s_ref[h] = s_prev * jnp.exp(lc * lgh) + _dot_tn(kh, vend)


def ret_scan(u_ret, cosx, sinx, ret_decay, ncx_tokens):
    b, ta, _ = u_ret.shape
    lc = RET_CHUNK
    nc = ta // lc
    ncx = ncx_tokens // lc
    ch = functools.partial(_chunk_of, ncx=ncx, nc=nc)
    kern = functools.partial(_ret_kernel, ncx=ncx, nc=nc)
    return pl.pallas_call(
        kern,
        out_shape=jax.ShapeDtypeStruct((2, b, ta, RET_INNER), F32),
        grid=(b, 2, nc),
        in_specs=[
            pl.BlockSpec((1, lc, 2 * RET_QK), lambda bi, d, i: (bi, ch(d, i), 0)),
            pl.BlockSpec((1, lc, RET_INNER), lambda bi, d, i: (bi, ch(d, i), 1)),
            pl.BlockSpec((lc, 2 * RET_QK), lambda bi, d, i: (ch(d, i), 0)),
            pl.BlockSpec((lc, 2 * RET_QK), lambda bi, d, i: (ch(d, i), 0)),
            pl.BlockSpec((1, 1, LANES), lambda bi, d, i: (d, 0, 0)),
        ],
        out_specs=pl.BlockSpec((1, 1, lc, RET_INNER), lambda bi, d, i: (d, bi, ch(d, i), 0)),
        scratch_shapes=[pltpu.VMEM((RET_HEADS, RET_KEY_DIM, RET_VAL_DIM), F32)],
        compiler_params=_cparams(3),
        name="ret_scan",
    )(u_ret, u_ret, cosx, sinx, ret_decay)


RWKV_PREC = HIGHEST


def _rwkv_kernel(u_ref, up_ref, un_ref, mix_ref, w0_ref, w2_ref, a0_ref, a2_ref, g2_ref, kk_ref, ka_ref,
                 rk_ref, bd_ref, out_ref, s_ref, *, ncx, nc):
    d = pl.program_id(1)
    i = pl.program_id(2)
    c = _chunk_of(d, i, ncx, nc)
    seg_start, seg_end = _seg_edges(c, ncx, nc)
    lc = RWKV_CHUNK
    nh, hd = RWKV_HEADS, RWKV_HEAD_DIM
    prec = RWKV_PREC

    @pl.when(i == 0)
    def _():
        s_ref[...] = jnp.zeros_like(s_ref)

    u = u_ref[0]
    prow = jnp.where(seg_start, 0.0, up_ref[0, SUBLANES - 1:SUBLANES, :])
    nrow = jnp.where(seg_end, 0.0, un_ref[0, 0:1, :])
    t_idx = lax.broadcasted_iota(jnp.int32, (lc, 1), 0)
    prev = jnp.where(t_idx == 0, prow, pltpu.roll(u, 1, axis=0))
    nxt = jnp.where(t_idx == lc - 1, nrow, pltpu.roll(u, lc - 1, axis=0))
    u = u + mix_ref[...] * (0.5 * (prev + nxt) - u)
    ni = RWKV_INNER
    r, k, v = u[:, 0:ni], u[:, ni:2 * ni], u[:, 2 * ni:3 * ni]
    o = 3 * ni
    w_lo = u[:, o:o + RWKV_DECAY_LORA]
    a_lo = u[:, o + RWKV_DECAY_LORA:o + RWKV_DECAY_LORA + RWKV_AAA_LORA]
    g_lo = u[:, o + RWKV_DECAY_LORA + RWKV_AAA_LORA:]

    bd = bd_ref[...]
    w_log = -_softplus(-(w0_ref[0] + jnp.dot(jnp.tanh(w_lo), w2_ref[0], preferred_element_type=F32,
                                              precision=HIGHEST))) - 0.5
    logw = -jnp.exp(w_log)
    a_gate = jax.nn.sigmoid(a0_ref[0] + jnp.dot(a_lo, a2_ref[...], preferred_element_type=F32,
                                                precision=HIGHEST))
    kk = k * kk_ref[...]
    kk = kk / jnp.maximum(jnp.sqrt(_dot_sel_rhs(kk * kk, bd)), 1e-12)
    kd = k * (1.0 + (a_gate - 1.0) * ka_ref[...])
    bvec = kk * a_gate
    avec = -kk
    bonus = _dot_sel_rhs(r * kd * rk_ref[...], bd) * v
    gate = jnp.dot(jax.nn.sigmoid(g_lo).astype(BF16), g2_ref[...], preferred_element_type=F32)
    out_ref[0, 0, :, ni:2 * ni] = bonus
    out_ref[0, 0, :, 2 * ni:3 * ni] = gate

    incl = _tri_mask(lc, d, strict=False)
    strict = _tri_mask(lc, d, strict=True)
    cw = _dot_sel_lhs(incl.astype(BF16), logw)
    cw_ex = cw - logw
    ctot = jnp.sum(logw, axis=0, keepdims=True)
    e_neg = jnp.exp(-cw)
    e_end = jnp.exp(ctot - cw)
    ar = jnp.concatenate([avec * jnp.exp(cw_ex), r * jnp.exp(cw)], axis=0)
    bk = jnp.concatenate([bvec * e_neg, kd * e_neg], axis=0)
    bk_end = jnp.concatenate([bvec * e_end, kd * e_end], axis=0)
    wtot = jnp.exp(ctot)

    eye = (lax.broadcasted_iota(jnp.int32, (lc, lc), 0) == lax.broadcasted_iota(jnp.int32, (lc, lc), 1))
    eye_f = eye.astype(F32)
    strict2 = _tri_mask(lc, d, strict=True, reps=2)
    incl2 = _tri_mask(lc, d, strict=False, reps=2)
    zeros_v = jnp.zeros((lc, hd), F32)

    def mm(a, b):
        return jnp.dot(a, b, preferred_element_type=F32, precision=prec)

    for h in range(nh):
        hs = slice(h * hd, (h + 1) * hd)
        ar_h, bk_h, bkend_h, v_h = ar[:, hs], bk[:, hs], bk_end[:, hs], v[:, hs]
        p = _dot_nt(ar_h, bk_h, precision=prec)
        m_a = jnp.where(strict2, p[:lc], 0.0)
        m_r = jnp.where(incl2, p[lc:], 0.0)
        m_ab = m_a[:, :lc]
        mv = mm(m_a, jnp.concatenate([zeros_v, v_h], axis=0))
        x = m_ab
        t = eye_f + x
        n_sq = int(math.log2(lc)) - 1
        for _ in range(n_sq):
            x = mm(x, x)
            t = t + mm(t, x)
        a_p = mm(t, ar_h[:lc])
        u_l = mm(t, mv)
        rhs = jnp.concatenate([u_l, v_h], axis=0)
        g_mat = jnp.where(eye, wtot[:, hs], 0.0) + _dot_tn(bkend_h[:lc], a_p, precision=prec)
        h_mat = _dot_tn(bkend_h, rhs, precision=prec)
        r_p = ar_h[lc:] + mm(m_r[:, :lc], a_p)
        y_l = mm(m_r, rhs)
        st = s_ref[h]
        out_ref[0, 0, :, hs] = mm(r_p, st) + y_l
        s_ref[h] = mm(g_mat, st) + h_mat


def rwkv_scan(u_rwkv, mix, w0, w2, a0, a2, g2, k_k, k_a, r_k, bd, ncx_tokens):
    b, ta, _ = u_rwkv.shape
    lc = RWKV_CHUNK
    nc = ta // lc
    ncx = ncx_tokens // lc
    nb = lc // SUBLANES
    nblk = ta // SUBLANES
    ni = RWKV_INNER
    ch = functools.partial(_chunk_of, ncx=ncx, nc=nc)
    kern = functools.partial(_rwkv_kernel, ncx=ncx, nc=nc)
    const2 = lambda bi, d, i: (0, 0)
    return pl.pallas_call(
        kern,
        out_shape=jax.ShapeDtypeStruct((2, b, ta, 3 * ni), F32),
        grid=(b, 2, nc),
        in_specs=[
            pl.BlockSpec((1, lc, RWKV_IN), lambda bi, d, i: (bi, ch(d, i), 0)),
            pl.BlockSpec((1, SUBLANES, RWKV_IN), lambda bi, d, i: (bi, jnp.maximum(ch(d, i) * nb - 1, 0), 0)),
            pl.BlockSpec((1, SUBLANES, RWKV_IN), lambda bi, d, i: (bi, jnp.minimum((ch(d, i) + 1) * nb, nblk - 1), 0)),
            pl.BlockSpec((1, RWKV_IN), const2),
            pl.BlockSpec((1, 1, ni), lambda bi, d, i: (d, 0, 0)),
            pl.BlockSpec((1, RWKV_DECAY_LORA, ni), lambda bi, d, i: (d, 0, 0)),
            pl.BlockSpec((1, 1, ni), lambda bi, d, i: (d, 0, 0)),
            pl.BlockSpec((RWKV_AAA_LORA, ni), const2),
            pl.BlockSpec((RWKV_GATE_LORA, ni), const2),
            pl.BlockSpec((1, ni), const2),
            pl.BlockSpec((1, ni), const2),
            pl.BlockSpec((1, ni), const2),
            pl.BlockSpec((ni, ni), const2),
        ],
        out_specs=pl.BlockSpec((1, 1, lc, 3 * ni), lambda bi, d, i: (d, bi, ch(d, i), 0)),
        scratch_shapes=[pltpu.VMEM((RWKV_HEADS, RWKV_HEAD_DIM, RWKV_HEAD_DIM), F32)],
        compiler_params=_cparams(3),
        name="rwkv_scan",
    )(u_rwkv, u_rwkv, u_rwkv, mix, w0, w2, a0, a2, g2, k_k, k_a, r_k, bd)


def _merge_kernel(ssd_ref, z_ref, ret_ref, rg_ref, rw_ref, gate_ref, x_ref, mod_ref, nw_ref, ssdnw_ref,
                  lnw_ref, lnb_ref, bd_ref, wso_ref, wro_ref, wwo_ref, wo_ref, o_ref):
    ys = (ssd_ref[0, 0] + ssd_ref[1, 0]) * _silu(z_ref[0][:, :SSD_INNER])
    parts = []
    for g in range(SSD_GROUPS):
        yg = ys[:, g * SSD_GROUP_W:(g + 1) * SSD_GROUP_W]
        parts.append(yg * lax.rsqrt(jnp.mean(yg * yg, axis=-1, keepdims=True) + EPS))
    ys = jnp.concatenate(parts, axis=1) * ssdnw_ref[...]
    o_ssd = jnp.dot(ys.astype(BF16), wso_ref[...], preferred_element_type=F32)

    yr = ret_ref[0, 0] + ret_ref[1, 0]
    parts = []
    for h in range(RET_HEADS):
        yh = yr[:, h * RET_VAL_DIM:(h + 1) * RET_VAL_DIM]
        yc = yh - jnp.mean(yh, axis=-1, keepdims=True)
        parts.append(yc * lax.rsqrt(jnp.mean(yc * yc, axis=-1, keepdims=True) + EPS))
    yr = jnp.concatenate(parts, axis=1) * _silu(rg_ref[0])
    o_ret = jnp.dot(yr.astype(BF16), wro_ref[...], preferred_element_type=F32)

    ni = RWKV_INNER
    rw0 = rw_ref[0, 0]
    rw1 = rw_ref[1, 0]
    yw = rw0[:, :ni] + rw1[:, :ni]
    bd = bd_ref[...]
    inv_hd = 1.0 / RWKV_HEAD_DIM
    yc = yw - _dot_sel_rhs(yw, bd) * inv_hd
    var = _dot_sel_rhs(yc * yc, bd) * inv_hd
    yw = yc * lax.rsqrt(var + RWKV_LN_EPS) * lnw_ref[...] + lnb_ref[...]
    yw = (yw + rw0[:, ni:2 * ni] + rw1[:, ni:2 * ni]) * rw0[:, 2 * ni:3 * ni]
    o_rw = jnp.dot(yw.astype(BF16), wwo_ref[...], preferred_element_type=F32)

    gate = gate_ref[0]
    merged = (jax.nn.sigmoid(gate[:, :D_MODEL]) * o_ssd
              + jax.nn.sigmoid(gate[:, D_MODEL:2 * D_MODEL]) * o_ret
              + jax.nn.sigmoid(gate[:, 2 * D_MODEL:]) * o_rw)
    yx = jnp.dot(merged.astype(BF16), wo_ref[...], preferred_element_type=F32)
    g1 = mod_ref[0, 0, 2:3, :]
    o_ref[0] = x_ref[0] + g1 * _rms(yx, nw_ref[...])


def merge_out(ssd_y, u_zdt, ret_y, u_ret, rw_y, u_gate, x, mod, nw1, ssd_nw, ln_w, ln_b, bd,
              w_ssd_out, w_ret_out, w_rwkv_out, w_out, n_ctx_tiles):
    b, ta, dm = x.shape
    tm = ROW_TILE
    row = lambda bi, i: (bi, i, 0)
    both = lambda bi, i: (0, bi, i, 0)
    const2 = lambda bi, i: (0, 0)
    return pl.pallas_call(
        _merge_kernel,
        out_shape=jax.ShapeDtypeStruct((b, ta, dm), F32),
        grid=(b, ta // tm),
        in_specs=[
            pl.BlockSpec((2, 1, tm, SSD_INNER), both),
            pl.BlockSpec((1, tm, u_zdt.shape[2]), row),
            pl.BlockSpec((2, 1, tm, RET_INNER), both),
            pl.BlockSpec((1, tm, RET_INNER), lambda bi, i: (bi, i, 2)),
            pl.BlockSpec((2, 1, tm, 3 * RWKV_INNER), both),
            pl.BlockSpec((1, tm, N_BRANCH * dm), row),
            pl.BlockSpec((1, tm, dm), row),
            pl.BlockSpec((1, 1, SUBLANES, dm), lambda bi, i: (bi, (i >= n_ctx_tiles).astype(jnp.int32), 0, 0)),
            pl.BlockSpec((1, dm), const2),
            pl.BlockSpec((1, SSD_INNER), const2),
            pl.BlockSpec((1, RWKV_INNER), const2),
            pl.BlockSpec((1, RWKV_INNER), const2),
            pl.BlockSpec((RWKV_INNER, RWKV_INNER), const2),
            pl.BlockSpec((SSD_INNER, dm), const2),
            pl.BlockSpec((RET_INNER, dm), const2),
            pl.BlockSpec((RWKV_INNER, dm), const2),
            pl.BlockSpec((dm, dm), const2),
        ],
        out_specs=pl.BlockSpec((1, tm, dm), row),
        compiler_params=_cparams(2),
        name="merge_out",
    )(ssd_y, u_zdt, ret_y, u_ret, rw_y, u_gate, x, mod, nw1, ssd_nw, ln_w, ln_b, bd,
      w_ssd_out, w_ret_out, w_rwkv_out, w_out)


def _mlp_kernel(x_ref, mod_ref, nw2_ref, nw3_ref, w1_ref, w2_ref, o_ref, h_ref, acc_ref):
    j = pl.program_id(2)

    @pl.when(j == 0)
    def _():
        y = _rms(x_ref[0], nw2_ref[...])
        h_ref[...] = (y * (1.0 + mod_ref[0, 0, 4:5, :]) + mod_ref[0, 0, 3:4, :]).astype(BF16)
        acc_ref[...] = jnp.zeros_like(acc_ref)

    hid = jnp.dot(h_ref[...], w1_ref[...], preferred_element_type=F32)
    hid = jnp.square(jnp.maximum(hid, 0.0))
    acc_ref[...] += jnp.dot(hid.astype(BF16), w2_ref[...], preferred_element_type=F32)

    @pl.when(j == pl.num_programs(2) - 1)
    def _():
        g2 = mod_ref[0, 0, 5:6, :]
        o_ref[0] = x_ref[0] + g2 * _rms(acc_ref[...], nw3_ref[...])


def mlp_block(x, mod, nw2, nw3, w1, w2, n_ctx_tiles, ff_tile=D_FF):
    b, ta, dm = x.shape
    tm = ROW_TILE
    row = lambda bi, i, j: (bi, i, 0)
    const2 = lambda bi, i, j: (0, 0)
    return pl.pallas_call(
        _mlp_kernel,
        out_shape=jax.ShapeDtypeStruct((b, ta, dm), F32),
        grid=(b, ta // tm, D_FF // ff_tile),
        in_specs=[
            pl.BlockSpec((1, tm, dm), row),
            pl.BlockSpec((1, 1, SUBLANES, dm), lambda bi, i, j: (bi, (i >= n_ctx_tiles).astype(jnp.int32), 0, 0)),
            pl.BlockSpec((1, dm), const2),
            pl.BlockSpec((1, dm), const2),
            pl.BlockSpec((dm, ff_tile), lambda bi, i, j: (0, j)),
            pl.BlockSpec((ff_tile, dm), lambda bi, i, j: (j, 0)),
        ],
        out_specs=pl.BlockSpec((1, tm, dm), row),
        scratch_shapes=[pltpu.VMEM((tm, dm), BF16), pltpu.VMEM((tm, dm), F32)],
        compiler_params=_cparams(3),
        name="mlp",
    )(x, mod, nw2, nw3, w1, w2)


def _rope_tables(n_ctx, n_lat):
    rows = n_lat // GRID_W
    row = np.repeat(np.arange(rows), GRID_W).astype(np.float32)
    col = np.tile(np.arange(GRID_W), rows).astype(np.float32)
    n_freq = RET_KEY_DIM // 4
    inv = jnp.power(ROPE_BASE, -jnp.arange(n_freq, dtype=F32) / n_freq)
    ang = jnp.concatenate([jnp.asarray(row)[:, None] * inv, jnp.asarray(col)[:, None] * inv], axis=-1)
    cos = jnp.concatenate([jnp.ones((n_ctx, RET_KEY_DIM // 2), F32), jnp.cos(ang)], axis=0)
    sin = jnp.concatenate([jnp.zeros((n_ctx, RET_KEY_DIM // 2), F32), jnp.sin(ang)], axis=0)
    cos_h = jnp.concatenate([cos, cos], axis=1)
    sin_h = jnp.concatenate([-sin, sin], axis=1)
    reps = 2 * RET_HEADS
    return jnp.tile(cos_h, (1, reps)), jnp.tile(sin_h, (1, reps))


def _pad_lanes(a, width=LANES):
    return jnp.pad(a, [(0, 0)] * (a.ndim - 1) + [(0, width - a.shape[-1])])


def kernel(x, c, ctx, c_ctx, norm_w, ada_w, ada_b, w_in, ssd_conv_w, ssd_conv_b, ssd_dt_bias, ssd_a_log,
           ssd_d, ssd_norm_w, ret_decay, rwkv_mix, rwkv_w0, rwkv_w2, rwkv_a0, rwkv_a2, rwkv_g2, rwkv_k_k,
           rwkv_k_a, rwkv_r_k, rwkv_lnx_w, rwkv_lnx_b, w_ssd_out, w_ret_out, w_rwkv_out, w_out, mlp_w1, mlp_w2):
    b, n_lat, dm = x.shape
    n_ctx = ctx.shape[1]
    ta = n_ctx + n_lat
    depth = norm_w.shape[0]
    n_ctx_tiles = n_ctx // ROW_TILE
    assert n_ctx % ROW_TILE == 0 and n_lat % ROW_TILE == 0 and b + 1 <= SUBLANES

    hexp = np.zeros((LANES, SSD_INNER), np.float32)
    for h in range(SSD_HEADS):
        hexp[h, h * SSD_HEAD_DIM:(h + 1) * SSD_HEAD_DIM] = 1.0
    hexp = jnp.asarray(hexp, BF16)
    bd = jnp.asarray(np.kron(np.eye(RWKV_HEADS, dtype=np.float32),
                             np.ones((RWKV_HEAD_DIM, RWKV_HEAD_DIM), np.float32)), BF16)
    cosx, sinx = _rope_tables(n_ctx, n_lat)

    cond = jnp.zeros((SUBLANES, dm), F32).at[:b].set(c).at[b].set(c_ctx)
    mod_all = modulation_all(cond, ada_w, ada_b)

    sizes = (N_BRANCH * dm, SSD_INNER, SSD_CONV_DIM, 2 * SSD_HEADS, RET_QK, RET_QK, RET_INNER, RET_INNER, RWKV_IN)
    offs = np.concatenate([[0], np.cumsum(sizes)])
    perm = np.concatenate([np.concatenate([np.arange(0, RET_KEY_DIM, 2), np.arange(1, RET_KEY_DIM, 2)]) + h * RET_KEY_DIM
                           for h in range(RET_HEADS)])

    xall = jnp.concatenate([ctx, x], axis=1)
    tm_mm = 512 if (b * ta) % 512 == 0 else ROW_TILE
    for l in range(depth):
        wl = w_in[l]
        seg = [wl[:, offs[j]:offs[j + 1]] for j in range(len(sizes))]
        w_gate = seg[0].astype(BF16)
        w_zdt = jnp.concatenate([seg[1], _pad_lanes(seg[3][:, :SSD_HEADS]), _pad_lanes(seg[3][:, SSD_HEADS:])],
                                axis=1).astype(BF16)
        w_xbc = seg[2].astype(BF16)
        w_ret = jnp.concatenate([seg[4][:, perm], seg[5][:, perm], seg[6], seg[7]], axis=1).astype(BF16)
        w_rw = seg[8].astype(BF16)

        m = mod_all[l].reshape(SUBLANES, 6, dm)
        m = jnp.pad(m, ((0, 0), (0, SUBLANES - 6), (0, 0)))
        mod = jnp.stack([jnp.broadcast_to(m[b], (b,) + m.shape[1:]), m[:b]], axis=1)
        nw = norm_w[l]

        h = norm_modulate(xall, nw[0:1], mod, n_ctx_tiles).reshape(b * ta, dm)
        u_gate = matmul(h, w_gate, tm_mm, 1024).reshape(b, ta, -1)
        u_zdt = matmul(h, w_zdt, tm_mm, w_zdt.shape[1]).reshape(b, ta, -1)
        u_xbc = matmul(h, w_xbc, tm_mm, SSD_CONV_DIM).reshape(b, ta, -1)
        u_ret = matmul(h, w_ret, tm_mm, w_ret.shape[1]).reshape(b, ta, -1)
        u_rw = matmul(h, w_rw, tm_mm, RWKV_IN).reshape(b, ta, -1)

        conv_w = jnp.pad(ssd_conv_w[l], ((0, SUBLANES - SSD_CONV), (0, 0)))
        dskip = jnp.repeat(ssd_d[l], SSD_HEAD_DIM)[None, :]
        ssd_y = ssd_scan(u_xbc, u_zdt, conv_w, ssd_conv_b[l][None, :],
                         _pad_lanes(ssd_dt_bias[l])[:, None, :], _pad_lanes(ssd_a_log[l])[:, None, :],
                         dskip, hexp, n_ctx)
        ret_y = ret_scan(u_ret, cosx, sinx, _pad_lanes(ret_decay[l])[:, None, :], n_ctx)
        rw_y = rwkv_scan(u_rw, rwkv_mix[l][None, :], rwkv_w0[l][:, None, :], rwkv_w2[l], rwkv_a0[l][:, None, :],
                         rwkv_a2[l], rwkv_g2[l].astype(BF16), rwkv_k_k[l][None, :], rwkv_k_a[l][None, :],
                         rwkv_r_k[l].reshape(1, RWKV_INNER), bd, n_ctx)

        xall = merge_out(ssd_y, u_zdt, ret_y, u_ret, rw_y, u_gate, xall, mod, nw[1:2], ssd_norm_w[l][None, :],
                         rwkv_lnx_w[l][None, :], rwkv_lnx_b[l][None, :], bd,
                         w_ssd_out[l].astype(BF16), w_ret_out[l].astype(BF16), w_rwkv_out[l].astype(BF16),
                         w_out[l].astype(BF16), n_ctx_tiles)
        xall = mlp_block(xall, mod, nw[2:3], nw[3:4], mlp_w1[l].astype(BF16), mlp_w2[l].astype(BF16), n_ctx_tiles)
    return xall[:, n_ctx:, :]
```

```python
import functools
import math

import numpy as np
import jax
import jax.numpy as jnp
from jax import lax
from jax.experimental import pallas as pl
from jax.experimental.pallas import tpu as pltpu

F32 = jnp.float32
BF16 = jnp.bfloat16
HIGHEST = lax.Precision.HIGHEST

D_MODEL = 1024
DEPTH = 4
GRID_W = 64
EPS = 1e-6
N_BRANCH = 3

SSD_HEADS = 16
SSD_HEAD_DIM = 64
SSD_INNER = SSD_HEADS * SSD_HEAD_DIM
SSD_GROUPS = 2
SSD_STATE = 128
SSD_CONV = 5
SSD_CHUNK = 128
SSD_CONV_DIM = SSD_INNER + 2 * SSD_GROUPS * SSD_STATE
SSD_GROUP_W = SSD_INNER // SSD_GROUPS

RET_HEADS = 4
RET_KEY_DIM = 64
RET_VAL_DIM = 128
RET_QK = RET_HEADS * RET_KEY_DIM
RET_INNER = RET_HEADS * RET_VAL_DIM
RET_CHUNK = 128
ROPE_BASE = 10000.0

RWKV_HEADS = 8
RWKV_HEAD_DIM = 64
RWKV_INNER = RWKV_HEADS * RWKV_HEAD_DIM
RWKV_DECAY_LORA = 64
RWKV_AAA_LORA = 64
RWKV_GATE_LORA = 128
RWKV_IN = 3 * RWKV_INNER + RWKV_DECAY_LORA + RWKV_AAA_LORA + RWKV_GATE_LORA
RWKV_LN_EPS = 64e-5
RWKV_CHUNK = 64

D_FF = 4 * D_MODEL

SUBLANES = 8
LANES = 128
VMEM_LIMIT_BYTES = 56 * 1024 * 1024

ROW_TILE = 256
NEG_BIG = -1e30


def _cparams(n_axes):
    return pltpu.CompilerParams(dimension_semantics=("arbitrary",) * n_axes,
                                vmem_limit_bytes=VMEM_LIMIT_BYTES)


def _silu(x):
    return x * jax.nn.sigmoid(x)


def _softplus(x):
    return jnp.maximum(x, 0.0) + jnp.log1p(jnp.exp(-jnp.abs(x)))


def _split3(x):
    x1 = x.astype(BF16)
    r1 = x - x1.astype(F32)
    x2 = r1.astype(BF16)
    r2 = r1 - x2.astype(F32)
    return x1, x2, r2.astype(BF16)


def _dot_sel_rhs(x, sel):
    x1, x2, x3 = _split3(x)
    return (jnp.dot(x1, sel, preferred_element_type=F32)
            + jnp.dot(x2, sel, preferred_element_type=F32)
            + jnp.dot(x3, sel, preferred_element_type=F32))


def _dot_sel_lhs(sel, x):
    x1, x2, x3 = _split3(x)
    return (jnp.dot(sel, x1, preferred_element_type=F32)
            + jnp.dot(sel, x2, preferred_element_type=F32)
            + jnp.dot(sel, x3, preferred_element_type=F32))


def _mm(a, b, passes):
    a1 = a.astype(BF16)
    b1 = b.astype(BF16)
    out = jnp.dot(a1, b1, preferred_element_type=F32)
    if passes == 1:
        return out
    a2 = (a - a1.astype(F32)).astype(BF16)
    b2 = (b - b1.astype(F32)).astype(BF16)
    return out + (jnp.dot(a1, b2, preferred_element_type=F32) + jnp.dot(a2, b1, preferred_element_type=F32))


def _dot_nt(a, b, precision=None):
    return lax.dot_general(a, b, (((1,), (1,)), ((), ())), preferred_element_type=F32, precision=precision)


def _dot_tn(a, b, precision=None):
    return lax.dot_general(a, b, (((0,), (0,)), ((), ())), preferred_element_type=F32, precision=precision)


def _chunk_of(d, i, ncx, nc):
    bwd = jnp.where(i < ncx, ncx - 1 - i, ncx + nc - 1 - i)
    return jnp.where(d == 0, i, bwd)


def _seg_edges(c, ncx, nc):
    seg_start = jnp.logical_or(c == 0, c == ncx)
    seg_end = jnp.logical_or(c == ncx - 1, c == nc - 1)
    return seg_start, seg_end


def _tri_mask(n, d, strict, reps=1):
    row = lax.broadcasted_iota(jnp.int32, (n, n * reps), 0)
    col = lax.broadcasted_iota(jnp.int32, (n, n * reps), 1) % n
    lead = jnp.where(d == 0, row - col, col - row)
    return lead > 0 if strict else lead >= 0


def _mod_kernel(c_ref, w_ref, b_ref, o_ref):
    s = _silu(c_ref[...])
    o_ref[0] = jnp.dot(s, w_ref[0], preferred_element_type=F32, precision=HIGHEST) + b_ref[0]


def modulation_all(cond, ada_w, ada_b):
    depth = ada_w.shape[0]
    tn = 1536
    return pl.pallas_call(
        _mod_kernel,
        out_shape=jax.ShapeDtypeStruct((depth, SUBLANES, 6 * D_MODEL), F32),
        grid=(depth, 6 * D_MODEL // tn),
        in_specs=[pl.BlockSpec((SUBLANES, D_MODEL), lambda l, j: (0, 0)),
                  pl.BlockSpec((1, D_MODEL, tn), lambda l, j: (l, 0, j)),
                  pl.BlockSpec((1, 1, tn), lambda l, j: (l, 0, j))],
        out_specs=pl.BlockSpec((1, SUBLANES, tn), lambda l, j: (l, 0, j)),
        compiler_params=_cparams(2),
        name="modulation",
    )(cond, ada_w, ada_b.reshape(depth, 1, 6 * D_MODEL))


def _rms(x, w):
    return x * lax.rsqrt(jnp.mean(x * x, axis=-1, keepdims=True) + EPS) * w


def _normmod_kernel(x_ref, nw_ref, mod_ref, h_ref):
    y = _rms(x_ref[0], nw_ref[...])
    shift = mod_ref[0, 0, 0:1, :]
    scale = mod_ref[0, 0, 1:2, :]
    h_ref[0] = (y * (1.0 + scale) + shift).astype(BF16)


def norm_modulate(x, nw, mod, n_ctx_tiles):
    b, ta, dm = x.shape
    return pl.pallas_call(
        _normmod_kernel,
        out_shape=jax.ShapeDtypeStruct((b, ta, dm), BF16),
        grid=(b, ta // ROW_TILE),
        in_specs=[pl.BlockSpec((1, ROW_TILE, dm), lambda bi, i: (bi, i, 0)),
                  pl.BlockSpec((1, dm), lambda bi, i: (0, 0)),
                  pl.BlockSpec((1, 1, SUBLANES, dm), lambda bi, i: (bi, (i >= n_ctx_tiles).astype(jnp.int32), 0, 0))],
        out_specs=pl.BlockSpec((1, ROW_TILE, dm), lambda bi, i: (bi, i, 0)),
        compiler_params=_cparams(2),
        name="norm_modulate",
    )(x, nw, mod)


def _mm_kernel(a_ref, w_ref, o_ref):
    o_ref[...] = jnp.dot(a_ref[...], w_ref[...], preferred_element_type=F32).astype(o_ref.dtype)


def matmul(a, w, tm, tn, out_dtype=F32):
    r, k = a.shape
    n = w.shape[1]
    assert r % tm == 0 and n % tn == 0
    return pl.pallas_call(
        _mm_kernel,
        out_shape=jax.ShapeDtypeStruct((r, n), out_dtype),
        grid=(n // tn, r // tm),
        in_specs=[pl.BlockSpec((tm, k), lambda j, i: (i, 0)),
                  pl.BlockSpec((k, tn), lambda j, i: (0, j))],
        out_specs=pl.BlockSpec((tm, tn), lambda j, i: (i, j)),
        compiler_params=_cparams(2),
        name="in_proj",
    )(a, w)


def _ssd_kernel(xbc_ref, xp_ref, xn_ref, dt_ref, cw_ref, cb_ref, dtb_ref, alog_ref, dskip_ref,
                hexp_ref, y_ref, s_ref, ext_ref, *, ncx, nc):
    d = pl.program_id(1)
    i = pl.program_id(2)
    c = _chunk_of(d, i, ncx, nc)
    seg_start, seg_end = _seg_edges(c, ncx, nc)
    lc = SSD_CHUNK
    hb = SUBLANES

    @pl.when(i == 0)
    def _():
        s_ref[...] = jnp.zeros_like(s_ref)

    ext_ref[0:hb, :] = jnp.where(seg_start, 0.0, xp_ref[0])
    ext_ref[hb:hb + lc, :] = xbc_ref[0]
    ext_ref[hb + lc:hb + lc + hb, :] = jnp.where(seg_end, 0.0, xn_ref[0])
    pad = SSD_CONV // 2
    acc = ext_ref[hb - pad:hb - pad + lc, :] * cw_ref[0:1, :]
    for j in range(1, SSD_CONV):
        acc = acc + ext_ref[hb - pad + j:hb - pad + j + lc, :] * cw_ref[j:j + 1, :]
    xbc = _silu(acc + cb_ref[...])
    xs = xbc[:, :SSD_INNER]
    bm = xbc[:, SSD_INNER:SSD_INNER + SSD_GROUPS * SSD_STATE]
    cm = xbc[:, SSD_INNER + SSD_GROUPS * SSD_STATE:]

    dt = _softplus(dt_ref[0] + dtb_ref[0])
    la = dt * (-jnp.exp(alog_ref[0]))
    tri = _tri_mask(lc, d, strict=False)
    acum = _dot_sel_lhs(tri.astype(BF16), la)
    atot = jnp.sum(la, axis=0, keepdims=True)
    acum_t = acum.T

    hexp = hexp_ref[...]
    dtx = _dot_sel_rhs(dt, hexp)
    eacx = _dot_sel_rhs(jnp.exp(acum), hexp)
    eendx = _dot_sel_rhs(jnp.exp(atot - acum), hexp)
    etotx = _dot_sel_rhs(jnp.broadcast_to(jnp.exp(atot), (SUBLANES, LANES)), hexp)[0:1, :]

    v = xs * dtx
    vb = v.astype(BF16)
    vend = (v * eendx).astype(BF16)
    skip = jnp.where(d == 0, 1.0, 0.0) * dskip_ref[...]
    hg = SSD_HEADS // SSD_GROUPS
    for g in range(SSD_GROUPS):
        gs = slice(g * SSD_STATE, (g + 1) * SSD_STATE)
        gw = slice(g * SSD_GROUP_W, (g + 1) * SSD_GROUP_W)
        cg = cm[:, gs].astype(BF16)
        bg = bm[:, gs].astype(BF16)
        scores = _dot_nt(cg, bg)
        s_prev = s_ref[g]
        y_off = jnp.dot(cg, s_prev.astype(BF16), preferred_element_type=F32) * eacx[:, gw]
        y_ref[0, 0, :, gw] = y_off + xs[:, gw] * skip[:, gw]
        for hh in range(hg):
            h = g * hg + hh
            diff = acum[:, h:h + 1] - acum_t[h:h + 1, :]
            dm = jnp.exp(jnp.where(tri, diff, NEG_BIG))
            hs = slice(h * SSD_HEAD_DIM, (h + 1) * SSD_HEAD_DIM)
            y_ref[0, 0, :, hs] += jnp.dot((scores * dm).astype(BF16), vb[:, hs], preferred_element_type=F32)
        s_ref[g] = s_prev * etotx[:, gw] + _dot_tn(bg, vend[:, gw])


def ssd_scan(u_xbc, u_zdt, conv_w, conv_b, dt_bias, a_log, dskip, hexp, ncx_tokens):
    b, ta, _ = u_xbc.shape
    lc = SSD_CHUNK
    nc = ta // lc
    ncx = ncx_tokens // lc
    nb = lc // SUBLANES
    nblk = ta // SUBLANES
    ch = functools.partial(_chunk_of, ncx=ncx, nc=nc)
    kern = functools.partial(_ssd_kernel, ncx=ncx, nc=nc)
    zdt_blk0 = SSD_INNER // LANES
    return pl.pallas_call(
        kern,
        out_shape=jax.ShapeDtypeStruct((2, b, ta, SSD_INNER), F32),
        grid=(b, 2, nc),
        in_specs=[
            pl.BlockSpec((1, lc, SSD_CONV_DIM), lambda bi, d, i: (bi, ch(d, i), 0)),
            pl.BlockSpec((1, SUBLANES, SSD_CONV_DIM), lambda bi, d, i: (bi, jnp.maximum(ch(d, i) * nb - 1, 0), 0)),
            pl.BlockSpec((1, SUBLANES, SSD_CONV_DIM), lambda bi, d, i: (bi, jnp.minimum((ch(d, i) + 1) * nb, nblk - 1), 0)),
            pl.BlockSpec((1, lc, LANES), lambda bi, d, i: (bi, ch(d, i), zdt_blk0 + d)),
            pl.BlockSpec((SUBLANES, SSD_CONV_DIM), lambda bi, d, i: (0, 0)),
            pl.BlockSpec((1, SSD_CONV_DIM), lambda bi, d, i: (0, 0)),
            pl.BlockSpec((1, 1, LANES), lambda bi, d, i: (d, 0, 0)),
            pl.BlockSpec((1, 1, LANES), lambda bi, d, i: (d, 0, 0)),
            pl.BlockSpec((1, SSD_INNER), lambda bi, d, i: (0, 0)),
            pl.BlockSpec((LANES, SSD_INNER), lambda bi, d, i: (0, 0)),
        ],
        out_specs=pl.BlockSpec((1, 1, lc, SSD_INNER), lambda bi, d, i: (d, bi, ch(d, i), 0)),
        scratch_shapes=[pltpu.VMEM((SSD_GROUPS, SSD_STATE, SSD_GROUP_W), F32),
                        pltpu.VMEM((lc + 2 * SUBLANES, SSD_CONV_DIM), F32)],
        compiler_params=_cparams(3),
        name="ssd_scan",
    )(u_xbc, u_xbc, u_xbc, u_zdt, conv_w, conv_b, dt_bias, a_log, dskip, hexp)


def _ret_kernel(qk_ref, v_ref, cos_ref, sin_ref, dec_ref, y_ref, s_ref, *, ncx, nc):
    d = pl.program_id(1)
    i = pl.program_id(2)
    lc = RET_CHUNK

    @pl.when(i == 0)
    def _():
        s_ref[...] = jnp.zeros_like(s_ref)

    qk = qk_ref[0]
    lane = lax.broadcasted_iota(jnp.int32, qk.shape, 1)
    half = RET_KEY_DIM // 2
    width = qk.shape[1]
    swapped = jnp.where((lane % RET_KEY_DIM) < half,
                        pltpu.roll(qk, width - half, axis=1), pltpu.roll(qk, half, axis=1))
    qk = qk * cos_ref[...] + swapped * sin_ref[...]
    q = qk[:, :RET_QK].astype(BF16)
    k = (qk[:, RET_QK:] * (RET_KEY_DIM ** -0.5)).astype(BF16)
    v = v_ref[0]

    lg = -_softplus(-dec_ref[0])
    row = lax.broadcasted_iota(jnp.int32, (lc, lc), 0)
    col = lax.broadcasted_iota(jnp.int32, (lc, lc), 1)
    tri = jnp.where(d == 0, row - col, col - row) >= 0
    dist = jnp.abs(row - col).astype(F32)
    t_idx = lax.broadcasted_iota(jnp.int32, (lc, 1), 0)
    pos = jnp.where(d == 0, t_idx, lc - 1 - t_idx).astype(F32)
    for h in range(RET_HEADS):
        lgh = lg[:, h:h + 1]
        ks = slice(h * RET_KEY_DIM, (h + 1) * RET_KEY_DIM)
        vs = slice(h * RET_VAL_DIM, (h + 1) * RET_VAL_DIM)
        qh, kh, vh = q[:, ks], k[:, ks], v[:, vs]
        dm = jnp.exp(jnp.where(tri, dist * lgh, NEG_BIG))
        scores = _dot_nt(qh, kh) * dm
        s_prev = s_ref[h]
        y = jnp.dot(scores.astype(BF16), vh.astype(BF16), preferred_element_type=F32)
        y = y + jnp.dot(qh, s_prev.astype(BF16), preferred_element_type=F32) * jnp.exp((pos + 1.0) * lgh)
        y_ref[0, 0, :, vs] = y
        vend = (vh * jnp.exp((lc - 1.0 - pos) * lgh)).astype(BF16)
        s_ref[h] = s_prev * jnp.exp(lc * lgh) + _dot_tn(kh, vend)


def ret_scan(u_ret, cosx, sinx, ret_decay, ncx_tokens):
    b, ta, _ = u_ret.shape
    lc = RET_CHUNK
    nc = ta // lc
    ncx = ncx_tokens // lc
    ch = functools.partial(_chunk_of, ncx=ncx, nc=nc)
    kern = functools.partial(_ret_kernel, ncx=ncx, nc=nc)
    return pl.pallas_call(
        kern,
        out_shape=jax.ShapeDtypeStruct((2, b, ta, RET_INNER), F32),
        grid=(b, 2, nc),
        in_specs=[
            pl.BlockSpec((1, lc, 2 * RET_QK), lambda bi, d, i: (bi, ch(d, i), 0)),
            pl.BlockSpec((1, lc, RET_INNER), lambda bi, d, i: (bi, ch(d, i), 1)),
            pl.BlockSpec((lc, 2 * RET_QK), lambda bi, d, i: (ch(d, i), 0)),
            pl.BlockSpec((lc, 2 * RET_QK), lambda bi, d, i: (ch(d, i), 0)),
            pl.BlockSpec((1, 1, LANES), lambda bi, d, i: (d, 0, 0)),
        ],
        out_specs=pl.BlockSpec((1, 1, lc, RET_INNER), lambda bi, d, i: (d, bi, ch(d, i), 0)),
        scratch_shapes=[pltpu.VMEM((RET_HEADS, RET_KEY_DIM, RET_VAL_DIM), F32)],
        compiler_params=_cparams(3),
        name="ret_scan",
    )(u_ret, u_ret, cosx, sinx, ret_decay)


RWKV_PASSES = {"p": 1, "mv": 1, "inv": 3, "sol": 1, "gh": 1, "ry": 1, "st": 1}


def _rwkv_kernel(u_ref, up_ref, un_ref, mix_ref, w0_ref, w2_ref, a0_ref, a2_ref, g2_ref, kk_ref, ka_ref,
                 rk_ref, bd_ref, out_ref, s_ref, *, ncx, nc):
    d = pl.program_id(1)
    i = pl.program_id(2)
    c = _chunk_of(d, i, ncx, nc)
    seg_start, seg_end = _seg_edges(c, ncx, nc)
    lc = RWKV_CHUNK
    nh, hd = RWKV_HEADS, RWKV_HEAD_DIM

    @pl.when(i == 0)
    def _():
        s_ref[...] = jnp.zeros_like(s_ref)

    u = u_ref[0]
    prow = jnp.where(seg_start, 0.0, up_ref[0, SUBLANES - 1:SUBLANES, :])
    nrow = jnp.where(seg_end, 0.0, un_ref[0, 0:1, :])
    t_idx = lax.broadcasted_iota(jnp.int32, (lc, 1), 0)
    prev = jnp.where(t_idx == 0, prow, pltpu.roll(u, 1, axis=0))
    nxt = jnp.where(t_idx == lc - 1, nrow, pltpu.roll(u, lc - 1, axis=0))
    u = u + mix_ref[...] * (0.5 * (prev + nxt) - u)
    ni = RWKV_INNER
    r, k, v = u[:, 0:ni], u[:, ni:2 * ni], u[:, 2 * ni:3 * ni]
    o = 3 * ni
    w_lo = u[:, o:o + RWKV_DECAY_LORA]
    a_lo = u[:, o + RWKV_DECAY_LORA:o + RWKV_DECAY_LORA + RWKV_AAA_LORA]
    g_lo = u[:, o + RWKV_DECAY_LORA + RWKV_AAA_LORA:]

    bd = bd_ref[...]
    w_log = -_softplus(-(w0_ref[0] + jnp.dot(jnp.tanh(w_lo), w2_ref[0], preferred_element_type=F32,
                                              precision=HIGHEST))) - 0.5
    logw = -jnp.exp(w_log)
    a_gate = jax.nn.sigmoid(a0_ref[0] + jnp.dot(a_lo, a2_ref[...], preferred_element_type=F32,
                                                precision=HIGHEST))
    kk = k * kk_ref[...]
    kd = k * (1.0 + (a_gate - 1.0) * ka_ref[...])
    s1, s2, _ = _split3(kk * kk)
    b1, b2, _ = _split3(r * kd * rk_ref[...])
    sums = jnp.dot(jnp.concatenate([s1, s2, b1, b2], axis=0), bd, preferred_element_type=F32)
    kk = kk / jnp.maximum(jnp.sqrt(sums[0:lc] + sums[lc:2 * lc]), 1e-12)
    bvec = kk * a_gate
    avec = -kk
    out_ref[0, 0, :, ni:2 * ni] = (sums[2 * lc:3 * lc] + sums[3 * lc:4 * lc]) * v
    out_ref[0, 0, :, 2 * ni:3 * ni] = jnp.dot(jax.nn.sigmoid(g_lo).astype(BF16), g2_ref[...],
                                              preferred_element_type=F32)

    incl = _tri_mask(lc, d, strict=False)
    l1, l2, l3 = _split3(logw)
    cw3 = jnp.dot(incl.astype(BF16), jnp.concatenate([l1, l2, l3], axis=1), preferred_element_type=F32)
    cw = cw3[:, 0:ni] + cw3[:, ni:2 * ni] + cw3[:, 2 * ni:3 * ni]
    cw_ex = cw - logw
    logw_t = logw.T
    wtot_col = jnp.exp(jnp.sum(logw_t, axis=1, keepdims=True))
    ctot = jnp.sum(logw, axis=0, keepdims=True)
    e_neg = jnp.exp(-cw)
    e_end = jnp.exp(ctot - cw)
    ar = jnp.concatenate([avec * jnp.exp(cw_ex), r * jnp.exp(cw)], axis=0)
    bk_t = jnp.concatenate([bvec * e_neg, kd * e_neg], axis=0).T
    bkend_t = jnp.concatenate([bvec * e_end, kd * e_end], axis=0).T

    eye_f = (lax.broadcasted_iota(jnp.int32, (lc, lc), 0)
             == lax.broadcasted_iota(jnp.int32, (lc, lc), 1)).astype(F32)
    strict2 = _tri_mask(lc, d, strict=True, reps=2)
    incl2 = _tri_mask(lc, d, strict=False, reps=2)
    zeros_v = jnp.zeros((lc, hd), F32)
    n_sq = int(math.log2(lc)) - 1
    heads = range(nh)
    pp = RWKV_PASSES

    hsl = [slice(h * hd, (h + 1) * hd) for h in heads]
    ar_h = [ar[:, s] for s in hsl]
    v_h = [v[:, s] for s in hsl]
    bkt_h = [bk_t[s, :] for s in hsl]
    bkendt_h = [bkend_t[s, :] for s in hsl]
    p = [_mm(ar_h[h], bkt_h[h], pp["p"]) for h in heads]
    m_a = [jnp.where(strict2, p[h][:lc], 0.0) for h in heads]
    m_r = [jnp.where(incl2, p[h][lc:], 0.0) for h in heads]
    mv = [_mm(m_a[h], jnp.concatenate([zeros_v, v_h[h]], axis=0), pp["mv"]) for h in heads]
    x = [m_a[h][:, :lc] for h in heads]
    t = [eye_f + x[h] for h in heads]
    for _ in range(n_sq):
        x = [_mm(x[h], x[h], pp["inv"]) for h in heads]
        t = [t[h] + _mm(t[h], x[h], pp["inv"]) for h in heads]
    a_p = [_mm(t[h], ar_h[h][:lc], pp["sol"]) for h in heads]
    u_l = [_mm(t[h], mv[h], pp["sol"]) for h in heads]
    rhs = [jnp.concatenate([u_l[h], v_h[h]], axis=0) for h in heads]
    g_k = [_mm(bkendt_h[h][:, :lc], a_p[h], pp["gh"]) for h in heads]
    h_mat = [_mm(bkendt_h[h], rhs[h], pp["gh"]) for h in heads]
    r_p = [ar_h[h][lc:] + _mm(m_r[h][:, :lc], a_p[h], pp["ry"]) for h in heads]
    y_l = [_mm(m_r[h], rhs[h], pp["ry"]) for h in heads]
    st = [s_ref[h] for h in heads]
    for h in heads:
        out_ref[0, 0, :, hsl[h]] = _mm(r_p[h], st[h], pp["st"]) + y_l[h]
    for h in heads:
        s_ref[h] = wtot_col[hsl[h], :] * st[h] + _mm(g_k[h], st[h], pp["st"]) + h_mat[h]


def rwkv_scan(u_rwkv, mix, w0, w2, a0, a2, g2, k_k, k_a, r_k, bd, ncx_tokens):
    b, ta, _ = u_rwkv.shape
    lc = RWKV_CHUNK
    nc = ta // lc
    ncx = ncx_tokens // lc
    nb = lc // SUBLANES
    nblk = ta // SUBLANES
    ni = RWKV_INNER
    ch = functools.partial(_chunk_of, ncx=ncx, nc=nc)
    kern = functools.partial(_rwkv_kernel, ncx=ncx, nc=nc)
    const2 = lambda bi, d, i: (0, 0)
    return pl.pallas_call(
        kern,
        out_shape=jax.ShapeDtypeStruct((2, b, ta, 3 * ni), F32),
        grid=(b, 2, nc),
        in_specs=[
            pl.BlockSpec((1, lc, RWKV_IN), lambda bi, d, i: (bi, ch(d, i), 0)),
            pl.BlockSpec((1, SUBLANES, RWKV_IN), lambda bi, d, i: (bi, jnp.maximum(ch(d, i) * nb - 1, 0), 0)),
            pl.BlockSpec((1, SUBLANES, RWKV_IN), lambda bi, d, i: (bi, jnp.minimum((ch(d, i) + 1) * nb, nblk - 1), 0)),
            pl.BlockSpec((1, RWKV_IN), const2),
            pl.BlockSpec((1, 1, ni), lambda bi, d, i: (d, 0, 0)),
            pl.BlockSpec((1, RWKV_DECAY_LORA, ni), lambda bi, d, i: (d, 0, 0)),
            pl.BlockSpec((1, 1, ni), lambda bi, d, i: (d, 0, 0)),
            pl.BlockSpec((RWKV_AAA_LORA, ni), const2),
            pl.BlockSpec((RWKV_GATE_LORA, ni), const2),
            pl.BlockSpec((1, ni), const2),
            pl.BlockSpec((1, ni), const2),
            pl.BlockSpec((1, ni), const2),
            pl.BlockSpec((ni, ni), const2),
        ],
        out_specs=pl.BlockSpec((1, 1, lc, 3 * ni), lambda bi, d, i: (d, bi, ch(d, i), 0)),
        scratch_shapes=[pltpu.VMEM((RWKV_HEADS, RWKV_HEAD_DIM, RWKV_HEAD_DIM), F32)],
        compiler_params=_cparams(3),
        name="rwkv_scan",
    )(u_rwkv, u_rwkv, u_rwkv, mix, w0, w2, a0, a2, g2, k_k, k_a, r_k, bd)


def _merge_kernel(ssd_ref, z_ref, ret_ref, rg_ref, rw_ref, gate_ref, x_ref, mod_ref, nw_ref, ssdnw_ref,
                  lnw_ref, lnb_ref, bd_ref, wso_ref, wro_ref, wwo_ref, wo_ref, o_ref):
    ys = (ssd_ref[0, 0] + ssd_ref[1, 0]) * _silu(z_ref[0][:, :SSD_INNER])
    parts = []
    for g in range(SSD_GROUPS):
        yg = ys[:, g * SSD_GROUP_W:(g + 1) * SSD_GROUP_W]
        parts.append(yg * lax.rsqrt(jnp.mean(yg * yg, axis=-1, keepdims=True) + EPS))
    ys = jnp.concatenate(parts, axis=1) * ssdnw_ref[...]
    o_ssd = jnp.dot(ys.astype(BF16), wso_ref[...], preferred_element_type=F32)

    yr = ret_ref[0, 0] + ret_ref[1, 0]
    parts = []
    for h in range(RET_HEADS):
        yh = yr[:, h * RET_VAL_DIM:(h + 1) * RET_VAL_DIM]
        yc = yh - jnp.mean(yh, axis=-1, keepdims=True)
        parts.append(yc * lax.rsqrt(jnp.mean(yc * yc, axis=-1, keepdims=True) + EPS))
    yr = jnp.concatenate(parts, axis=1) * _silu(rg_ref[0])
    o_ret = jnp.dot(yr.astype(BF16), wro_ref[...], preferred_element_type=F32)

    ni = RWKV_INNER
    rw0 = rw_ref[0, 0]
    rw1 = rw_ref[1, 0]
    yw = rw0[:, :ni] + rw1[:, :ni]
    bd = bd_ref[...]
    inv_hd = 1.0 / RWKV_HEAD_DIM
    yc = yw - _dot_sel_rhs(yw, bd) * inv_hd
    var = _dot_sel_rhs(yc * yc, bd) * inv_hd
    yw = yc * lax.rsqrt(var + RWKV_LN_EPS) * lnw_ref[...] + lnb_ref[...]
    yw = (yw + rw0[:, ni:2 * ni] + rw1[:, ni:2 * ni]) * rw0[:, 2 * ni:3 * ni]
    o_rw = jnp.dot(yw.astype(BF16), wwo_ref[...], preferred_element_type=F32)

    gate = gate_ref[0]
    merged = (jax.nn.sigmoid(gate[:, :D_MODEL]) * o_ssd
              + jax.nn.sigmoid(gate[:, D_MODEL:2 * D_MODEL]) * o_ret
              + jax.nn.sigmoid(gate[:, 2 * D_MODEL:]) * o_rw)
    yx = jnp.dot(merged.astype(BF16), wo_ref[...], preferred_element_type=F32)
    g1 = mod_ref[0, 0, 2:3, :]
    o_ref[0] = x_ref[0] + g1 * _rms(yx, nw_ref[...])


def merge_out(ssd_y, u_zdt, ret_y, u_ret, rw_y, u_gate, x, mod, nw1, ssd_nw, ln_w, ln_b, bd,
              w_ssd_out, w_ret_out, w_rwkv_out, w_out, n_ctx_tiles):
    b, ta, dm = x.shape
    tm = ROW_TILE
    row = lambda bi, i: (bi, i, 0)
    both = lambda bi, i: (0, bi, i, 0)
    const2 = lambda bi, i: (0, 0)
    return pl.pallas_call(
        _merge_kernel,
        out_shape=jax.ShapeDtypeStruct((b, ta, dm), F32),
        grid=(b, ta // tm),
        in_specs=[
            pl.BlockSpec((2, 1, tm, SSD_INNER), both),
            pl.BlockSpec((1, tm, u_zdt.shape[2]), row),
            pl.BlockSpec((2, 1, tm, RET_INNER), both),
            pl.BlockSpec((1, tm, RET_INNER), lambda bi, i: (bi, i, 2)),
            pl.BlockSpec((2, 1, tm, 3 * RWKV_INNER), both),
            pl.BlockSpec((1, tm, N_BRANCH * dm), row),
            pl.BlockSpec((1, tm, dm), row),
            pl.BlockSpec((1, 1, SUBLANES, dm), lambda bi, i: (bi, (i >= n_ctx_tiles).astype(jnp.int32), 0, 0)),
            pl.BlockSpec((1, dm), const2),
            pl.BlockSpec((1, SSD_INNER), const2),
            pl.BlockSpec((1, RWKV_INNER), const2),
            pl.BlockSpec((1, RWKV_INNER), const2),
            pl.BlockSpec((RWKV_INNER, RWKV_INNER), const2),
            pl.BlockSpec((SSD_INNER, dm), const2),
            pl.BlockSpec((RET_INNER, dm), const2),
            pl.BlockSpec((RWKV_INNER, dm), const2),
            pl.BlockSpec((dm, dm), const2),
        ],
        out_specs=pl.BlockSpec((1, tm, dm), row),
        compiler_params=_cparams(2),
        name="merge_out",
    )(ssd_y, u_zdt, ret_y, u_ret, rw_y, u_gate, x, mod, nw1, ssd_nw, ln_w, ln_b, bd,
      w_ssd_out, w_ret_out, w_rwkv_out, w_out)


def _mlp_kernel(x_ref, mod_ref, nw2_ref, nw3_ref, w1_ref, w2_ref, o_ref, h_ref, acc_ref):
    j = pl.program_id(2)

    @pl.when(j == 0)
    def _():
        y = _rms(x_ref[0], nw2_ref[...])
        h_ref[...] = (y * (1.0 + mod_ref[0, 0, 4:5, :]) + mod_ref[0, 0, 3:4, :]).astype(BF16)
        acc_ref[...] = jnp.zeros_like(acc_ref)

    hid = jnp.dot(h_ref[...], w1_ref[...], preferred_element_type=F32)
    hid = jnp.square(jnp.maximum(hid, 0.0))
    acc_ref[...] += jnp.dot(hid.astype(BF16), w2_ref[...], preferred_element_type=F32)

    @pl.when(j == pl.num_programs(2) - 1)
    def _():
        g2 = mod_ref[0, 0, 5:6, :]
        o_ref[0] = x_ref[0] + g2 * _rms(acc_ref[...], nw3_ref[...])


def mlp_block(x, mod, nw2, nw3, w1, w2, n_ctx_tiles, ff_tile=D_FF):
    b, ta, dm = x.shape
    tm = ROW_TILE
    row = lambda bi, i, j: (bi, i, 0)
    const2 = lambda bi, i, j: (0, 0)
    return pl.pallas_call(
        _mlp_kernel,
        out_shape=jax.ShapeDtypeStruct((b, ta, dm), F32),
        grid=(b, ta // tm, D_FF // ff_tile),
        in_specs=[
            pl.BlockSpec((1, tm, dm), row),
            pl.BlockSpec((1, 1, SUBLANES, dm), lambda bi, i, j: (bi, (i >= n_ctx_tiles).astype(jnp.int32), 0, 0)),
            pl.BlockSpec((1, dm), const2),
            pl.BlockSpec((1, dm), const2),
            pl.BlockSpec((dm, ff_tile), lambda bi, i, j: (0, j)),
            pl.BlockSpec((ff_tile, dm), lambda bi, i, j: (j, 0)),
        ],
        out_specs=pl.BlockSpec((1, tm, dm), row),
        scratch_shapes=[pltpu.VMEM((tm, dm), BF16), pltpu.VMEM((tm, dm), F32)],
        compiler_params=_cparams(3),
        name="mlp",
    )(x, mod, nw2, nw3, w1, w2)


def _rope_tables(n_ctx, n_lat):
    rows = n_lat // GRID_W
    row = np.repeat(np.arange(rows), GRID_W).astype(np.float32)
    col = np.tile(np.arange(GRID_W), rows).astype(np.float32)
    n_freq = RET_KEY_DIM // 4
    inv = jnp.power(ROPE_BASE, -jnp.arange(n_freq, dtype=F32) / n_freq)
    ang = jnp.concatenate([jnp.asarray(row)[:, None] * inv, jnp.asarray(col)[:, None] * inv], axis=-1)
    cos = jnp.concatenate([jnp.ones((n_ctx, RET_KEY_DIM // 2), F32), jnp.cos(ang)], axis=0)
    sin = jnp.concatenate([jnp.zeros((n_ctx, RET_KEY_DIM // 2), F32), jnp.sin(ang)], axis=0)
    cos_h = jnp.concatenate([cos, cos], axis=1)
    sin_h = jnp.concatenate([-sin, sin], axis=1)
    reps = 2 * RET_HEADS
    return jnp.tile(cos_h, (1, reps)), jnp.tile(sin_h, (1, reps))


def _pad_lanes(a, width=LANES):
    return jnp.pad(a, [(0, 0)] * (a.ndim - 1) + [(0, width - a.shape[-1])])


def kernel(x, c, ctx, c_ctx, norm_w, ada_w, ada_b, w_in, ssd_conv_w, ssd_conv_b, ssd_dt_bias, ssd_a_log,
           ssd_d, ssd_norm_w, ret_decay, rwkv_mix, rwkv_w0, rwkv_w2, rwkv_a0, rwkv_a2, rwkv_g2, rwkv_k_k,
           rwkv_k_a, rwkv_r_k, rwkv_lnx_w, rwkv_lnx_b, w_ssd_out, w_ret_out, w_rwkv_out, w_out, mlp_w1, mlp_w2):
    b, n_lat, dm = x.shape
    n_ctx = ctx.shape[1]
    ta = n_ctx + n_lat
    depth = norm_w.shape[0]
    n_ctx_tiles = n_ctx // ROW_TILE
    assert n_ctx % ROW_TILE == 0 and n_lat % ROW_TILE == 0 and b + 1 <= SUBLANES

    hexp = np.zeros((LANES, SSD_INNER), np.float32)
    for h in range(SSD_HEADS):
        hexp[h, h * SSD_HEAD_DIM:(h + 1) * SSD_HEAD_DIM] = 1.0
    hexp = jnp.asarray(hexp, BF16)
    bd = jnp.asarray(np.kron(np.eye(RWKV_HEADS, dtype=np.float32),
                             np.ones((RWKV_HEAD_DIM, RWKV_HEAD_DIM), np.float32)), BF16)
    cosx, sinx = _rope_tables(n_ctx, n_lat)

    cond = jnp.zeros((SUBLANES, dm), F32).at[:b].set(c).at[b].set(c_ctx)
    mod_all = modulation_all(cond, ada_w, ada_b)

    sizes = (N_BRANCH * dm, SSD_INNER, SSD_CONV_DIM, 2 * SSD_HEADS, RET_QK, RET_QK, RET_INNER, RET_INNER, RWKV_IN)
    offs = np.concatenate([[0], np.cumsum(sizes)])
    perm = np.concatenate([np.concatenate([np.arange(0, RET_KEY_DIM, 2), np.arange(1, RET_KEY_DIM, 2)]) + h * RET_KEY_DIM
                           for h in range(RET_HEADS)])

    xall = jnp.concatenate([ctx, x], axis=1)
    tm_mm = 512 if (b * ta) % 512 == 0 else ROW_TILE
    for l in range(depth):
        wl = w_in[l]
        seg = [wl[:, offs[j]:offs[j + 1]] for j in range(len(sizes))]
        w_gate = seg[0].astype(BF16)
        w_zdt = jnp.concatenate([seg[1], _pad_lanes(seg[3][:, :SSD_HEADS]), _pad_lanes(seg[3][:, SSD_HEADS:])],
                                axis=1).astype(BF16)
        w_xbc = seg[2].astype(BF16)
        w_ret = jnp.concatenate([seg[4][:, perm], seg[5][:, perm], seg[6], seg[7]], axis=1).astype(BF16)
        w_rw = seg[8].astype(BF16)

        m = mod_all[l].reshape(SUBLANES, 6, dm)
        m = jnp.pad(m, ((0, 0), (0, SUBLANES - 6), (0, 0)))
        mod = jnp.stack([jnp.broadcast_to(m[b], (b,) + m.shape[1:]), m[:b]], axis=1)
        nw = norm_w[l]

        h = norm_modulate(xall, nw[0:1], mod, n_ctx_tiles).reshape(b * ta, dm)
        u_gate = matmul(h, w_gate, tm_mm, 1024).reshape(b, ta, -1)
        u_zdt = matmul(h, w_zdt, tm_mm, w_zdt.shape[1]).reshape(b, ta, -1)
        u_xbc = matmul(h, w_xbc, tm_mm, SSD_CONV_DIM).reshape(b, ta, -1)
        u_ret = matmul(h, w_ret, tm_mm, w_ret.shape[1]).reshape(b, ta, -1)
        u_rw = matmul(h, w_rw, tm_mm, RWKV_IN).reshape(b, ta, -1)

        conv_w = jnp.pad(ssd_conv_w[l], ((0, SUBLANES - SSD_CONV), (0, 0)))
        dskip = jnp.repeat(ssd_d[l], SSD_HEAD_DIM)[None, :]
        ssd_y = ssd_scan(u_xbc, u_zdt, conv_w, ssd_conv_b[l][None, :],
                         _pad_lanes(ssd_dt_bias[l])[:, None, :], _pad_lanes(ssd_a_log[l])[:, None, :],
                         dskip, hexp, n_ctx)
        ret_y = ret_scan(u_ret, cosx, sinx, _pad_lanes(ret_decay[l])[:, None, :], n_ctx)
        rw_y = rwkv_scan(u_rw, rwkv_mix[l][None, :], rwkv_w0[l][:, None, :], rwkv_w2[l], rwkv_a0[l][:, None, :],
                         rwkv_a2[l], rwkv_g2[l].astype(BF16), rwkv_k_k[l][None, :], rwkv_k_a[l][None, :],
                         rwkv_r_k[l].reshape(1, RWKV_INNER), bd, n_ctx)

        xall = merge_out(ssd_y, u_zdt, ret_y, u_ret, rw_y, u_gate, xall, mod, nw[1:2], ssd_norm_w[l][None, :],
                         rwkv_lnx_w[l][None, :], rwkv_lnx_b[l][None, :], bd,
                         w_ssd_out[l].astype(BF16), w_ret_out[l].astype(BF16), w_rwkv_out[l].astype(BF16),
                         w_out[l].astype(BF16), n_ctx_tiles)
        xall = mlp_block(xall, mod, nw[2:3], nw[3:4], mlp_w1[l].astype(BF16), mlp_w2[l].astype(BF16), n_ctx_tiles)
    return xall[:, n_ctx:, :]
```

```python
import functools
import math

import numpy as np
import jax
import jax.numpy as jnp
from jax import lax
from jax.experimental import pallas as pl
from jax.experimental.pallas import tpu as pltpu

F32 = jnp.float32
BF16 = jnp.bfloat16
HIGHEST = lax.Precision.HIGHEST

D_MODEL = 1024
DEPTH = 4
GRID_W = 64
EPS = 1e-6
N_BRANCH = 3

SSD_HEADS = 16
SSD_HEAD_DIM = 64
SSD_INNER = SSD_HEADS * SSD_HEAD_DIM
SSD_GROUPS = 2
SSD_STATE = 128
SSD_CONV = 5
SSD_CHUNK = 128
SSD_CONV_DIM = SSD_INNER + 2 * SSD_GROUPS * SSD_STATE
SSD_GROUP_W = SSD_INNER // SSD_GROUPS

RET_HEADS = 4
RET_KEY_DIM = 64
RET_VAL_DIM = 128
RET_QK = RET_HEADS * RET_KEY_DIM
RET_INNER = RET_HEADS * RET_VAL_DIM
RET_CHUNK = 128
ROPE_BASE = 10000.0

RWKV_HEADS = 8
RWKV_HEAD_DIM = 64
RWKV_INNER = RWKV_HEADS * RWKV_HEAD_DIM
RWKV_DECAY_LORA = 64
RWKV_AAA_LORA = 64
RWKV_GATE_LORA = 128
RWKV_IN = 3 * RWKV_INNER + RWKV_DECAY_LORA + RWKV_AAA_LORA + RWKV_GATE_LORA
RWKV_LN_EPS = 64e-5
RWKV_CHUNK = 64

D_FF = 4 * D_MODEL

SUBLANES = 8
LANES = 128
VMEM_LIMIT_BYTES = 56 * 1024 * 1024

ROW_TILE = 256
NEG_BIG = -1e30


def _cparams(n_axes):
    return pltpu.CompilerParams(dimension_semantics=("arbitrary",) * n_axes,
                                vmem_limit_bytes=VMEM_LIMIT_BYTES)


def _silu(x):
    return x * jax.nn.sigmoid(x)


def _softplus(x):
    return jnp.maximum(x, 0.0) + jnp.log1p(jnp.exp(-jnp.abs(x)))


def _split3(x):
    x1 = x.astype(BF16)
    r1 = x - x1.astype(F32)
    x2 = r1.astype(BF16)
    r2 = r1 - x2.astype(F32)
    return x1, x2, r2.astype(BF16)


def _dot_sel_rhs(x, sel):
    x1, x2, x3 = _split3(x)
    return (jnp.dot(x1, sel, preferred_element_type=F32)
            + jnp.dot(x2, sel, preferred_element_type=F32)
            + jnp.dot(x3, sel, preferred_element_type=F32))


def _dot_sel_lhs(sel, x):
    x1, x2, x3 = _split3(x)
    return (jnp.dot(sel, x1, preferred_element_type=F32)
            + jnp.dot(sel, x2, preferred_element_type=F32)
            + jnp.dot(sel, x3, preferred_element_type=F32))


def _dot_nt(a, b, precision=None):
    return lax.dot_general(a, b, (((1,), (1,)), ((), ())), preferred_element_type=F32, precision=precision)


def _dot_tn(a, b, precision=None):
    return lax.dot_general(a, b, (((0,), (0,)), ((), ())), preferred_element_type=F32, precision=precision)


def _chunk_of(d, i, ncx, nc):
    bwd = jnp.where(i < ncx, ncx - 1 - i, ncx + nc - 1 - i)
    return jnp.where(d == 0, i, bwd)


def _seg_edges(c, ncx, nc):
    seg_start = jnp.logical_or(c == 0, c == ncx)
    seg_end = jnp.logical_or(c == ncx - 1, c == nc - 1)
    return seg_start, seg_end


def _tri_mask(n, d, strict, reps=1):
    row = lax.broadcasted_iota(jnp.int32, (n, n * reps), 0)
    col = lax.broadcasted_iota(jnp.int32, (n, n * reps), 1) % n
    lead = jnp.where(d == 0, row - col, col - row)
    return lead > 0 if strict else lead >= 0


def _mod_kernel(c_ref, w_ref, b_ref, o_ref):
    s = _silu(c_ref[...])
    o_ref[0] = jnp.dot(s, w_ref[0], preferred_element_type=F32, precision=HIGHEST) + b_ref[0]


def modulation_all(cond, ada_w, ada_b):
    depth = ada_w.shape[0]
    tn = 1536
    return pl.pallas_call(
        _mod_kernel,
        out_shape=jax.ShapeDtypeStruct((depth, SUBLANES, 6 * D_MODEL), F32),
        grid=(depth, 6 * D_MODEL // tn),
        in_specs=[pl.BlockSpec((SUBLANES, D_MODEL), lambda l, j: (0, 0)),
                  pl.BlockSpec((1, D_MODEL, tn), lambda l, j: (l, 0, j)),
                  pl.BlockSpec((1, 1, tn), lambda l, j: (l, 0, j))],
        out_specs=pl.BlockSpec((1, SUBLANES, tn), lambda l, j: (l, 0, j)),
        compiler_params=_cparams(2),
        name="modulation",
    )(cond, ada_w, ada_b.reshape(depth, 1, 6 * D_MODEL))


def _rms(x, w):
    return x * lax.rsqrt(jnp.mean(x * x, axis=-1, keepdims=True) + EPS) * w


def _normmod_kernel(x_ref, nw_ref, mod_ref, h_ref):
    y = _rms(x_ref[0], nw_ref[...])
    shift = mod_ref[0, 0, 0:1, :]
    scale = mod_ref[0, 0, 1:2, :]
    h_ref[0] = (y * (1.0 + scale) + shift).astype(BF16)


def norm_modulate(x, nw, mod, n_ctx_tiles):
    b, ta, dm = x.shape
    return pl.pallas_call(
        _normmod_kernel,
        out_shape=jax.ShapeDtypeStruct((b, ta, dm), BF16),
        grid=(b, ta // ROW_TILE),
        in_specs=[pl.BlockSpec((1, ROW_TILE, dm), lambda bi, i: (bi, i, 0)),
                  pl.BlockSpec((1, dm), lambda bi, i: (0, 0)),
                  pl.BlockSpec((1, 1, SUBLANES, dm), lambda bi, i: (bi, (i >= n_ctx_tiles).astype(jnp.int32), 0, 0))],
        out_specs=pl.BlockSpec((1, ROW_TILE, dm), lambda bi, i: (bi, i, 0)),
        compiler_params=_cparams(2),
        name="norm_modulate",
    )(x, nw, mod)


def _mm_kernel(a_ref, w_ref, o_ref):
    o_ref[...] = jnp.dot(a_ref[...], w_ref[...], preferred_element_type=F32).astype(o_ref.dtype)


def matmul(a, w, tm, tn, out_dtype=F32):
    r, k = a.shape
    n = w.shape[1]
    assert r % tm == 0 and n % tn == 0
    return pl.pallas_call(
        _mm_kernel,
        out_shape=jax.ShapeDtypeStruct((r, n), out_dtype),
        grid=(n // tn, r // tm),
        in_specs=[pl.BlockSpec((tm, k), lambda j, i: (i, 0)),
                  pl.BlockSpec((k, tn), lambda j, i: (0, j))],
        out_specs=pl.BlockSpec((tm, tn), lambda j, i: (i, j)),
        compiler_params=_cparams(2),
        name="in_proj",
    )(a, w)


def _ssd_kernel(xbc_ref, xp_ref, xn_ref, dt_ref, cw_ref, cb_ref, dtb_ref, alog_ref, dskip_ref,
                hexp_ref, y_ref, s_ref, ext_ref, *, ncx, nc):
    d = pl.program_id(1)
    i = pl.program_id(2)
    c = _chunk_of(d, i, ncx, nc)
    seg_start, seg_end = _seg_edges(c, ncx, nc)
    lc = SSD_CHUNK
    hb = SUBLANES

    @pl.when(i == 0)
    def _():
        s_ref[...] = jnp.zeros_like(s_ref)

    ext_ref[0:hb, :] = jnp.where(seg_start, 0.0, xp_ref[0])
    ext_ref[hb:hb + lc, :] = xbc_ref[0]
    ext_ref[hb + lc:hb + lc + hb, :] = jnp.where(seg_end, 0.0, xn_ref[0])
    pad = SSD_CONV // 2
    acc = ext_ref[hb - pad:hb - pad + lc, :] * cw_ref[0:1, :]
    for j in range(1, SSD_CONV):
        acc = acc + ext_ref[hb - pad + j:hb - pad + j + lc, :] * cw_ref[j:j + 1, :]
    xbc = _silu(acc + cb_ref[...])
    xs = xbc[:, :SSD_INNER]
    bm = xbc[:, SSD_INNER:SSD_INNER + SSD_GROUPS * SSD_STATE]
    cm = xbc[:, SSD_INNER + SSD_GROUPS * SSD_STATE:]

    dt = _softplus(dt_ref[0] + dtb_ref[0])
    la = dt * (-jnp.exp(alog_ref[0]))
    tri = _tri_mask(lc, d, strict=False)
    acum = _dot_sel_lhs(tri.astype(BF16), la)
    atot = jnp.sum(la, axis=0, keepdims=True)
    acum_t = acum.T

    hexp = hexp_ref[...]
    dtx = _dot_sel_rhs(dt, hexp)
    eacx = _dot_sel_rhs(jnp.exp(acum), hexp)
    eendx = _dot_sel_rhs(jnp.exp(atot - acum), hexp)
    etotx = _dot_sel_rhs(jnp.broadcast_to(jnp.exp(atot), (SUBLANES, LANES)), hexp)[0:1, :]

    v = xs * dtx
    vb = v.astype(BF16)
    vend = (v * eendx).astype(BF16)
    skip = jnp.where(d == 0, 1.0, 0.0) * dskip_ref[...]
    hg = SSD_HEADS // SSD_GROUPS
    for g in range(SSD_GROUPS):
        gs = slice(g * SSD_STATE, (g + 1) * SSD_STATE)
        gw = slice(g * SSD_GROUP_W, (g + 1) * SSD_GROUP_W)
        cg = cm[:, gs].astype(BF16)
        bg = bm[:, gs].astype(BF16)
        scores = _dot_nt(cg, bg)
        s_prev = s_ref[g]
        y_off = jnp.dot(cg, s_prev.astype(BF16), preferred_element_type=F32) * eacx[:, gw]
        y_ref[0, 0, :, gw] = y_off + xs[:, gw] * skip[:, gw]
        for hh in range(hg):
            h = g * hg + hh
            diff = acum[:, h:h + 1] - acum_t[h:h + 1, :]
            dm = jnp.exp(jnp.where(tri, diff, NEG_BIG))
            hs = slice(h * SSD_HEAD_DIM, (h + 1) * SSD_HEAD_DIM)
            y_ref[0, 0, :, hs] += jnp.dot((scores * dm).astype(BF16), vb[:, hs], preferred_element_type=F32)
        s_ref[g] = s_prev * etotx[:, gw] + _dot_tn(bg, vend[:, gw])


def ssd_scan(u_xbc, u_zdt, conv_w, conv_b, dt_bias, a_log, dskip, hexp, ncx_tokens):
    b, ta, _ = u_xbc.shape
    lc = SSD_CHUNK
    nc = ta // lc
    ncx = ncx_tokens // lc
    nb = lc // SUBLANES
    nblk = ta // SUBLANES
    ch = functools.partial(_chunk_of, ncx=ncx, nc=nc)
    kern = functools.partial(_ssd_kernel, ncx=ncx, nc=nc)
    zdt_blk0 = SSD_INNER // LANES
    return pl.pallas_call(
        kern,
        out_shape=jax.ShapeDtypeStruct((2, b, ta, SSD_INNER), F32),
        grid=(b, 2, nc),
        in_specs=[
            pl.BlockSpec((1, lc, SSD_CONV_DIM), lambda bi, d, i: (bi, ch(d, i), 0)),
            pl.BlockSpec((1, SUBLANES, SSD_CONV_DIM), lambda bi, d, i: (bi, jnp.maximum(ch(d, i) * nb - 1, 0), 0)),
            pl.BlockSpec((1, SUBLANES, SSD_CONV_DIM), lambda bi, d, i: (bi, jnp.minimum((ch(d, i) + 1) * nb, nblk - 1), 0)),
            pl.BlockSpec((1, lc, LANES), lambda bi, d, i: (bi, ch(d, i), zdt_blk0 + d)),
            pl.BlockSpec((SUBLANES, SSD_CONV_DIM), lambda bi, d, i: (0, 0)),
            pl.BlockSpec((1, SSD_CONV_DIM), lambda bi, d, i: (0, 0)),
            pl.BlockSpec((1, 1, LANES), lambda bi, d, i: (d, 0, 0)),
            pl.BlockSpec((1, 1, LANES), lambda bi, d, i: (d, 0, 0)),
            pl.BlockSpec((1, SSD_INNER), lambda bi, d, i: (0, 0)),
            pl.BlockSpec((LANES, SSD_INNER), lambda bi, d, i: (0, 0)),
        ],
        out_specs=pl.BlockSpec((1, 1, lc, SSD_INNER), lambda bi, d, i: (d, bi, ch(d, i), 0)),
        scratch_shapes=[pltpu.VMEM((SSD_GROUPS, SSD_STATE, SSD_GROUP_W), F32),
                        pltpu.VMEM((lc + 2 * SUBLANES, SSD_CONV_DIM), F32)],
        compiler_params=_cparams(3),
        name="ssd_scan",
    )(u_xbc, u_xbc, u_xbc, u_zdt, conv_w, conv_b, dt_bias, a_log, dskip, hexp)


def _ret_kernel(qk_ref, v_ref, cos_ref, sin_ref, dec_ref, y_ref, s_ref, *, ncx, nc):
    d = pl.program_id(1)
    i = pl.program_id(2)
    lc = RET_CHUNK

    @pl.when(i == 0)
    def _():
        s_ref[...] = jnp.zeros_like(s_ref)

    qk = qk_ref[0]
    lane = lax.broadcasted_iota(jnp.int32, qk.shape, 1)
    half = RET_KEY_DIM // 2
    width = qk.shape[1]
    swapped = jnp.where((lane % RET_KEY_DIM) < half,
                        pltpu.roll(qk, width - half, axis=1), pltpu.roll(qk, half, axis=1))
    qk = qk * cos_ref[...] + swapped * sin_ref[...]
    q = qk[:, :RET_QK].astype(BF16)
    k = (qk[:, RET_QK:] * (RET_KEY_DIM ** -0.5)).astype(BF16)
    v = v_ref[0]

    lg = -_softplus(-dec_ref[0])
    row = lax.broadcasted_iota(jnp.int32, (lc, lc), 0)
    col = lax.broadcasted_iota(jnp.int32, (lc, lc), 1)
    tri = jnp.where(d == 0, row - col, col - row) >= 0
    dist = jnp.abs(row - col).astype(F32)
    t_idx = lax.broadcasted_iota(jnp.int32, (lc, 1), 0)
    pos = jnp.where(d == 0, t_idx, lc - 1 - t_idx).astype(F32)
    for h in range(RET_HEADS):
        lgh = lg[:, h:h + 1]
        ks = slice(h * RET_KEY_DIM, (h + 1) * RET_KEY_DIM)
        vs = slice(h * RET_VAL_DIM, (h + 1) * RET_VAL_DIM)
        qh, kh, vh = q[:, ks], k[:, ks], v[:, vs]
        dm = jnp.exp(jnp.where(tri, dist * lgh, NEG_BIG))
        scores = _dot_nt(qh, kh) * dm
        s_prev = s_ref[h]
        y = jnp.dot(scores.astype(BF16), vh.astype(BF16), preferred_element_type=F32)
        y = y + jnp.dot(qh, s_prev.astype(BF16), preferred_element_type=F32) * jnp.exp((pos + 1.0) * lgh)
        y_ref[0, 0, :, vs] = y
        vend = (vh * jnp.exp((lc - 1.0 - pos) * lgh)).astype(BF16)
        s_ref[h] = s_prev * jnp.exp(lc * lgh) + _dot_tn(kh, vend)


def ret_scan(u_ret, cosx, sinx, ret_decay, ncx_tokens):
    b, ta, _ = u_ret.shape
    lc = RET_CHUNK
    nc = ta // lc
    ncx = ncx_tokens // lc
    ch = functools.partial(_chunk_of, ncx=ncx, nc=nc)
    kern = functools.partial(_ret_kernel, ncx=ncx, nc=nc)
    return pl.pallas_call(
        kern,
        out_shape=jax.ShapeDtypeStruct((2, b, ta, RET_INNER), F32),
        grid=(b, 2, nc),
        in_specs=[
            pl.BlockSpec((1, lc, 2 * RET_QK), lambda bi, d, i: (bi, ch(d, i), 0)),
            pl.BlockSpec((1, lc, RET_INNER), lambda bi, d, i: (bi, ch(d, i), 1)),
            pl.BlockSpec((lc, 2 * RET_QK), lambda bi, d, i: (ch(d, i), 0)),
            pl.BlockSpec((lc, 2 * RET_QK), lambda bi, d, i: (ch(d, i), 0)),
            pl.BlockSpec((1, 1, LANES), lambda bi, d, i: (d, 0, 0)),
        ],
        out_specs=pl.BlockSpec((1, 1, lc, RET_INNER), lambda bi, d, i: (d, bi, ch(d, i), 0)),
        scratch_shapes=[pltpu.VMEM((RET_HEADS, RET_KEY_DIM, RET_VAL_DIM), F32)],
        compiler_params=_cparams(3),
        name="ret_scan",
    )(u_ret, u_ret, cosx, sinx, ret_decay)


RWKV_PAIRS = RWKV_HEADS // 2


def _bdiag(tile, bd2):
    return jnp.where(bd2, jnp.concatenate([tile, tile], axis=0), 0.0)


def _split2(x):
    hi = x.astype(BF16)
    return hi, (x - hi.astype(F32)).astype(BF16)


def _pair_mm(lhs_splits, w_split, bd2):
    hi = jnp.concatenate([s[0] for s in lhs_splits], axis=0)
    lo = jnp.concatenate([s[1] for s in lhs_splits], axis=0)
    n = hi.shape[0]
    o = jnp.dot(jnp.concatenate([hi, lo], axis=0), _bdiag(w_split[0], bd2), preferred_element_type=F32)
    tot = o[:n] + o[n:] + jnp.dot(hi, _bdiag(w_split[1], bd2), preferred_element_type=F32)
    return [tot[i * RWKV_HEAD_DIM:(i + 1) * RWKV_HEAD_DIM] for i in range(len(lhs_splits))]


def _pair_rows(x_t, j, low_half):
    base = j * LANES
    return jnp.where(low_half, x_t[base:base + RWKV_HEAD_DIM], x_t[base + RWKV_HEAD_DIM:base + LANES])


def _rwkv_prep(d, u, prow, nrow, mix, w0, w2, a0, a2, kkw, ka, rk, bd):
    lc = RWKV_CHUNK
    ni = RWKV_INNER
    t_idx = lax.broadcasted_iota(jnp.int32, (lc, 1), 0)
    prev = jnp.where(t_idx == 0, prow, pltpu.roll(u, 1, axis=0))
    nxt = jnp.where(t_idx == lc - 1, nrow, pltpu.roll(u, lc - 1, axis=0))
    u = u + mix * (0.5 * (prev + nxt) - u)
    r, k, v = u[:, 0:ni], u[:, ni:2 * ni], u[:, 2 * ni:3 * ni]
    o = 3 * ni
    w_lo = u[:, o:o + RWKV_DECAY_LORA]
    a_lo = u[:, o + RWKV_DECAY_LORA:o + RWKV_DECAY_LORA + RWKV_AAA_LORA]
    g_lo = u[:, o + RWKV_DECAY_LORA + RWKV_AAA_LORA:]

    w_log = -_softplus(-(w0 + jnp.dot(jnp.tanh(w_lo), w2, preferred_element_type=F32, precision=HIGHEST))) - 0.5
    logw = -jnp.exp(w_log)
    a_gate = jax.nn.sigmoid(a0 + jnp.dot(a_lo, a2, preferred_element_type=F32, precision=HIGHEST))
    kk = k * kkw
    kd = k * (1.0 + (a_gate - 1.0) * ka)
    s1, s2, _ = _split3(kk * kk)
    b1, b2, _ = _split3(r * kd * rk)
    sums = jnp.dot(jnp.concatenate([s1, s2, b1, b2], axis=0), bd, preferred_element_type=F32)
    kk = kk / jnp.maximum(jnp.sqrt(sums[0:lc] + sums[lc:2 * lc]), 1e-12)
    bvec = kk * a_gate
    bonus = (sums[2 * lc:3 * lc] + sums[3 * lc:4 * lc]) * v

    row = lax.broadcasted_iota(jnp.int32, (lc, lc), 0)
    col = lax.broadcasted_iota(jnp.int32, (lc, lc), 1)
    incl = (col <= row) if d == 0 else (col >= row)
    l1, l2, l3 = _split3(logw)
    cw3 = jnp.dot(incl.astype(BF16), jnp.concatenate([l1, l2, l3], axis=1), preferred_element_type=F32)
    cw = cw3[:, 0:ni] + cw3[:, ni:2 * ni] + cw3[:, 2 * ni:3 * ni]
    ctot = jnp.sum(logw, axis=0, keepdims=True)
    e_neg = jnp.exp(-cw)
    e_end = jnp.exp(ctot - cw)

    def dup_t(x):
        return jnp.concatenate([x, x], axis=0).T

    return dict(
        a_t=-kk * jnp.exp(cw - logw), r_t=r * jnp.exp(cw), v=v,
        bn_t=dup_t(bvec * e_neg), kn_t=dup_t(kd * e_neg), be_t=dup_t(bvec * e_end), ke_t=dup_t(kd * e_end),
        wtot_col=jnp.exp(jnp.sum(logw.T, axis=1, keepdims=True)),
        bonus=bonus, g_lo=g_lo)


def _rwkv_kernel(uf_ref, ufp_ref, ufn_ref, ub_ref, ubp_ref, ubn_ref, mix_ref, w0_ref, w2_ref, a0_ref, a2_ref,
                 g2_ref, kk_ref, ka_ref, rk_ref, bd_ref, of_ref, ob_ref, s_ref, *, ncx, nc):
    i = pl.program_id(1)
    lc = RWKV_CHUNK
    ni = RWKV_INNER
    hd = RWKV_HEAD_DIM
    chunk = (i, _chunk_of(1, i, ncx, nc))

    @pl.when(i == 0)
    def _():
        s_ref[...] = jnp.zeros_like(s_ref)

    u_refs = ((uf_ref, ufp_ref, ufn_ref), (ub_ref, ubp_ref, ubn_ref))
    out_refs = (of_ref, ob_ref)
    prep = []
    for d in (0, 1):
        seg_start, seg_end = _seg_edges(chunk[d], ncx, nc)
        u_ref, up_ref, un_ref = u_refs[d]
        prow = jnp.where(seg_start, 0.0, up_ref[0, SUBLANES - 1:SUBLANES, :])
        nrow = jnp.where(seg_end, 0.0, un_ref[0, 0:1, :])
        q = _rwkv_prep(d, u_ref[0], prow, nrow, mix_ref[...], w0_ref[d], w2_ref[d], a0_ref[d], a2_ref[...],
                       kk_ref[...], ka_ref[...], rk_ref[...], bd_ref[...])
        out_refs[d][0, :, ni:2 * ni] = q["bonus"]
        if d == 0:
            of_ref[0, :, 2 * ni:3 * ni] = jnp.dot(jax.nn.sigmoid(q["g_lo"]).astype(BF16), g2_ref[...],
                                                  preferred_element_type=F32)
        prep.append(q)

    row4 = lax.broadcasted_iota(jnp.int32, (lc, 2 * LANES), 0)
    col4 = lax.broadcasted_iota(jnp.int32, (lc, 2 * LANES), 1) % hd
    strict4 = (col4 < row4, col4 > row4)
    incl4 = (col4 <= row4, col4 >= row4)
    eye2 = (lax.broadcasted_iota(jnp.int32, (lc, LANES), 1) % hd
            == lax.broadcasted_iota(jnp.int32, (lc, LANES), 0)).astype(F32)
    low_half = lax.broadcasted_iota(jnp.int32, (hd, LANES), 1) < hd
    bd2 = ((lax.broadcasted_iota(jnp.int32, (LANES, LANES), 0) < hd)
           == (lax.broadcasted_iota(jnp.int32, (LANES, LANES), 1) < hd))
    zeros_w = jnp.zeros((LANES, LANES), F32)

    def dot(a, b):
        return jnp.dot(a.astype(BF16), b.astype(BF16), preferred_element_type=F32)

    items = [(d, j) for d in (0, 1) for j in range(RWKV_PAIRS)]
    pl_ = {j: slice(j * LANES, (j + 1) * LANES) for j in range(RWKV_PAIRS)}
    n_it = range(len(items))
    a_t = [prep[d]["a_t"][:, pl_[j]] for d, j in items]
    r_t = [prep[d]["r_t"][:, pl_[j]] for d, j in items]
    v_w = [_bdiag(prep[d]["v"][:, pl_[j]], bd2) for d, j in items]
    w_p = [jnp.concatenate([jnp.where(bd2, prep[d]["bn_t"][pl_[j], :], 0.0),
                            jnp.where(bd2, prep[d]["kn_t"][pl_[j], :], 0.0)], axis=1) for d, j in items]
    p = [dot(jnp.concatenate([a_t[n], r_t[n]], axis=0), w_p[n]) for n in n_it]
    m_a = [jnp.where(strict4[items[n][0]], p[n][:lc], 0.0) for n in n_it]
    m_r = [jnp.where(incl4[items[n][0]], p[n][lc:], 0.0) for n in n_it]
    mv = [dot(m_a[n][:, LANES:], v_w[n]) for n in n_it]
    x = [m_a[n][:, :LANES] for n in n_it]
    t = [eye2 + x[n] for n in n_it]
    xs = [_split2(x[n]) for n in n_it]
    x = [_pair_mm([xs[n]], xs[n], bd2)[0] for n in n_it]
    for _ in range(int(math.log2(lc)) - 2):
        xs = [_split2(x[n]) for n in n_it]
        tx = [_pair_mm([_split2(t[n]), xs[n]], xs[n], bd2) for n in n_it]
        t = [t[n] + tx[n][0] for n in n_it]
        x = [tx[n][1] for n in n_it]
    t = [t[n] + _pair_mm([_split2(t[n])], _split2(x[n]), bd2)[0] for n in n_it]
    au = [dot(t[n], jnp.concatenate([_bdiag(a_t[n], bd2), _bdiag(mv[n], bd2)], axis=1)) for n in n_it]
    w2 = [jnp.concatenate([jnp.concatenate([_bdiag(au[n][:, :LANES], bd2), _bdiag(au[n][:, LANES:], bd2)], axis=1),
                           jnp.concatenate([zeros_w, v_w[n]], axis=1)], axis=0) for n in n_it]
    q_bk = [jnp.concatenate([_pair_rows(prep[d]["be_t"], j, low_half), _pair_rows(prep[d]["ke_t"], j, low_half)],
                            axis=1) for d, j in items]
    big = [dot(jnp.concatenate([q_bk[n], m_r[n]], axis=0), w2[n]) for n in n_it]
    st = [s_ref[d, j] for d, j in items]
    sy = [dot(jnp.concatenate([r_t[n] + big[n][lc:, :LANES], big[n][:lc, :LANES]], axis=0), _bdiag(st[n], bd2))
          for n in n_it]
    for n, (d, j) in enumerate(items):
        out_refs[d][0, :, pl_[j]] = sy[n][:lc] + big[n][lc:, LANES:]
    for n, (d, j) in enumerate(items):
        wc = jnp.where(low_half,
                       jnp.broadcast_to(prep[d]["wtot_col"][j * LANES:j * LANES + hd, :], (hd, LANES)),
                       jnp.broadcast_to(prep[d]["wtot_col"][j * LANES + hd:(j + 1) * LANES, :], (hd, LANES)))
        s_ref[d, j] = wc * st[n] + sy[n][lc:] + big[n][:lc, LANES:]


def rwkv_scan(u_rwkv, mix, w0, w2, a0, a2, g2, k_k, k_a, r_k, bd, ncx_tokens):
    b, ta, _ = u_rwkv.shape
    lc = RWKV_CHUNK
    nc = ta // lc
    ncx = ncx_tokens // lc
    nb = lc // SUBLANES
    nblk = ta // SUBLANES
    ni = RWKV_INNER
    kern = functools.partial(_rwkv_kernel, ncx=ncx, nc=nc)
    const2 = lambda bi, i: (0, 0)
    const3 = lambda bi, i: (0, 0, 0)
    fwd = lambda i: i
    bwd = lambda i: _chunk_of(1, i, ncx, nc)

    def u_specs(ch):
        return [pl.BlockSpec((1, lc, RWKV_IN), lambda bi, i: (bi, ch(i), 0)),
                pl.BlockSpec((1, SUBLANES, RWKV_IN), lambda bi, i: (bi, jnp.maximum(ch(i) * nb - 1, 0), 0)),
                pl.BlockSpec((1, SUBLANES, RWKV_IN), lambda bi, i: (bi, jnp.minimum((ch(i) + 1) * nb, nblk - 1), 0))]

    return pl.pallas_call(
        kern,
        out_shape=(jax.ShapeDtypeStruct((b, ta, 3 * ni), F32), jax.ShapeDtypeStruct((b, ta, 2 * ni), F32)),
        grid=(b, nc),
        in_specs=u_specs(fwd) + u_specs(bwd) + [
            pl.BlockSpec((1, RWKV_IN), const2),
            pl.BlockSpec((2, 1, ni), const3),
            pl.BlockSpec((2, RWKV_DECAY_LORA, ni), const3),
            pl.BlockSpec((2, 1, ni), const3),
            pl.BlockSpec((RWKV_AAA_LORA, ni), const2),
            pl.BlockSpec((RWKV_GATE_LORA, ni), const2),
            pl.BlockSpec((1, ni), const2),
            pl.BlockSpec((1, ni), const2),
            pl.BlockSpec((1, ni), const2),
            pl.BlockSpec((ni, ni), const2),
        ],
        out_specs=(pl.BlockSpec((1, lc, 3 * ni), lambda bi, i: (bi, i, 0)),
                   pl.BlockSpec((1, lc, 2 * ni), lambda bi, i: (bi, bwd(i), 0))),
        scratch_shapes=[pltpu.VMEM((2, RWKV_PAIRS, RWKV_HEAD_DIM, LANES), F32)],
        compiler_params=_cparams(2),
        name="rwkv_scan",
    )(u_rwkv, u_rwkv, u_rwkv, u_rwkv, u_rwkv, u_rwkv, mix, w0, w2, a0, a2, g2, k_k, k_a, r_k, bd)


def _merge_kernel(ssd_ref, z_ref, ret_ref, rg_ref, rwf_ref, rwb_ref, gate_ref, x_ref, mod_ref, nw_ref, ssdnw_ref,
                  lnw_ref, lnb_ref, bd_ref, wso_ref, wro_ref, wwo_ref, wo_ref, o_ref):
    ys = (ssd_ref[0, 0] + ssd_ref[1, 0]) * _silu(z_ref[0][:, :SSD_INNER])
    parts = []
    for g in range(SSD_GROUPS):
        yg = ys[:, g * SSD_GROUP_W:(g + 1) * SSD_GROUP_W]
        parts.append(yg * lax.rsqrt(jnp.mean(yg * yg, axis=-1, keepdims=True) + EPS))
    ys = jnp.concatenate(parts, axis=1) * ssdnw_ref[...]
    o_ssd = jnp.dot(ys.astype(BF16), wso_ref[...], preferred_element_type=F32)

    yr = ret_ref[0, 0] + ret_ref[1, 0]
    parts = []
    for h in range(RET_HEADS):
        yh = yr[:, h * RET_VAL_DIM:(h + 1) * RET_VAL_DIM]
        yc = yh - jnp.mean(yh, axis=-1, keepdims=True)
        parts.append(yc * lax.rsqrt(jnp.mean(yc * yc, axis=-1, keepdims=True) + EPS))
    yr = jnp.concatenate(parts, axis=1) * _silu(rg_ref[0])
    o_ret = jnp.dot(yr.astype(BF16), wro_ref[...], preferred_element_type=F32)

    ni = RWKV_INNER
    rw0 = rwf_ref[0]
    rw1 = rwb_ref[0]
    yw = rw0[:, :ni] + rw1[:, :ni]
    bd = bd_ref[...]
    inv_hd = 1.0 / RWKV_HEAD_DIM
    yc = yw - _dot_sel_rhs(yw, bd) * inv_hd
    var = _dot_sel_rhs(yc * yc, bd) * inv_hd
    yw = yc * lax.rsqrt(var + RWKV_LN_EPS) * lnw_ref[...] + lnb_ref[...]
    yw = (yw + rw0[:, ni:2 * ni] + rw1[:, ni:2 * ni]) * rw0[:, 2 * ni:3 * ni]
    o_rw = jnp.dot(yw.astype(BF16), wwo_ref[...], preferred_element_type=F32)

    gate = gate_ref[0]
    merged = (jax.nn.sigmoid(gate[:, :D_MODEL]) * o_ssd
              + jax.nn.sigmoid(gate[:, D_MODEL:2 * D_MODEL]) * o_ret
              + jax.nn.sigmoid(gate[:, 2 * D_MODEL:]) * o_rw)
    yx = jnp.dot(merged.astype(BF16), wo_ref[...], preferred_element_type=F32)
    g1 = mod_ref[0, 0, 2:3, :]
    o_ref[0] = x_ref[0] + g1 * _rms(yx, nw_ref[...])


def merge_out(ssd_y, u_zdt, ret_y, u_ret, rw_f, rw_b, u_gate, x, mod, nw1, ssd_nw, ln_w, ln_b, bd,
              w_ssd_out, w_ret_out, w_rwkv_out, w_out, n_ctx_tiles):
    b, ta, dm = x.shape
    tm = ROW_TILE
    row = lambda bi, i: (bi, i, 0)
    both = lambda bi, i: (0, bi, i, 0)
    const2 = lambda bi, i: (0, 0)
    return pl.pallas_call(
        _merge_kernel,
        out_shape=jax.ShapeDtypeStruct((b, ta, dm), F32),
        grid=(b, ta // tm),
        in_specs=[
            pl.BlockSpec((2, 1, tm, SSD_INNER), both),
            pl.BlockSpec((1, tm, u_zdt.shape[2]), row),
            pl.BlockSpec((2, 1, tm, RET_INNER), both),
            pl.BlockSpec((1, tm, RET_INNER), lambda bi, i: (bi, i, 2)),
            pl.BlockSpec((1, tm, 3 * RWKV_INNER), row),
            pl.BlockSpec((1, tm, 2 * RWKV_INNER), row),
            pl.BlockSpec((1, tm, N_BRANCH * dm), row),
            pl.BlockSpec((1, tm, dm), row),
            pl.BlockSpec((1, 1, SUBLANES, dm), lambda bi, i: (bi, (i >= n_ctx_tiles).astype(jnp.int32), 0, 0)),
            pl.BlockSpec((1, dm), const2),
            pl.BlockSpec((1, SSD_INNER), const2),
            pl.BlockSpec((1, RWKV_INNER), const2),
            pl.BlockSpec((1, RWKV_INNER), const2),
            pl.BlockSpec((RWKV_INNER, RWKV_INNER), const2),
            pl.BlockSpec((SSD_INNER, dm), const2),
            pl.BlockSpec((RET_INNER, dm), const2),
            pl.BlockSpec((RWKV_INNER, dm), const2),
            pl.BlockSpec((dm, dm), const2),
        ],
        out_specs=pl.BlockSpec((1, tm, dm), row),
        compiler_params=_cparams(2),
        name="merge_out",
    )(ssd_y, u_zdt, ret_y, u_ret, rw_f, rw_b, u_gate, x, mod, nw1, ssd_nw, ln_w, ln_b, bd,
      w_ssd_out, w_ret_out, w_rwkv_out, w_out)


def _mlp_kernel(x_ref, mod_ref, nw2_ref, nw3_ref, w1_ref, w2_ref, o_ref, h_ref, acc_ref):
    j = pl.program_id(2)

    @pl.when(j == 0)
    def _():
        y = _rms(x_ref[0], nw2_ref[...])
        h_ref[...] = (y * (1.0 + mod_ref[0, 0, 4:5, :]) + mod_ref[0, 0, 3:4, :]).astype(BF16)
        acc_ref[...] = jnp.zeros_like(acc_ref)

    hid = jnp.dot(h_ref[...], w1_ref[...], preferred_element_type=F32)
    hid = jnp.square(jnp.maximum(hid, 0.0))
    acc_ref[...] += jnp.dot(hid.astype(BF16), w2_ref[...], preferred_element_type=F32)

    @pl.when(j == pl.num_programs(2) - 1)
    def _():
        g2 = mod_ref[0, 0, 5:6, :]
        o_ref[0] = x_ref[0] + g2 * _rms(acc_ref[...], nw3_ref[...])


def mlp_block(x, mod, nw2, nw3, w1, w2, n_ctx_tiles, ff_tile=D_FF):
    b, ta, dm = x.shape
    tm = ROW_TILE
    row = lambda bi, i, j: (bi, i, 0)
    const2 = lambda bi, i, j: (0, 0)
    return pl.pallas_call(
        _mlp_kernel,
        out_shape=jax.ShapeDtypeStruct((b, ta, dm), F32),
        grid=(b, ta // tm, D_FF // ff_tile),
        in_specs=[
            pl.BlockSpec((1, tm, dm), row),
            pl.BlockSpec((1, 1, SUBLANES, dm), lambda bi, i, j: (bi, (i >= n_ctx_tiles).astype(jnp.int32), 0, 0)),
            pl.BlockSpec((1, dm), const2),
            pl.BlockSpec((1, dm), const2),
            pl.BlockSpec((dm, ff_tile), lambda bi, i, j: (0, j)),
            pl.BlockSpec((ff_tile, dm), lambda bi, i, j: (j, 0)),
        ],
        out_specs=pl.BlockSpec((1, tm, dm), row),
        scratch_shapes=[pltpu.VMEM((tm, dm), BF16), pltpu.VMEM((tm, dm), F32)],
        compiler_params=_cparams(3),
        name="mlp",
    )(x, mod, nw2, nw3, w1, w2)


def _rope_tables(n_ctx, n_lat):
    rows = n_lat // GRID_W
    row = np.repeat(np.arange(rows), GRID_W).astype(np.float32)
    col = np.tile(np.arange(GRID_W), rows).astype(np.float32)
    n_freq = RET_KEY_DIM // 4
    inv = jnp.power(ROPE_BASE, -jnp.arange(n_freq, dtype=F32) / n_freq)
    ang = jnp.concatenate([jnp.asarray(row)[:, None] * inv, jnp.asarray(col)[:, None] * inv], axis=-1)
    cos = jnp.concatenate([jnp.ones((n_ctx, RET_KEY_DIM // 2), F32), jnp.cos(ang)], axis=0)
    sin = jnp.concatenate([jnp.zeros((n_ctx, RET_KEY_DIM // 2), F32), jnp.sin(ang)], axis=0)
    cos_h = jnp.concatenate([cos, cos], axis=1)
    sin_h = jnp.concatenate([-sin, sin], axis=1)
    reps = 2 * RET_HEADS
    return jnp.tile(cos_h, (1, reps)), jnp.tile(sin_h, (1, reps))


def _pad_lanes(a, width=LANES):
    return jnp.pad(a, [(0, 0)] * (a.ndim - 1) + [(0, width - a.shape[-1])])


def kernel(x, c, ctx, c_ctx, norm_w, ada_w, ada_b, w_in, ssd_conv_w, ssd_conv_b, ssd_dt_bias, ssd_a_log,
           ssd_d, ssd_norm_w, ret_decay, rwkv_mix, rwkv_w0, rwkv_w2, rwkv_a0, rwkv_a2, rwkv_g2, rwkv_k_k,
           rwkv_k_a, rwkv_r_k, rwkv_lnx_w, rwkv_lnx_b, w_ssd_out, w_ret_out, w_rwkv_out, w_out, mlp_w1, mlp_w2):
    b, n_lat, dm = x.shape
    n_ctx = ctx.shape[1]
    ta = n_ctx + n_lat
    depth = norm_w.shape[0]
    n_ctx_tiles = n_ctx // ROW_TILE
    assert n_ctx % ROW_TILE == 0 and n_lat % ROW_TILE == 0 and b + 1 <= SUBLANES

    hexp = np.zeros((LANES, SSD_INNER), np.float32)
    for h in range(SSD_HEADS):
        hexp[h, h * SSD_HEAD_DIM:(h + 1) * SSD_HEAD_DIM] = 1.0
    hexp = jnp.asarray(hexp, BF16)
    bd = jnp.asarray(np.kron(np.eye(RWKV_HEADS, dtype=np.float32),
                             np.ones((RWKV_HEAD_DIM, RWKV_HEAD_DIM), np.float32)), BF16)
    cosx, sinx = _rope_tables(n_ctx, n_lat)

    cond = jnp.zeros((SUBLANES, dm), F32).at[:b].set(c).at[b].set(c_ctx)
    mod_all = modulation_all(cond, ada_w, ada_b)

    sizes = (N_BRANCH * dm, SSD_INNER, SSD_CONV_DIM, 2 * SSD_HEADS, RET_QK, RET_QK, RET_INNER, RET_INNER, RWKV_IN)
    offs = np.concatenate([[0], np.cumsum(sizes)])
    perm = np.concatenate([np.concatenate([np.arange(0, RET_KEY_DIM, 2), np.arange(1, RET_KEY_DIM, 2)]) + h * RET_KEY_DIM
                           for h in range(RET_HEADS)])

    xall = jnp.concatenate([ctx, x], axis=1)
    tm_mm = 1024 if (b * ta) % 1024 == 0 else ROW_TILE
    for l in range(depth):
        wl = w_in[l]
        seg = [wl[:, offs[j]:offs[j + 1]] for j in range(len(sizes))]
        w_gate = seg[0].astype(BF16)
        w_zdt = jnp.concatenate([seg[1], _pad_lanes(seg[3][:, :SSD_HEADS]), _pad_lanes(seg[3][:, SSD_HEADS:])],
                                axis=1).astype(BF16)
        w_xbc = seg[2].astype(BF16)
        w_ret = jnp.concatenate([seg[4][:, perm], seg[5][:, perm], seg[6], seg[7]], axis=1).astype(BF16)
        w_rw = seg[8].astype(BF16)

        m = mod_all[l].reshape(SUBLANES, 6, dm)
        m = jnp.pad(m, ((0, 0), (0, SUBLANES - 6), (0, 0)))
        mod = jnp.stack([jnp.broadcast_to(m[b], (b,) + m.shape[1:]), m[:b]], axis=1)
        nw = norm_w[l]

        h = norm_modulate(xall, nw[0:1], mod, n_ctx_tiles).reshape(b * ta, dm)
        u_gate = matmul(h, w_gate, tm_mm, 1024).reshape(b, ta, -1)
        u_zdt = matmul(h, w_zdt, tm_mm, w_zdt.shape[1]).reshape(b, ta, -1)
        u_xbc = matmul(h, w_xbc, tm_mm, SSD_CONV_DIM).reshape(b, ta, -1)
        u_ret = matmul(h, w_ret, tm_mm, w_ret.shape[1]).reshape(b, ta, -1)
        u_rw = matmul(h, w_rw, tm_mm, RWKV_IN).reshape(b, ta, -1)

        conv_w = jnp.pad(ssd_conv_w[l], ((0, SUBLANES - SSD_CONV), (0, 0)))
        dskip = jnp.repeat(ssd_d[l], SSD_HEAD_DIM)[None, :]
        ssd_y = ssd_scan(u_xbc, u_zdt, conv_w, ssd_conv_b[l][None, :],
                         _pad_lanes(ssd_dt_bias[l])[:, None, :], _pad_lanes(ssd_a_log[l])[:, None, :],
                         dskip, hexp, n_ctx)
        ret_y = ret_scan(u_ret, cosx, sinx, _pad_lanes(ret_decay[l])[:, None, :], n_ctx)
        rw_f, rw_b = rwkv_scan(u_rw, rwkv_mix[l][None, :], rwkv_w0[l][:, None, :], rwkv_w2[l], rwkv_a0[l][:, None, :],
                         rwkv_a2[l], rwkv_g2[l].astype(BF16), rwkv_k_k[l][None, :], rwkv_k_a[l][None, :],
                         rwkv_r_k[l].reshape(1, RWKV_INNER), bd, n_ctx)

        xall = merge_out(ssd_y, u_zdt, ret_y, u_ret, rw_f, rw_b, u_gate, xall, mod, nw[1:2], ssd_norm_w[l][None, :],
                         rwkv_lnx_w[l][None, :], rwkv_lnx_b[l][None, :], bd,
                         w_ssd_out[l].astype(BF16), w_ret_out[l].astype(BF16), w_rwkv_out[l].astype(BF16),
                         w_out[l].astype(BF16), n_ctx_tiles)
        xall = mlp_block(xall, mod, nw[2:3], nw[3:4], mlp_w1[l].astype(BF16), mlp_w2[l].astype(BF16), n_ctx_tiles)
    return xall[:, n_ctx:, :]
```

```python
import functools
import math

import numpy as np
import jax
import jax.numpy as jnp
from jax import lax
from jax.experimental import pallas as pl
from jax.experimental.pallas import tpu as pltpu

F32 = jnp.float32
BF16 = jnp.bfloat16
HIGHEST = lax.Precision.HIGHEST

D_MODEL = 1024
DEPTH = 4
GRID_W = 64
EPS = 1e-6
N_BRANCH = 3

SSD_HEADS = 16
SSD_HEAD_DIM = 64
SSD_INNER = SSD_HEADS * SSD_HEAD_DIM
SSD_GROUPS = 2
SSD_STATE = 128
SSD_CONV = 5
SSD_CHUNK = 128
SSD_CONV_DIM = SSD_INNER + 2 * SSD_GROUPS * SSD_STATE
SSD_GROUP_W = SSD_INNER // SSD_GROUPS

RET_HEADS = 4
RET_KEY_DIM = 64
RET_VAL_DIM = 128
RET_QK = RET_HEADS * RET_KEY_DIM
RET_INNER = RET_HEADS * RET_VAL_DIM
RET_CHUNK = 128
ROPE_BASE = 10000.0

RWKV_HEADS = 8
RWKV_HEAD_DIM = 64
RWKV_INNER = RWKV_HEADS * RWKV_HEAD_DIM
RWKV_DECAY_LORA = 64
RWKV_AAA_LORA = 64
RWKV_GATE_LORA = 128
RWKV_IN = 3 * RWKV_INNER + RWKV_DECAY_LORA + RWKV_AAA_LORA + RWKV_GATE_LORA
RWKV_LN_EPS = 64e-5
RWKV_CHUNK = 64

D_FF = 4 * D_MODEL

SUBLANES = 8
LANES = 128
VMEM_LIMIT_BYTES = 56 * 1024 * 1024

ROW_TILE = 256
NEG_BIG = -1e30


def _cparams(n_axes):
    return pltpu.CompilerParams(dimension_semantics=("arbitrary",) * n_axes,
                                vmem_limit_bytes=VMEM_LIMIT_BYTES)


def _silu(x):
    return x * jax.nn.sigmoid(x)


def _softplus(x):
    return jnp.maximum(x, 0.0) + jnp.log1p(jnp.exp(-jnp.abs(x)))


def _split3(x):
    x1 = x.astype(BF16)
    r1 = x - x1.astype(F32)
    x2 = r1.astype(BF16)
    r2 = r1 - x2.astype(F32)
    return x1, x2, r2.astype(BF16)


def _dot_sel_rhs(x, sel):
    x1, x2, x3 = _split3(x)
    return (jnp.dot(x1, sel, preferred_element_type=F32)
            + jnp.dot(x2, sel, preferred_element_type=F32)
            + jnp.dot(x3, sel, preferred_element_type=F32))


def _dot_sel_lhs(sel, x):
    x1, x2, x3 = _split3(x)
    return (jnp.dot(sel, x1, preferred_element_type=F32)
            + jnp.dot(sel, x2, preferred_element_type=F32)
            + jnp.dot(sel, x3, preferred_element_type=F32))


def _dot_nt(a, b, precision=None):
    return lax.dot_general(a, b, (((1,), (1,)), ((), ())), preferred_element_type=F32, precision=precision)


def _dot_tn(a, b, precision=None):
    return lax.dot_general(a, b, (((0,), (0,)), ((), ())), preferred_element_type=F32, precision=precision)


def _chunk_of(d, i, ncx, nc):
    bwd = jnp.where(i < ncx, ncx - 1 - i, ncx + nc - 1 - i)
    return jnp.where(d == 0, i, bwd)


def _seg_edges(c, ncx, nc):
    seg_start = jnp.logical_or(c == 0, c == ncx)
    seg_end = jnp.logical_or(c == ncx - 1, c == nc - 1)
    return seg_start, seg_end


def _tri_mask(n, d, strict, reps=1):
    row = lax.broadcasted_iota(jnp.int32, (n, n * reps), 0)
    col = lax.broadcasted_iota(jnp.int32, (n, n * reps), 1) % n
    lead = jnp.where(d == 0, row - col, col - row)
    return lead > 0 if strict else lead >= 0


def _mod_kernel(c_ref, w_ref, b_ref, o_ref):
    s = _silu(c_ref[...])
    o_ref[0] = jnp.dot(s, w_ref[0], preferred_element_type=F32, precision=HIGHEST) + b_ref[0]


def modulation_all(cond, ada_w, ada_b):
    depth = ada_w.shape[0]
    tn = 1536
    return pl.pallas_call(
        _mod_kernel,
        out_shape=jax.ShapeDtypeStruct((depth, SUBLANES, 6 * D_MODEL), F32),
        grid=(depth, 6 * D_MODEL // tn),
        in_specs=[pl.BlockSpec((SUBLANES, D_MODEL), lambda l, j: (0, 0)),
                  pl.BlockSpec((1, D_MODEL, tn), lambda l, j: (l, 0, j)),
                  pl.BlockSpec((1, 1, tn), lambda l, j: (l, 0, j))],
        out_specs=pl.BlockSpec((1, SUBLANES, tn), lambda l, j: (l, 0, j)),
        compiler_params=_cparams(2),
        name="modulation",
    )(cond, ada_w, ada_b.reshape(depth, 1, 6 * D_MODEL))


def _rms(x, w):
    return x * lax.rsqrt(jnp.mean(x * x, axis=-1, keepdims=True) + EPS) * w


def _normmod_kernel(x_ref, nw_ref, mod_ref, h_ref):
    y = _rms(x_ref[0], nw_ref[...])
    shift = mod_ref[0, 0, 0:1, :]
    scale = mod_ref[0, 0, 1:2, :]
    h_ref[0] = (y * (1.0 + scale) + shift).astype(BF16)


def norm_modulate(x, nw, mod, n_ctx_tiles):
    b, ta, dm = x.shape
    return pl.pallas_call(
        _normmod_kernel,
        out_shape=jax.ShapeDtypeStruct((b, ta, dm), BF16),
        grid=(b, ta // ROW_TILE),
        in_specs=[pl.BlockSpec((1, ROW_TILE, dm), lambda bi, i: (bi, i, 0)),
                  pl.BlockSpec((1, dm), lambda bi, i: (0, 0)),
                  pl.BlockSpec((1, 1, SUBLANES, dm), lambda bi, i: (bi, (i >= n_ctx_tiles).astype(jnp.int32), 0, 0))],
        out_specs=pl.BlockSpec((1, ROW_TILE, dm), lambda bi, i: (bi, i, 0)),
        compiler_params=_cparams(2),
        name="norm_modulate",
    )(x, nw, mod)


def _mm_kernel(a_ref, w_ref, o_ref):
    o_ref[...] = jnp.dot(a_ref[...], w_ref[...], preferred_element_type=F32).astype(o_ref.dtype)


def matmul(a, w, tm, tn, out_dtype=F32):
    r, k = a.shape
    n = w.shape[1]
    assert r % tm == 0 and n % tn == 0
    return pl.pallas_call(
        _mm_kernel,
        out_shape=jax.ShapeDtypeStruct((r, n), out_dtype),
        grid=(n // tn, r // tm),
        in_specs=[pl.BlockSpec((tm, k), lambda j, i: (i, 0)),
                  pl.BlockSpec((k, tn), lambda j, i: (0, j))],
        out_specs=pl.BlockSpec((tm, tn), lambda j, i: (i, j)),
        compiler_params=_cparams(2),
        name="in_proj",
    )(a, w)


def _ssd_kernel(xbc_ref, xp_ref, xn_ref, dt_ref, cw_ref, cb_ref, dtb_ref, alog_ref, dskip_ref,
                hexp_ref, y_ref, s_ref, ext_ref, *, ncx, nc):
    d = pl.program_id(1)
    i = pl.program_id(2)
    c = _chunk_of(d, i, ncx, nc)
    seg_start, seg_end = _seg_edges(c, ncx, nc)
    lc = SSD_CHUNK
    hb = SUBLANES

    @pl.when(i == 0)
    def _():
        s_ref[...] = jnp.zeros_like(s_ref)

    ext_ref[0:hb, :] = jnp.where(seg_start, 0.0, xp_ref[0])
    ext_ref[hb:hb + lc, :] = xbc_ref[0]
    ext_ref[hb + lc:hb + lc + hb, :] = jnp.where(seg_end, 0.0, xn_ref[0])
    pad = SSD_CONV // 2
    acc = ext_ref[hb - pad:hb - pad + lc, :] * cw_ref[0:1, :]
    for j in range(1, SSD_CONV):
        acc = acc + ext_ref[hb - pad + j:hb - pad + j + lc, :] * cw_ref[j:j + 1, :]
    xbc = _silu(acc + cb_ref[...])
    xs = xbc[:, :SSD_INNER]
    bm = xbc[:, SSD_INNER:SSD_INNER + SSD_GROUPS * SSD_STATE]
    cm = xbc[:, SSD_INNER + SSD_GROUPS * SSD_STATE:]

    dt = _softplus(dt_ref[0] + dtb_ref[0])
    la = dt * (-jnp.exp(alog_ref[0]))
    tri = _tri_mask(lc, d, strict=False)
    ac3 = jnp.dot(tri.astype(BF16), jnp.concatenate(_split3(la), axis=1), preferred_element_type=F32)
    acum = ac3[:, :LANES] + ac3[:, LANES:2 * LANES] + ac3[:, 2 * LANES:]
    atot = jnp.sum(la, axis=0, keepdims=True)
    acum_t = acum.T

    etot8 = jnp.broadcast_to(jnp.exp(atot), (SUBLANES, LANES))
    parts = jnp.concatenate(_split2(dt) + _split2(jnp.exp(acum)) + _split2(jnp.exp(atot - acum))
                            + _split2(etot8), axis=0)
    ex = jnp.dot(parts, hexp_ref[...], preferred_element_type=F32)
    dtx = ex[0:lc] + ex[lc:2 * lc]
    eacx = ex[2 * lc:3 * lc] + ex[3 * lc:4 * lc]
    eendx = ex[4 * lc:5 * lc] + ex[5 * lc:6 * lc]
    etotx = (ex[6 * lc:6 * lc + SUBLANES] + ex[6 * lc + SUBLANES:])[0:1, :]

    v = xs * dtx
    vb = v.astype(BF16)
    vend = (v * eendx).astype(BF16)
    skip = jnp.where(d == 0, 1.0, 0.0) * dskip_ref[...]
    hg = SSD_HEADS // SSD_GROUPS
    for g in range(SSD_GROUPS):
        gs = slice(g * SSD_STATE, (g + 1) * SSD_STATE)
        gw = slice(g * SSD_GROUP_W, (g + 1) * SSD_GROUP_W)
        cg = cm[:, gs].astype(BF16)
        bg = bm[:, gs].astype(BF16)
        scores = _dot_nt(cg, bg)
        s_prev = s_ref[g]
        y_off = jnp.dot(cg, s_prev.astype(BF16), preferred_element_type=F32) * eacx[:, gw]
        y_ref[0, 0, :, gw] = y_off + xs[:, gw] * skip[:, gw]
        for hh in range(hg):
            h = g * hg + hh
            diff = acum[:, h:h + 1] - acum_t[h:h + 1, :]
            dm = jnp.exp(jnp.where(tri, diff, NEG_BIG))
            hs = slice(h * SSD_HEAD_DIM, (h + 1) * SSD_HEAD_DIM)
            y_ref[0, 0, :, hs] += jnp.dot((scores * dm).astype(BF16), vb[:, hs], preferred_element_type=F32)
        s_ref[g] = s_prev * etotx[:, gw] + _dot_tn(bg, vend[:, gw])


def ssd_scan(u_xbc, u_zdt, conv_w, conv_b, dt_bias, a_log, dskip, hexp, ncx_tokens):
    b, ta, _ = u_xbc.shape
    lc = SSD_CHUNK
    nc = ta // lc
    ncx = ncx_tokens // lc
    nb = lc // SUBLANES
    nblk = ta // SUBLANES
    ch = functools.partial(_chunk_of, ncx=ncx, nc=nc)
    kern = functools.partial(_ssd_kernel, ncx=ncx, nc=nc)
    zdt_blk0 = SSD_INNER // LANES
    return pl.pallas_call(
        kern,
        out_shape=jax.ShapeDtypeStruct((2, b, ta, SSD_INNER), F32),
        grid=(b, 2, nc),
        in_specs=[
            pl.BlockSpec((1, lc, SSD_CONV_DIM), lambda bi, d, i: (bi, ch(d, i), 0)),
            pl.BlockSpec((1, SUBLANES, SSD_CONV_DIM), lambda bi, d, i: (bi, jnp.maximum(ch(d, i) * nb - 1, 0), 0)),
            pl.BlockSpec((1, SUBLANES, SSD_CONV_DIM), lambda bi, d, i: (bi, jnp.minimum((ch(d, i) + 1) * nb, nblk - 1), 0)),
            pl.BlockSpec((1, lc, LANES), lambda bi, d, i: (bi, ch(d, i), zdt_blk0 + d)),
            pl.BlockSpec((SUBLANES, SSD_CONV_DIM), lambda bi, d, i: (0, 0)),
            pl.BlockSpec((1, SSD_CONV_DIM), lambda bi, d, i: (0, 0)),
            pl.BlockSpec((1, 1, LANES), lambda bi, d, i: (d, 0, 0)),
            pl.BlockSpec((1, 1, LANES), lambda bi, d, i: (d, 0, 0)),
            pl.BlockSpec((1, SSD_INNER), lambda bi, d, i: (0, 0)),
            pl.BlockSpec((LANES, SSD_INNER), lambda bi, d, i: (0, 0)),
        ],
        out_specs=pl.BlockSpec((1, 1, lc, SSD_INNER), lambda bi, d, i: (d, bi, ch(d, i), 0)),
        scratch_shapes=[pltpu.VMEM((SSD_GROUPS, SSD_STATE, SSD_GROUP_W), F32),
                        pltpu.VMEM((lc + 2 * SUBLANES, SSD_CONV_DIM), F32)],
        compiler_params=_cparams(3),
        name="ssd_scan",
    )(u_xbc, u_xbc, u_xbc, u_zdt, conv_w, conv_b, dt_bias, a_log, dskip, hexp)


def _ret_kernel(qkf_ref, vf_ref, cosf_ref, sinf_ref, qkb_ref, vb_ref, cosb_ref, sinb_ref, dec_ref,
                yf_ref, yb_ref, s_ref):
    i = pl.program_id(1)
    lc = RET_CHUNK

    @pl.when(i == 0)
    def _():
        s_ref[...] = jnp.zeros_like(s_ref)

    in_refs = ((qkf_ref, vf_ref, cosf_ref, sinf_ref), (qkb_ref, vb_ref, cosb_ref, sinb_ref))
    out_refs = (yf_ref, yb_ref)
    half = RET_KEY_DIM // 2
    width = 2 * RET_QK
    lane = lax.broadcasted_iota(jnp.int32, (lc, width), 1)
    row = lax.broadcasted_iota(jnp.int32, (lc, lc), 0)
    col = lax.broadcasted_iota(jnp.int32, (lc, lc), 1)
    dist = jnp.abs(row - col).astype(F32)
    tri = (col <= row, col >= row)
    t_idx = lax.broadcasted_iota(jnp.int32, (lc, 1), 0).astype(F32)
    pos = (t_idx, lc - 1.0 - t_idx)
    q, k, v, lg = [], [], [], []
    for d in (0, 1):
        qk_ref, v_ref, cos_ref, sin_ref = in_refs[d]
        qk = qk_ref[0]
        swapped = jnp.where((lane % RET_KEY_DIM) < half,
                            pltpu.roll(qk, width - half, axis=1), pltpu.roll(qk, half, axis=1))
        qk = qk * cos_ref[...] + swapped * sin_ref[...]
        q.append(qk[:, :RET_QK].astype(BF16))
        k.append((qk[:, RET_QK:] * (RET_KEY_DIM ** -0.5)).astype(BF16))
        v.append(v_ref[0])
        lg.append(-_softplus(-dec_ref[d]))

    items = [(d, h) for d in (0, 1) for h in range(RET_HEADS)]
    n_it = range(len(items))
    ks = [slice(h * RET_KEY_DIM, (h + 1) * RET_KEY_DIM) for d, h in items]
    vs = [slice(h * RET_VAL_DIM, (h + 1) * RET_VAL_DIM) for d, h in items]
    lgh = [lg[d][:, h:h + 1] for d, h in items]
    qh = [q[d][:, ks[n]] for n, (d, h) in enumerate(items)]
    kh = [k[d][:, ks[n]] for n, (d, h) in enumerate(items)]
    vh = [v[d][:, vs[n]] for n, (d, h) in enumerate(items)]
    scores = [(_dot_nt(qh[n], kh[n]) * jnp.exp(jnp.where(tri[items[n][0]], dist * lgh[n], NEG_BIG))).astype(BF16)
              for n in n_it]
    st = [s_ref[d, h] for d, h in items]
    y_in = [jnp.dot(scores[n], vh[n].astype(BF16), preferred_element_type=F32) for n in n_it]
    y_st = [jnp.dot(qh[n], st[n].astype(BF16), preferred_element_type=F32) for n in n_it]
    for n, (d, h) in enumerate(items):
        out_refs[d][0, :, vs[n]] = y_in[n] + y_st[n] * jnp.exp((pos[d] + 1.0) * lgh[n])
    vend = [(vh[n] * jnp.exp((lc - 1.0 - pos[items[n][0]]) * lgh[n])).astype(BF16) for n in n_it]
    cs = [_dot_tn(kh[n], vend[n]) for n in n_it]
    for n, (d, h) in enumerate(items):
        s_ref[d, h] = st[n] * jnp.exp(lc * lgh[n]) + cs[n]


def ret_scan(u_ret, cosx, sinx, ret_decay, ncx_tokens):
    b, ta, _ = u_ret.shape
    lc = RET_CHUNK
    nc = ta // lc
    ncx = ncx_tokens // lc
    fwd = lambda i: i
    bwd = lambda i: _chunk_of(1, i, ncx, nc)

    def specs(ch):
        return [pl.BlockSpec((1, lc, 2 * RET_QK), lambda bi, i: (bi, ch(i), 0)),
                pl.BlockSpec((1, lc, RET_INNER), lambda bi, i: (bi, ch(i), 1)),
                pl.BlockSpec((lc, 2 * RET_QK), lambda bi, i: (ch(i), 0)),
                pl.BlockSpec((lc, 2 * RET_QK), lambda bi, i: (ch(i), 0))]

    return pl.pallas_call(
        _ret_kernel,
        out_shape=(jax.ShapeDtypeStruct((b, ta, RET_INNER), F32), jax.ShapeDtypeStruct((b, ta, RET_INNER), F32)),
        grid=(b, nc),
        in_specs=specs(fwd) + specs(bwd) + [pl.BlockSpec((2, 1, LANES), lambda bi, i: (0, 0, 0))],
        out_specs=(pl.BlockSpec((1, lc, RET_INNER), lambda bi, i: (bi, i, 0)),
                   pl.BlockSpec((1, lc, RET_INNER), lambda bi, i: (bi, bwd(i), 0))),
        scratch_shapes=[pltpu.VMEM((2, RET_HEADS, RET_KEY_DIM, RET_VAL_DIM), F32)],
        compiler_params=_cparams(2),
        name="ret_scan",
    )(u_ret, u_ret, cosx, sinx, u_ret, u_ret, cosx, sinx, ret_decay)


RWKV_PAIRS = RWKV_HEADS // 2


def _bdiag(tile, bd2):
    return jnp.where(bd2, jnp.concatenate([tile, tile], axis=0), 0.0)


def _split2(x):
    hi = x.astype(BF16)
    return hi, (x - hi.astype(F32)).astype(BF16)


def _head_sums(xs, pair_ones):
    rows = xs[0].shape[0]
    parts = jnp.concatenate([p for x in xs for p in _split2(x)], axis=0)
    blk = parts.shape[0]
    n_pair = parts.shape[1] // LANES
    stacked = jnp.concatenate([parts[:, j * LANES:(j + 1) * LANES] for j in range(n_pair)], axis=0)
    y = jnp.dot(stacked, pair_ones, preferred_element_type=F32)
    full = jnp.concatenate([y[j * blk:(j + 1) * blk] for j in range(n_pair)], axis=1)
    return [full[2 * i * rows:(2 * i + 1) * rows] + full[(2 * i + 1) * rows:(2 * i + 2) * rows]
            for i in range(len(xs))]


def _pair_ones():
    hd = RWKV_HEAD_DIM
    return jnp.where((lax.broadcasted_iota(jnp.int32, (LANES, LANES), 0) < hd)
                     == (lax.broadcasted_iota(jnp.int32, (LANES, LANES), 1) < hd), 1.0, 0.0).astype(BF16)


def _dot3(a, b):
    a1, a2 = _split2(a)
    b1, b2 = _split2(b)
    n = a.shape[0]
    o = jnp.dot(jnp.concatenate([a1, a2], axis=0), b1, preferred_element_type=F32)
    return o[:n] + o[n:] + jnp.dot(a1, b2, preferred_element_type=F32)


def _pair_mm(lhs_splits, w_split, bd2):
    hi = jnp.concatenate([s[0] for s in lhs_splits], axis=0)
    lo = jnp.concatenate([s[1] for s in lhs_splits], axis=0)
    n = hi.shape[0]
    o = jnp.dot(jnp.concatenate([hi, lo], axis=0), _bdiag(w_split[0], bd2), preferred_element_type=F32)
    tot = o[:n] + o[n:] + jnp.dot(hi, _bdiag(w_split[1], bd2), preferred_element_type=F32)
    return [tot[i * RWKV_HEAD_DIM:(i + 1) * RWKV_HEAD_DIM] for i in range(len(lhs_splits))]


def _pair_rows(x_t, j, low_half):
    base = j * LANES
    return jnp.where(low_half, x_t[base:base + RWKV_HEAD_DIM], x_t[base + RWKV_HEAD_DIM:base + LANES])


def _rwkv_prep(d, u, prow, nrow, mix, w0, w2, a0, a2, kkw, ka, rk, bd):
    lc = RWKV_CHUNK
    ni = RWKV_INNER
    t_idx = lax.broadcasted_iota(jnp.int32, (lc, 1), 0)
    prev = jnp.where(t_idx == 0, prow, pltpu.roll(u, 1, axis=0))
    nxt = jnp.where(t_idx == lc - 1, nrow, pltpu.roll(u, lc - 1, axis=0))
    u = u + mix * (0.5 * (prev + nxt) - u)
    r, k, v = u[:, 0:ni], u[:, ni:2 * ni], u[:, 2 * ni:3 * ni]
    o = 3 * ni
    w_lo = u[:, o:o + RWKV_DECAY_LORA]
    a_lo = u[:, o + RWKV_DECAY_LORA:o + RWKV_DECAY_LORA + RWKV_AAA_LORA]
    g_lo = u[:, o + RWKV_DECAY_LORA + RWKV_AAA_LORA:]

    logw = -math.exp(-0.5) * jax.nn.sigmoid(w0 + _dot3(jnp.tanh(w_lo), w2))
    a_gate = jax.nn.sigmoid(a0 + _dot3(a_lo, a2))
    kk = k * kkw
    kd = k * (1.0 + (a_gate - 1.0) * ka)
    kk_ss, rk_sum = _head_sums([kk * kk, r * kd * rk], bd)
    kk = kk / jnp.maximum(jnp.sqrt(kk_ss), 1e-12)
    bvec = kk * a_gate
    bonus = rk_sum * v

    row = lax.broadcasted_iota(jnp.int32, (lc, lc), 0)
    col = lax.broadcasted_iota(jnp.int32, (lc, lc), 1)
    incl = (col <= row) if d == 0 else (col >= row)
    l1, l2, l3 = _split3(logw)
    cw3 = jnp.dot(incl.astype(BF16), jnp.concatenate([l1, l2, l3], axis=1), preferred_element_type=F32)
    cw = cw3[:, 0:ni] + cw3[:, ni:2 * ni] + cw3[:, 2 * ni:3 * ni]
    ctot = jnp.sum(logw, axis=0, keepdims=True)
    e_neg = jnp.exp(-cw)
    e_end = jnp.exp(ctot - cw)

    def dup_t(x):
        return jnp.concatenate([x, x], axis=0).T

    return dict(
        a_t=-kk * jnp.exp(cw - logw), r_t=r * jnp.exp(cw), v=v,
        bn_t=dup_t(bvec * e_neg), kn_t=dup_t(kd * e_neg), be_t=dup_t(bvec * e_end), ke_t=dup_t(kd * e_end),
        wtot_col=jnp.exp(jnp.sum(logw.T, axis=1, keepdims=True)),
        bonus=bonus, g_lo=g_lo)


def _rwkv_kernel(uf_ref, ufp_ref, ufn_ref, ub_ref, ubp_ref, ubn_ref, mix_ref, w0_ref, w2_ref, a0_ref, a2_ref,
                 g2_ref, kk_ref, ka_ref, rk_ref, of_ref, ob_ref, s_ref, *, ncx, nc):
    i = pl.program_id(1)
    lc = RWKV_CHUNK
    ni = RWKV_INNER
    hd = RWKV_HEAD_DIM
    chunk = (i, _chunk_of(1, i, ncx, nc))

    @pl.when(i == 0)
    def _():
        s_ref[...] = jnp.zeros_like(s_ref)

    bd2 = ((lax.broadcasted_iota(jnp.int32, (LANES, LANES), 0) < hd)
           == (lax.broadcasted_iota(jnp.int32, (LANES, LANES), 1) < hd))
    pair_ones = jnp.where(bd2, 1.0, 0.0).astype(BF16)
    u_refs = ((uf_ref, ufp_ref, ufn_ref), (ub_ref, ubp_ref, ubn_ref))
    out_refs = (of_ref, ob_ref)
    prep = []
    for d in (0, 1):
        seg_start, seg_end = _seg_edges(chunk[d], ncx, nc)
        u_ref, up_ref, un_ref = u_refs[d]
        prow = jnp.where(seg_start, 0.0, up_ref[0, SUBLANES - 1:SUBLANES, :])
        nrow = jnp.where(seg_end, 0.0, un_ref[0, 0:1, :])
        q = _rwkv_prep(d, u_ref[0], prow, nrow, mix_ref[...], w0_ref[d], w2_ref[d], a0_ref[d], a2_ref[...],
                       kk_ref[...], ka_ref[...], rk_ref[...], pair_ones)
        out_refs[d][0, :, ni:2 * ni] = q["bonus"]
        if d == 0:
            of_ref[0, :, 2 * ni:3 * ni] = jnp.dot(jax.nn.sigmoid(q["g_lo"]).astype(BF16), g2_ref[...],
                                                  preferred_element_type=F32)
        prep.append(q)

    row4 = lax.broadcasted_iota(jnp.int32, (lc, 2 * LANES), 0)
    col4 = lax.broadcasted_iota(jnp.int32, (lc, 2 * LANES), 1) % hd
    strict4 = (col4 < row4, col4 > row4)
    incl4 = (col4 <= row4, col4 >= row4)
    eye2 = (lax.broadcasted_iota(jnp.int32, (lc, LANES), 1) % hd
            == lax.broadcasted_iota(jnp.int32, (lc, LANES), 0)).astype(F32)
    low_half = lax.broadcasted_iota(jnp.int32, (hd, LANES), 1) < hd
    zeros_w = jnp.zeros((LANES, LANES), F32)

    def dot(a, b):
        return jnp.dot(a.astype(BF16), b.astype(BF16), preferred_element_type=F32)

    items = [(d, j) for d in (0, 1) for j in range(RWKV_PAIRS)]
    pl_ = {j: slice(j * LANES, (j + 1) * LANES) for j in range(RWKV_PAIRS)}
    n_it = range(len(items))
    a_t = [prep[d]["a_t"][:, pl_[j]] for d, j in items]
    r_t = [prep[d]["r_t"][:, pl_[j]] for d, j in items]
    v_w = [_bdiag(prep[d]["v"][:, pl_[j]], bd2) for d, j in items]
    w_p = [jnp.concatenate([jnp.where(bd2, prep[d]["bn_t"][pl_[j], :], 0.0),
                            jnp.where(bd2, prep[d]["kn_t"][pl_[j], :], 0.0)], axis=1) for d, j in items]
    p = [dot(jnp.concatenate([a_t[n], r_t[n]], axis=0), w_p[n]) for n in n_it]
    m_a = [jnp.where(strict4[items[n][0]], p[n][:lc], 0.0) for n in n_it]
    m_r = [jnp.where(incl4[items[n][0]], p[n][lc:], 0.0) for n in n_it]
    mv = [dot(m_a[n][:, LANES:], v_w[n]) for n in n_it]
    x = [m_a[n][:, :LANES] for n in n_it]
    t = [eye2 + x[n] for n in n_it]
    xs = [_split2(x[n]) for n in n_it]
    x = [_pair_mm([xs[n]], xs[n], bd2)[0] for n in n_it]
    for _ in range(int(math.log2(lc)) - 2):
        xs = [_split2(x[n]) for n in n_it]
        tx = [_pair_mm([_split2(t[n]), xs[n]], xs[n], bd2) for n in n_it]
        t = [t[n] + tx[n][0] for n in n_it]
        x = [tx[n][1] for n in n_it]
    t = [t[n] + _pair_mm([_split2(t[n])], _split2(x[n]), bd2)[0] for n in n_it]
    au = [dot(t[n], jnp.concatenate([_bdiag(a_t[n], bd2), _bdiag(mv[n], bd2)], axis=1)) for n in n_it]
    w2 = [jnp.concatenate([jnp.concatenate([_bdiag(au[n][:, :LANES], bd2), _bdiag(au[n][:, LANES:], bd2)], axis=1),
                           jnp.concatenate([zeros_w, v_w[n]], axis=1)], axis=0) for n in n_it]
    q_bk = [jnp.concatenate([_pair_rows(prep[d]["be_t"], j, low_half), _pair_rows(prep[d]["ke_t"], j, low_half)],
                            axis=1) for d, j in items]
    big = [dot(jnp.concatenate([q_bk[n], m_r[n]], axis=0), w2[n]) for n in n_it]
    st = [s_ref[d, j] for d, j in items]
    sy = [dot(jnp.concatenate([r_t[n] + big[n][lc:, :LANES], big[n][:lc, :LANES]], axis=0), _bdiag(st[n], bd2))
          for n in n_it]
    for n, (d, j) in enumerate(items):
        out_refs[d][0, :, pl_[j]] = sy[n][:lc] + big[n][lc:, LANES:]
    for n, (d, j) in enumerate(items):
        wc = jnp.where(low_half,
                       jnp.broadcast_to(prep[d]["wtot_col"][j * LANES:j * LANES + hd, :], (hd, LANES)),
                       jnp.broadcast_to(prep[d]["wtot_col"][j * LANES + hd:(j + 1) * LANES, :], (hd, LANES)))
        s_ref[d, j] = wc * st[n] + sy[n][lc:] + big[n][:lc, LANES:]


def rwkv_scan(u_rwkv, mix, w0, w2, a0, a2, g2, k_k, k_a, r_k, ncx_tokens):
    b, ta, _ = u_rwkv.shape
    lc = RWKV_CHUNK
    nc = ta // lc
    ncx = ncx_tokens // lc
    nb = lc // SUBLANES
    nblk = ta // SUBLANES
    ni = RWKV_INNER
    kern = functools.partial(_rwkv_kernel, ncx=ncx, nc=nc)
    const2 = lambda bi, i: (0, 0)
    const3 = lambda bi, i: (0, 0, 0)
    fwd = lambda i: i
    bwd = lambda i: _chunk_of(1, i, ncx, nc)

    def u_specs(ch):
        return [pl.BlockSpec((1, lc, RWKV_IN), lambda bi, i: (bi, ch(i), 0)),
                pl.BlockSpec((1, SUBLANES, RWKV_IN), lambda bi, i: (bi, jnp.maximum(ch(i) * nb - 1, 0), 0)),
                pl.BlockSpec((1, SUBLANES, RWKV_IN), lambda bi, i: (bi, jnp.minimum((ch(i) + 1) * nb, nblk - 1), 0))]

    return pl.pallas_call(
        kern,
        out_shape=(jax.ShapeDtypeStruct((b, ta, 3 * ni), F32), jax.ShapeDtypeStruct((b, ta, 2 * ni), F32)),
        grid=(b, nc),
        in_specs=u_specs(fwd) + u_specs(bwd) + [
            pl.BlockSpec((1, RWKV_IN), const2),
            pl.BlockSpec((2, 1, ni), const3),
            pl.BlockSpec((2, RWKV_DECAY_LORA, ni), const3),
            pl.BlockSpec((2, 1, ni), const3),
            pl.BlockSpec((RWKV_AAA_LORA, ni), const2),
            pl.BlockSpec((RWKV_GATE_LORA, ni), const2),
            pl.BlockSpec((1, ni), const2),
            pl.BlockSpec((1, ni), const2),
            pl.BlockSpec((1, ni), const2),
        ],
        out_specs=(pl.BlockSpec((1, lc, 3 * ni), lambda bi, i: (bi, i, 0)),
                   pl.BlockSpec((1, lc, 2 * ni), lambda bi, i: (bi, bwd(i), 0))),
        scratch_shapes=[pltpu.VMEM((2, RWKV_PAIRS, RWKV_HEAD_DIM, LANES), F32)],
        compiler_params=_cparams(2),
        name="rwkv_scan",
    )(u_rwkv, u_rwkv, u_rwkv, u_rwkv, u_rwkv, u_rwkv, mix, w0, w2, a0, a2, g2, k_k, k_a, r_k)


def _merge_kernel(ssd_ref, z_ref, retf_ref, retb_ref, rg_ref, rwf_ref, rwb_ref, gate_ref, x_ref, mod_ref, nw_ref,
                  ssdnw_ref, lnw_ref, lnb_ref, wso_ref, wro_ref, wwo_ref, wo_ref, o_ref):
    ys = (ssd_ref[0, 0] + ssd_ref[1, 0]) * _silu(z_ref[0][:, :SSD_INNER])
    parts = []
    for g in range(SSD_GROUPS):
        yg = ys[:, g * SSD_GROUP_W:(g + 1) * SSD_GROUP_W]
        parts.append(yg * lax.rsqrt(jnp.mean(yg * yg, axis=-1, keepdims=True) + EPS))
    ys = jnp.concatenate(parts, axis=1) * ssdnw_ref[...]
    o_ssd = jnp.dot(ys.astype(BF16), wso_ref[...], preferred_element_type=F32)

    yr = retf_ref[0] + retb_ref[0]
    parts = []
    for h in range(RET_HEADS):
        yh = yr[:, h * RET_VAL_DIM:(h + 1) * RET_VAL_DIM]
        yc = yh - jnp.mean(yh, axis=-1, keepdims=True)
        parts.append(yc * lax.rsqrt(jnp.mean(yc * yc, axis=-1, keepdims=True) + EPS))
    yr = jnp.concatenate(parts, axis=1) * _silu(rg_ref[0])
    o_ret = jnp.dot(yr.astype(BF16), wro_ref[...], preferred_element_type=F32)

    ni = RWKV_INNER
    rw0 = rwf_ref[0]
    rw1 = rwb_ref[0]
    yw = rw0[:, :ni] + rw1[:, :ni]
    pair_ones = _pair_ones()
    inv_hd = 1.0 / RWKV_HEAD_DIM
    yc = yw - _head_sums([yw], pair_ones)[0] * inv_hd
    var = _head_sums([yc * yc], pair_ones)[0] * inv_hd
    yw = yc * lax.rsqrt(var + RWKV_LN_EPS) * lnw_ref[...] + lnb_ref[...]
    yw = (yw + rw0[:, ni:2 * ni] + rw1[:, ni:2 * ni]) * rw0[:, 2 * ni:3 * ni]
    o_rw = jnp.dot(yw.astype(BF16), wwo_ref[...], preferred_element_type=F32)

    gate = gate_ref[0]
    merged = (jax.nn.sigmoid(gate[:, :D_MODEL]) * o_ssd
              + jax.nn.sigmoid(gate[:, D_MODEL:2 * D_MODEL]) * o_ret
              + jax.nn.sigmoid(gate[:, 2 * D_MODEL:]) * o_rw)
    yx = jnp.dot(merged.astype(BF16), wo_ref[...], preferred_element_type=F32)
    g1 = mod_ref[0, 0, 2:3, :]
    o_ref[0] = x_ref[0] + g1 * _rms(yx, nw_ref[...])


def merge_out(ssd_y, u_zdt, ret_f, ret_b, u_ret, rw_f, rw_b, u_gate, x, mod, nw1, ssd_nw, ln_w, ln_b,
              w_ssd_out, w_ret_out, w_rwkv_out, w_out, n_ctx_tiles):
    b, ta, dm = x.shape
    tm = ROW_TILE
    row = lambda bi, i: (bi, i, 0)
    both = lambda bi, i: (0, bi, i, 0)
    const2 = lambda bi, i: (0, 0)
    return pl.pallas_call(
        _merge_kernel,
        out_shape=jax.ShapeDtypeStruct((b, ta, dm), F32),
        grid=(b, ta // tm),
        in_specs=[
            pl.BlockSpec((2, 1, tm, SSD_INNER), both),
            pl.BlockSpec((1, tm, u_zdt.shape[2]), row),
            pl.BlockSpec((1, tm, RET_INNER), row),
            pl.BlockSpec((1, tm, RET_INNER), row),
            pl.BlockSpec((1, tm, RET_INNER), lambda bi, i: (bi, i, 2)),
            pl.BlockSpec((1, tm, 3 * RWKV_INNER), row),
            pl.BlockSpec((1, tm, 2 * RWKV_INNER), row),
            pl.BlockSpec((1, tm, N_BRANCH * dm), row),
            pl.BlockSpec((1, tm, dm), row),
            pl.BlockSpec((1, 1, SUBLANES, dm), lambda bi, i: (bi, (i >= n_ctx_tiles).astype(jnp.int32), 0, 0)),
            pl.BlockSpec((1, dm), const2),
            pl.BlockSpec((1, SSD_INNER), const2),
            pl.BlockSpec((1, RWKV_INNER), const2),
            pl.BlockSpec((1, RWKV_INNER), const2),
            pl.BlockSpec((SSD_INNER, dm), const2),
            pl.BlockSpec((RET_INNER, dm), const2),
            pl.BlockSpec((RWKV_INNER, dm), const2),
            pl.BlockSpec((dm, dm), const2),
        ],
        out_specs=pl.BlockSpec((1, tm, dm), row),
        compiler_params=_cparams(2),
        name="merge_out",
    )(ssd_y, u_zdt, ret_f, ret_b, u_ret, rw_f, rw_b, u_gate, x, mod, nw1, ssd_nw, ln_w, ln_b,
      w_ssd_out, w_ret_out, w_rwkv_out, w_out)


def _mlp_kernel(x_ref, mod_ref, nw2_ref, nw3_ref, w1_ref, w2_ref, o_ref, h_ref, acc_ref):
    j = pl.program_id(2)

    @pl.when(j == 0)
    def _():
        y = _rms(x_ref[0], nw2_ref[...])
        h_ref[...] = (y * (1.0 + mod_ref[0, 0, 4:5, :]) + mod_ref[0, 0, 3:4, :]).astype(BF16)
        acc_ref[...] = jnp.zeros_like(acc_ref)

    hid = jnp.dot(h_ref[...], w1_ref[...], preferred_element_type=F32)
    hid = jnp.square(jnp.maximum(hid, 0.0))
    acc_ref[...] += jnp.dot(hid.astype(BF16), w2_ref[...], preferred_element_type=F32)

    @pl.when(j == pl.num_programs(2) - 1)
    def _():
        g2 = mod_ref[0, 0, 5:6, :]
        o_ref[0] = x_ref[0] + g2 * _rms(acc_ref[...], nw3_ref[...])


def mlp_block(x, mod, nw2, nw3, w1, w2, n_ctx_tiles, ff_tile=D_FF):
    b, ta, dm = x.shape
    tm = ROW_TILE
    row = lambda bi, i, j: (bi, i, 0)
    const2 = lambda bi, i, j: (0, 0)
    return pl.pallas_call(
        _mlp_kernel,
        out_shape=jax.ShapeDtypeStruct((b, ta, dm), F32),
        grid=(b, ta // tm, D_FF // ff_tile),
        in_specs=[
            pl.BlockSpec((1, tm, dm), row),
            pl.BlockSpec((1, 1, SUBLANES, dm), lambda bi, i, j: (bi, (i >= n_ctx_tiles).astype(jnp.int32), 0, 0)),
            pl.BlockSpec((1, dm), const2),
            pl.BlockSpec((1, dm), const2),
            pl.BlockSpec((dm, ff_tile), lambda bi, i, j: (0, j)),
            pl.BlockSpec((ff_tile, dm), lambda bi, i, j: (j, 0)),
        ],
        out_specs=pl.BlockSpec((1, tm, dm), row),
        scratch_shapes=[pltpu.VMEM((tm, dm), BF16), pltpu.VMEM((tm, dm), F32)],
        compiler_params=_cparams(3),
        name="mlp",
    )(x, mod, nw2, nw3, w1, w2)


def _rope_tables(n_ctx, n_lat):
    rows = n_lat // GRID_W
    row = np.repeat(np.arange(rows), GRID_W).astype(np.float32)
    col = np.tile(np.arange(GRID_W), rows).astype(np.float32)
    n_freq = RET_KEY_DIM // 4
    inv = jnp.power(ROPE_BASE, -jnp.arange(n_freq, dtype=F32) / n_freq)
    ang = jnp.concatenate([jnp.asarray(row)[:, None] * inv, jnp.asarray(col)[:, None] * inv], axis=-1)
    cos = jnp.concatenate([jnp.ones((n_ctx, RET_KEY_DIM // 2), F32), jnp.cos(ang)], axis=0)
    sin = jnp.concatenate([jnp.zeros((n_ctx, RET_KEY_DIM // 2), F32), jnp.sin(ang)], axis=0)
    cos_h = jnp.concatenate([cos, cos], axis=1)
    sin_h = jnp.concatenate([-sin, sin], axis=1)
    reps = 2 * RET_HEADS
    return jnp.tile(cos_h, (1, reps)), jnp.tile(sin_h, (1, reps))


def _pad_lanes(a, width=LANES):
    return jnp.pad(a, [(0, 0)] * (a.ndim - 1) + [(0, width - a.shape[-1])])


def kernel(x, c, ctx, c_ctx, norm_w, ada_w, ada_b, w_in, ssd_conv_w, ssd_conv_b, ssd_dt_bias, ssd_a_log,
           ssd_d, ssd_norm_w, ret_decay, rwkv_mix, rwkv_w0, rwkv_w2, rwkv_a0, rwkv_a2, rwkv_g2, rwkv_k_k,
           rwkv_k_a, rwkv_r_k, rwkv_lnx_w, rwkv_lnx_b, w_ssd_out, w_ret_out, w_rwkv_out, w_out, mlp_w1, mlp_w2):
    b, n_lat, dm = x.shape
    n_ctx = ctx.shape[1]
    ta = n_ctx + n_lat
    depth = norm_w.shape[0]
    n_ctx_tiles = n_ctx // ROW_TILE
    assert n_ctx % ROW_TILE == 0 and n_lat % ROW_TILE == 0 and b + 1 <= SUBLANES

    hexp = np.zeros((LANES, SSD_INNER), np.float32)
    for h in range(SSD_HEADS):
        hexp[h, h * SSD_HEAD_DIM:(h + 1) * SSD_HEAD_DIM] = 1.0
    hexp = jnp.asarray(hexp, BF16)
    cosx, sinx = _rope_tables(n_ctx, n_lat)

    cond = jnp.zeros((SUBLANES, dm), F32).at[:b].set(c).at[b].set(c_ctx)
    mod_all = modulation_all(cond, ada_w, ada_b)

    sizes = (N_BRANCH * dm, SSD_INNER, SSD_CONV_DIM, 2 * SSD_HEADS, RET_QK, RET_QK, RET_INNER, RET_INNER, RWKV_IN)
    offs = np.concatenate([[0], np.cumsum(sizes)])
    perm = np.concatenate([np.concatenate([np.arange(0, RET_KEY_DIM, 2), np.arange(1, RET_KEY_DIM, 2)]) + h * RET_KEY_DIM
                           for h in range(RET_HEADS)])

    xall = jnp.concatenate([ctx, x], axis=1)
    tm_mm = 1024 if (b * ta) % 1024 == 0 else ROW_TILE
    for l in range(depth):
        wl = w_in[l]
        seg = [wl[:, offs[j]:offs[j + 1]] for j in range(len(sizes))]
        w_gate = seg[0].astype(BF16)
        w_zdt = jnp.concatenate([seg[1], _pad_lanes(seg[3][:, :SSD_HEADS]), _pad_lanes(seg[3][:, SSD_HEADS:])],
                                axis=1).astype(BF16)
        w_xbc = seg[2].astype(BF16)
        w_ret = jnp.concatenate([seg[4][:, perm], seg[5][:, perm], seg[6], seg[7]], axis=1).astype(BF16)
        w_rw = seg[8].astype(BF16)

        m = mod_all[l].reshape(SUBLANES, 6, dm)
        m = jnp.pad(m, ((0, 0), (0, SUBLANES - 6), (0, 0)))
        mod = jnp.stack([jnp.broadcast_to(m[b], (b,) + m.shape[1:]), m[:b]], axis=1)
        nw = norm_w[l]

        h = norm_modulate(xall, nw[0:1], mod, n_ctx_tiles).reshape(b * ta, dm)
        u_gate = matmul(h, w_gate, tm_mm, 1024).reshape(b, ta, -1)
        u_zdt = matmul(h, w_zdt, tm_mm, w_zdt.shape[1]).reshape(b, ta, -1)
        u_xbc = matmul(h, w_xbc, tm_mm, SSD_CONV_DIM).reshape(b, ta, -1)
        u_ret = matmul(h, w_ret, tm_mm, w_ret.shape[1]).reshape(b, ta, -1)
        u_rw = matmul(h, w_rw, tm_mm, RWKV_IN).reshape(b, ta, -1)

        conv_w = jnp.pad(ssd_conv_w[l], ((0, SUBLANES - SSD_CONV), (0, 0)))
        dskip = jnp.repeat(ssd_d[l], SSD_HEAD_DIM)[None, :]
        ssd_y = ssd_scan(u_xbc, u_zdt, conv_w, ssd_conv_b[l][None, :],
                         _pad_lanes(ssd_dt_bias[l])[:, None, :], _pad_lanes(ssd_a_log[l])[:, None, :],
                         dskip, hexp, n_ctx)
        ret_f, ret_b = ret_scan(u_ret, cosx, sinx, _pad_lanes(ret_decay[l])[:, None, :], n_ctx)
        rw_f, rw_b = rwkv_scan(u_rw, rwkv_mix[l][None, :], rwkv_w0[l][:, None, :], rwkv_w2[l], rwkv_a0[l][:, None, :],
                         rwkv_a2[l], rwkv_g2[l].astype(BF16), rwkv_k_k[l][None, :], rwkv_k_a[l][None, :],
                         rwkv_r_k[l].reshape(1, RWKV_INNER), n_ctx)

        xall = merge_out(ssd_y, u_zdt, ret_f, ret_b, u_ret, rw_f, rw_b, u_gate, xall, mod, nw[1:2],
                         ssd_norm_w[l][None, :], rwkv_lnx_w[l][None, :], rwkv_lnx_b[l][None, :],
                         w_ssd_out[l].astype(BF16), w_ret_out[l].astype(BF16), w_rwkv_out[l].astype(BF16),
                         w_out[l].astype(BF16), n_ctx_tiles)
        xall = mlp_block(xall, mod, nw[2:3], nw[3:4], mlp_w1[l].astype(BF16), mlp_w2[l].astype(BF16), n_ctx_tiles)
    return xall[:, n_ctx:, :]
```

```python
import functools
import math

import numpy as np
import jax
import jax.numpy as jnp
from jax import lax
from jax.experimental import pallas as pl
from jax.experimental.pallas import tpu as pltpu

F32 = jnp.float32
BF16 = jnp.bfloat16
HIGHEST = lax.Precision.HIGHEST

D_MODEL = 1024
DEPTH = 4
GRID_W = 64
EPS = 1e-6
N_BRANCH = 3

SSD_HEADS = 16
SSD_HEAD_DIM = 64
SSD_INNER = SSD_HEADS * SSD_HEAD_DIM
SSD_GROUPS = 2
SSD_STATE = 128
SSD_CONV = 5
SSD_CHUNK = 128
SSD_CONV_DIM = SSD_INNER + 2 * SSD_GROUPS * SSD_STATE
SSD_GROUP_W = SSD_INNER // SSD_GROUPS

RET_HEADS = 4
RET_KEY_DIM = 64
RET_VAL_DIM = 128
RET_QK = RET_HEADS * RET_KEY_DIM
RET_INNER = RET_HEADS * RET_VAL_DIM
RET_CHUNK = 128
ROPE_BASE = 10000.0

RWKV_HEADS = 8
RWKV_HEAD_DIM = 64
RWKV_INNER = RWKV_HEADS * RWKV_HEAD_DIM
RWKV_DECAY_LORA = 64
RWKV_AAA_LORA = 64
RWKV_GATE_LORA = 128
RWKV_IN = 3 * RWKV_INNER + RWKV_DECAY_LORA + RWKV_AAA_LORA + RWKV_GATE_LORA
RWKV_LN_EPS = 64e-5
RWKV_CHUNK = 64

D_FF = 4 * D_MODEL

SUBLANES = 8
LANES = 128
VMEM_LIMIT_BYTES = 56 * 1024 * 1024

ROW_TILE = 256
NEG_BIG = -1e30


def _cparams(n_axes):
    return pltpu.CompilerParams(dimension_semantics=("arbitrary",) * n_axes,
                                vmem_limit_bytes=VMEM_LIMIT_BYTES)


def _silu(x):
    return x * jax.nn.sigmoid(x)


def _softplus(x):
    return jnp.maximum(x, 0.0) + jnp.log1p(jnp.exp(-jnp.abs(x)))


def _split3(x):
    x1 = x.astype(BF16)
    r1 = x - x1.astype(F32)
    x2 = r1.astype(BF16)
    r2 = r1 - x2.astype(F32)
    return x1, x2, r2.astype(BF16)


def _dot_sel_rhs(x, sel):
    x1, x2, x3 = _split3(x)
    return (jnp.dot(x1, sel, preferred_element_type=F32)
            + jnp.dot(x2, sel, preferred_element_type=F32)
            + jnp.dot(x3, sel, preferred_element_type=F32))


def _dot_sel_lhs(sel, x):
    x1, x2, x3 = _split3(x)
    return (jnp.dot(sel, x1, preferred_element_type=F32)
            + jnp.dot(sel, x2, preferred_element_type=F32)
            + jnp.dot(sel, x3, preferred_element_type=F32))


def _dot_nt(a, b, precision=None):
    return lax.dot_general(a, b, (((1,), (1,)), ((), ())), preferred_element_type=F32, precision=precision)


def _dot_tn(a, b, precision=None):
    return lax.dot_general(a, b, (((0,), (0,)), ((), ())), preferred_element_type=F32, precision=precision)


def _chunk_of(d, i, ncx, nc):
    bwd = jnp.where(i < ncx, ncx - 1 - i, ncx + nc - 1 - i)
    return jnp.where(d == 0, i, bwd)


def _seg_edges(c, ncx, nc):
    seg_start = jnp.logical_or(c == 0, c == ncx)
    seg_end = jnp.logical_or(c == ncx - 1, c == nc - 1)
    return seg_start, seg_end


def _tri_mask(n, d, strict, reps=1):
    row = lax.broadcasted_iota(jnp.int32, (n, n * reps), 0)
    col = lax.broadcasted_iota(jnp.int32, (n, n * reps), 1) % n
    lead = jnp.where(d == 0, row - col, col - row)
    return lead > 0 if strict else lead >= 0


def _mod_kernel(c_ref, w_ref, b_ref, o_ref):
    s = _silu(c_ref[...])
    o_ref[0] = jnp.dot(s, w_ref[0], preferred_element_type=F32, precision=HIGHEST) + b_ref[0]


def modulation_all(cond, ada_w, ada_b):
    depth = ada_w.shape[0]
    tn = 1536
    return pl.pallas_call(
        _mod_kernel,
        out_shape=jax.ShapeDtypeStruct((depth, SUBLANES, 6 * D_MODEL), F32),
        grid=(depth, 6 * D_MODEL // tn),
        in_specs=[pl.BlockSpec((SUBLANES, D_MODEL), lambda l, j: (0, 0)),
                  pl.BlockSpec((1, D_MODEL, tn), lambda l, j: (l, 0, j)),
                  pl.BlockSpec((1, 1, tn), lambda l, j: (l, 0, j))],
        out_specs=pl.BlockSpec((1, SUBLANES, tn), lambda l, j: (l, 0, j)),
        compiler_params=_cparams(2),
        name="modulation",
    )(cond, ada_w, ada_b.reshape(depth, 1, 6 * D_MODEL))


def _rms(x, w):
    return x * lax.rsqrt(jnp.mean(x * x, axis=-1, keepdims=True) + EPS) * w


def _normmod_kernel(x_ref, nw_ref, mod_ref, h_ref):
    y = _rms(x_ref[0], nw_ref[...])
    shift = mod_ref[0, 0, 0:1, :]
    scale = mod_ref[0, 0, 1:2, :]
    h_ref[0] = (y * (1.0 + scale) + shift).astype(BF16)


def norm_modulate(x, nw, mod, n_ctx_tiles):
    b, ta, dm = x.shape
    return pl.pallas_call(
        _normmod_kernel,
        out_shape=jax.ShapeDtypeStruct((b, ta, dm), BF16),
        grid=(b, ta // ROW_TILE),
        in_specs=[pl.BlockSpec((1, ROW_TILE, dm), lambda bi, i: (bi, i, 0)),
                  pl.BlockSpec((1, dm), lambda bi, i: (0, 0)),
                  pl.BlockSpec((1, 1, SUBLANES, dm), lambda bi, i: (bi, (i >= n_ctx_tiles).astype(jnp.int32), 0, 0))],
        out_specs=pl.BlockSpec((1, ROW_TILE, dm), lambda bi, i: (bi, i, 0)),
        compiler_params=_cparams(2),
        name="norm_modulate",
    )(x, nw, mod)


def _mm_kernel(a_ref, w_ref, o_ref):
    o_ref[...] = jnp.dot(a_ref[...], w_ref[...], preferred_element_type=F32).astype(o_ref.dtype)


def matmul(a, w, tm, tn, out_dtype=F32):
    r, k = a.shape
    n = w.shape[1]
    assert r % tm == 0 and n % tn == 0
    return pl.pallas_call(
        _mm_kernel,
        out_shape=jax.ShapeDtypeStruct((r, n), out_dtype),
        grid=(n // tn, r // tm),
        in_specs=[pl.BlockSpec((tm, k), lambda j, i: (i, 0)),
                  pl.BlockSpec((k, tn), lambda j, i: (0, j))],
        out_specs=pl.BlockSpec((tm, tn), lambda j, i: (i, j)),
        compiler_params=_cparams(2),
        name="in_proj",
    )(a, w)


def _ssd_kernel(xbc_ref, xp_ref, xn_ref, dt_ref, cw_ref, cb_ref, dtb_ref, alog_ref, dskip_ref,
                hexp_ref, y_ref, s_ref, ext_ref, act_ref, *, ncx, nc):
    d = pl.program_id(1)
    i = pl.program_id(2)
    c = _chunk_of(d, i, ncx, nc)
    seg_start, seg_end = _seg_edges(c, ncx, nc)
    lc = SSD_CHUNK
    hb = SUBLANES

    @pl.when(i == 0)
    def _():
        s_ref[...] = jnp.zeros_like(s_ref)

    @pl.when(d == 0)
    def _():
        ext_ref[0:hb, :] = jnp.where(seg_start, 0.0, xp_ref[0])
        ext_ref[hb:hb + lc, :] = xbc_ref[0]
        ext_ref[hb + lc:hb + lc + hb, :] = jnp.where(seg_end, 0.0, xn_ref[0])
        pad = SSD_CONV // 2
        acc = ext_ref[hb - pad:hb - pad + lc, :] * cw_ref[0:1, :]
        for j in range(1, SSD_CONV):
            acc = acc + ext_ref[hb - pad + j:hb - pad + j + lc, :] * cw_ref[j:j + 1, :]
        act = _silu(acc + cb_ref[...])
        act_ref[c] = act.astype(BF16)
        y_ref[0, 0] = act[:, :SSD_INNER] * dskip_ref[...]

    @pl.when(d != 0)
    def _():
        y_ref[0, 0] = jnp.zeros(y_ref.shape[2:], F32)

    xbc = act_ref[c]
    bm = xbc[:, SSD_INNER:SSD_INNER + SSD_GROUPS * SSD_STATE]
    cm = xbc[:, SSD_INNER + SSD_GROUPS * SSD_STATE:]

    dt = _softplus(dt_ref[0] + dtb_ref[0])
    la = dt * (-jnp.exp(alog_ref[0]))
    tri = _tri_mask(lc, d, strict=False)
    ac3 = jnp.dot(tri.astype(BF16), jnp.concatenate(_split3(la), axis=1), preferred_element_type=F32)
    acum = ac3[:, :LANES] + ac3[:, LANES:2 * LANES] + ac3[:, 2 * LANES:]
    atot = jnp.sum(la, axis=0, keepdims=True)
    acum_t = acum.T

    etot8 = jnp.broadcast_to(jnp.exp(atot), (SUBLANES, LANES))
    parts = jnp.concatenate(_split2(jnp.exp(acum)) + _split2(etot8), axis=0)
    ex = jnp.dot(parts, hexp_ref[...], preferred_element_type=F32)
    eacx = ex[0:lc] + ex[lc:2 * lc]
    etotx = (ex[2 * lc:2 * lc + SUBLANES] + ex[2 * lc + SUBLANES:])[0:1, :]
    parts_b = jnp.concatenate([dt.astype(BF16), jnp.exp(atot - acum).astype(BF16)], axis=0)
    ex_b = jnp.dot(parts_b, hexp_ref[...], preferred_element_type=F32).astype(BF16)
    vb = xbc[:, :SSD_INNER] * ex_b[0:lc]
    vend = vb * ex_b[lc:2 * lc]
    hg = SSD_HEADS // SSD_GROUPS
    low_half = lax.broadcasted_iota(jnp.int32, (lc, 2 * SSD_HEAD_DIM), 1) < SSD_HEAD_DIM
    for g in range(SSD_GROUPS):
        gs = slice(g * SSD_STATE, (g + 1) * SSD_STATE)
        gw = slice(g * SSD_GROUP_W, (g + 1) * SSD_GROUP_W)
        cg = cm[:, gs]
        bg = bm[:, gs]
        scores = _dot_nt(cg, bg)
        s_prev = s_ref[g]
        y_off = jnp.dot(cg, s_prev.astype(BF16), preferred_element_type=F32) * eacx[:, gw]
        y_in = []
        for hp in range(hg // 2):
            a_pair = []
            for h in (g * hg + 2 * hp, g * hg + 2 * hp + 1):
                diff = acum[:, h:h + 1] - acum_t[h:h + 1, :]
                a_pair.append((scores * jnp.exp(jnp.where(tri, diff, NEG_BIG))).astype(BF16))
            ps = slice((g * hg + 2 * hp) * SSD_HEAD_DIM, (g * hg + 2 * hp + 2) * SSD_HEAD_DIM)
            v_pair = vb[:, ps]
            w_pair = jnp.concatenate([jnp.where(low_half, v_pair, 0.0), jnp.where(low_half, 0.0, v_pair)], axis=0)
            y_in.append(jnp.dot(jnp.concatenate(a_pair, axis=1), w_pair, preferred_element_type=F32))
        y_ref[0, 0, :, gw] += y_off + jnp.concatenate(y_in, axis=1)
        s_ref[g] = s_prev * etotx[:, gw] + _dot_tn(bg, vend[:, gw])


def ssd_scan(u_xbc, u_zdt, conv_w, conv_b, dt_bias, a_log, dskip, hexp, ncx_tokens):
    b, ta, _ = u_xbc.shape
    lc = SSD_CHUNK
    nc = ta // lc
    ncx = ncx_tokens // lc
    nb = lc // SUBLANES
    nblk = ta // SUBLANES
    ch = functools.partial(_chunk_of, ncx=ncx, nc=nc)
    kern = functools.partial(_ssd_kernel, ncx=ncx, nc=nc)
    zdt_blk0 = SSD_INNER // LANES
    return pl.pallas_call(
        kern,
        out_shape=jax.ShapeDtypeStruct((2, b, ta, SSD_INNER), F32),
        grid=(b, 2, nc),
        in_specs=[
            pl.BlockSpec((1, lc, SSD_CONV_DIM), lambda bi, d, i: (bi, i * (1 - d), 0)),
            pl.BlockSpec((1, SUBLANES, SSD_CONV_DIM), lambda bi, d, i: (bi, jnp.maximum(i * nb - 1, 0) * (1 - d), 0)),
            pl.BlockSpec((1, SUBLANES, SSD_CONV_DIM),
                         lambda bi, d, i: (bi, jnp.minimum((i + 1) * nb, nblk - 1) * (1 - d), 0)),
            pl.BlockSpec((1, lc, LANES), lambda bi, d, i: (bi, ch(d, i), zdt_blk0 + d)),
            pl.BlockSpec((SUBLANES, SSD_CONV_DIM), lambda bi, d, i: (0, 0)),
            pl.BlockSpec((1, SSD_CONV_DIM), lambda bi, d, i: (0, 0)),
            pl.BlockSpec((1, 1, LANES), lambda bi, d, i: (d, 0, 0)),
            pl.BlockSpec((1, 1, LANES), lambda bi, d, i: (d, 0, 0)),
            pl.BlockSpec((1, SSD_INNER), lambda bi, d, i: (0, 0)),
            pl.BlockSpec((LANES, SSD_INNER), lambda bi, d, i: (0, 0)),
        ],
        out_specs=pl.BlockSpec((1, 1, lc, SSD_INNER), lambda bi, d, i: (d, bi, ch(d, i), 0)),
        scratch_shapes=[pltpu.VMEM((SSD_GROUPS, SSD_STATE, SSD_GROUP_W), F32),
                        pltpu.VMEM((lc + 2 * SUBLANES, SSD_CONV_DIM), F32),
                        pltpu.VMEM((nc, lc, SSD_CONV_DIM), BF16)],
        compiler_params=_cparams(3),
        name="ssd_scan",
    )(u_xbc, u_xbc, u_xbc, u_zdt, conv_w, conv_b, dt_bias, a_log, dskip, hexp)


def _ret_kernel(qkf_ref, vf_ref, cosf_ref, sinf_ref, qkb_ref, vb_ref, cosb_ref, sinb_ref, dec_ref,
                yf_ref, yb_ref, s_ref):
    i = pl.program_id(1)
    lc = RET_CHUNK

    @pl.when(i == 0)
    def _():
        s_ref[...] = jnp.zeros_like(s_ref)

    in_refs = ((qkf_ref, vf_ref, cosf_ref, sinf_ref), (qkb_ref, vb_ref, cosb_ref, sinb_ref))
    out_refs = (yf_ref, yb_ref)
    half = RET_KEY_DIM // 2
    width = 2 * RET_QK
    lane = lax.broadcasted_iota(jnp.int32, (lc, width), 1)
    row = lax.broadcasted_iota(jnp.int32, (lc, lc), 0)
    col = lax.broadcasted_iota(jnp.int32, (lc, lc), 1)
    dist = jnp.abs(row - col).astype(F32)
    tri = (col <= row, col >= row)
    t_idx = lax.broadcasted_iota(jnp.int32, (lc, 1), 0).astype(F32)
    pos = (t_idx, lc - 1.0 - t_idx)
    q, k, v, lg = [], [], [], []
    for d in (0, 1):
        qk_ref, v_ref, cos_ref, sin_ref = in_refs[d]
        qk = qk_ref[0]
        swapped = jnp.where((lane % RET_KEY_DIM) < half,
                            pltpu.roll(qk, width - half, axis=1), pltpu.roll(qk, half, axis=1))
        qk = qk * cos_ref[...] + swapped * sin_ref[...]
        q.append(qk[:, :RET_QK].astype(BF16))
        k.append((qk[:, RET_QK:] * (RET_KEY_DIM ** -0.5)).astype(BF16))
        v.append(v_ref[0])
        lg.append(-_softplus(-dec_ref[d]))

    items = [(d, h) for d in (0, 1) for h in range(RET_HEADS)]
    n_it = range(len(items))
    ks = [slice(h * RET_KEY_DIM, (h + 1) * RET_KEY_DIM) for d, h in items]
    vs = [slice(h * RET_VAL_DIM, (h + 1) * RET_VAL_DIM) for d, h in items]
    lgh = [lg[d][:, h:h + 1] for d, h in items]
    qh = [q[d][:, ks[n]] for n, (d, h) in enumerate(items)]
    kh = [k[d][:, ks[n]] for n, (d, h) in enumerate(items)]
    vh = [v[d][:, vs[n]] for n, (d, h) in enumerate(items)]
    scores = [(_dot_nt(qh[n], kh[n]) * jnp.exp(jnp.where(tri[items[n][0]], dist * lgh[n], NEG_BIG))).astype(BF16)
              for n in n_it]
    st = [s_ref[d, h] for d, h in items]
    y_in = [jnp.dot(scores[n], vh[n].astype(BF16), preferred_element_type=F32) for n in n_it]
    y_st = [jnp.dot(qh[n], st[n].astype(BF16), preferred_element_type=F32) for n in n_it]
    for n, (d, h) in enumerate(items):
        out_refs[d][0, :, vs[n]] = y_in[n] + y_st[n] * jnp.exp((pos[d] + 1.0) * lgh[n])
    vend = [(vh[n] * jnp.exp((lc - 1.0 - pos[items[n][0]]) * lgh[n])).astype(BF16) for n in n_it]
    cs = [_dot_tn(kh[n], vend[n]) for n in n_it]
    for n, (d, h) in enumerate(items):
        s_ref[d, h] = st[n] * jnp.exp(lc * lgh[n]) + cs[n]


def ret_scan(u_ret, cosx, sinx, ret_decay, ncx_tokens):
    b, ta, _ = u_ret.shape
    lc = RET_CHUNK
    nc = ta // lc
    ncx = ncx_tokens // lc
    fwd = lambda i: i
    bwd = lambda i: _chunk_of(1, i, ncx, nc)

    def specs(ch):
        return [pl.BlockSpec((1, lc, 2 * RET_QK), lambda bi, i: (bi, ch(i), 0)),
                pl.BlockSpec((1, lc, RET_INNER), lambda bi, i: (bi, ch(i), 1)),
                pl.BlockSpec((lc, 2 * RET_QK), lambda bi, i: (ch(i), 0)),
                pl.BlockSpec((lc, 2 * RET_QK), lambda bi, i: (ch(i), 0))]

    return pl.pallas_call(
        _ret_kernel,
        out_shape=(jax.ShapeDtypeStruct((b, ta, RET_INNER), F32), jax.ShapeDtypeStruct((b, ta, RET_INNER), F32)),
        grid=(b, nc),
        in_specs=specs(fwd) + specs(bwd) + [pl.BlockSpec((2, 1, LANES), lambda bi, i: (0, 0, 0))],
        out_specs=(pl.BlockSpec((1, lc, RET_INNER), lambda bi, i: (bi, i, 0)),
                   pl.BlockSpec((1, lc, RET_INNER), lambda bi, i: (bi, bwd(i), 0))),
        scratch_shapes=[pltpu.VMEM((2, RET_HEADS, RET_KEY_DIM, RET_VAL_DIM), F32)],
        compiler_params=_cparams(2),
        name="ret_scan",
    )(u_ret, u_ret, cosx, sinx, u_ret, u_ret, cosx, sinx, ret_decay)


RWKV_PAIRS = RWKV_HEADS // 2


def _bdiag(tile, bd2):
    return jnp.where(bd2, jnp.concatenate([tile, tile], axis=0), 0.0)


def _split2(x):
    hi = x.astype(BF16)
    return hi, (x - hi.astype(F32)).astype(BF16)


def _head_sums(xs, pair_ones):
    rows = xs[0].shape[0]
    parts = jnp.concatenate([p for x in xs for p in _split2(x)], axis=0)
    blk = parts.shape[0]
    n_pair = parts.shape[1] // LANES
    stacked = jnp.concatenate([parts[:, j * LANES:(j + 1) * LANES] for j in range(n_pair)], axis=0)
    y = jnp.dot(stacked, pair_ones, preferred_element_type=F32)
    full = jnp.concatenate([y[j * blk:(j + 1) * blk] for j in range(n_pair)], axis=1)
    return [full[2 * i * rows:(2 * i + 1) * rows] + full[(2 * i + 1) * rows:(2 * i + 2) * rows]
            for i in range(len(xs))]


def _pair_ones():
    hd = RWKV_HEAD_DIM
    return jnp.where((lax.broadcasted_iota(jnp.int32, (LANES, LANES), 0) < hd)
                     == (lax.broadcasted_iota(jnp.int32, (LANES, LANES), 1) < hd), 1.0, 0.0).astype(BF16)


def _dot3(a, b):
    a1, a2 = _split2(a)
    b1, b2 = _split2(b)
    n = a.shape[0]
    o = jnp.dot(jnp.concatenate([a1, a2], axis=0), b1, preferred_element_type=F32)
    return o[:n] + o[n:] + jnp.dot(a1, b2, preferred_element_type=F32)


def _pair_mm(lhs_splits, w_split, bd2):
    hi = jnp.concatenate([s[0] for s in lhs_splits], axis=0)
    lo = jnp.concatenate([s[1] for s in lhs_splits], axis=0)
    n = hi.shape[0]
    o = jnp.dot(jnp.concatenate([hi, lo], axis=0), _bdiag(w_split[0], bd2), preferred_element_type=F32)
    tot = o[:n] + o[n:] + jnp.dot(hi, _bdiag(w_split[1], bd2), preferred_element_type=F32)
    return [tot[i * RWKV_HEAD_DIM:(i + 1) * RWKV_HEAD_DIM] for i in range(len(lhs_splits))]


def _pair_rows(x_t, j, low_half):
    base = j * LANES
    return jnp.where(low_half, x_t[base:base + RWKV_HEAD_DIM], x_t[base + RWKV_HEAD_DIM:base + LANES])


def _rwkv_prep(d, u, prow, nrow, mix, w0, w2, a0, a2, kkw, ka, rk, bd):
    lc = RWKV_CHUNK
    ni = RWKV_INNER
    t_idx = lax.broadcasted_iota(jnp.int32, (lc, 1), 0)
    prev = jnp.where(t_idx == 0, prow, pltpu.roll(u, 1, axis=0))
    nxt = jnp.where(t_idx == lc - 1, nrow, pltpu.roll(u, lc - 1, axis=0))
    u = u + mix * (0.5 * (prev + nxt) - u)
    r, k, v = u[:, 0:ni], u[:, ni:2 * ni], u[:, 2 * ni:3 * ni]
    o = 3 * ni
    w_lo = u[:, o:o + RWKV_DECAY_LORA]
    a_lo = u[:, o + RWKV_DECAY_LORA:o + RWKV_DECAY_LORA + RWKV_AAA_LORA]
    g_lo = u[:, o + RWKV_DECAY_LORA + RWKV_AAA_LORA:]

    logw = -math.exp(-0.5) * jax.nn.sigmoid(w0 + _dot3(jnp.tanh(w_lo), w2))
    a_gate = jax.nn.sigmoid(a0 + _dot3(a_lo, a2))
    kk = k * kkw
    kd = k * (1.0 + (a_gate - 1.0) * ka)
    kk_ss, rk_sum = _head_sums([kk * kk, r * kd * rk], bd)
    kk = kk / jnp.maximum(jnp.sqrt(kk_ss), 1e-12)
    bvec = kk * a_gate
    bonus = rk_sum * v

    row = lax.broadcasted_iota(jnp.int32, (lc, lc), 0)
    col = lax.broadcasted_iota(jnp.int32, (lc, lc), 1)
    incl = (col <= row) if d == 0 else (col >= row)
    l1, l2, l3 = _split3(logw)
    cw3 = jnp.dot(incl.astype(BF16), jnp.concatenate([l1, l2, l3], axis=1), preferred_element_type=F32)
    cw = cw3[:, 0:ni] + cw3[:, ni:2 * ni] + cw3[:, 2 * ni:3 * ni]
    ctot = jnp.sum(logw, axis=0, keepdims=True)
    e_neg = jnp.exp(-cw)
    e_end = jnp.exp(ctot - cw)

    def dup_t(x):
        return jnp.concatenate([x, x], axis=0).T

    return dict(
        a_t=-kk * jnp.exp(cw - logw), r_t=r * jnp.exp(cw), v=v,
        bn_t=dup_t(bvec * e_neg), kn_t=dup_t(kd * e_neg), be_t=dup_t(bvec * e_end), ke_t=dup_t(kd * e_end),
        wtot_col=jnp.exp(jnp.sum(logw.T, axis=1, keepdims=True)),
        bonus=bonus, g_lo=g_lo)


def _rwkv_kernel(uf_ref, ufp_ref, ufn_ref, ub_ref, ubp_ref, ubn_ref, mix_ref, w0_ref, w2_ref, a0_ref, a2_ref,
                 g2_ref, kk_ref, ka_ref, rk_ref, of_ref, ob_ref, s_ref, *, ncx, nc):
    i = pl.program_id(1)
    lc = RWKV_CHUNK
    ni = RWKV_INNER
    hd = RWKV_HEAD_DIM
    chunk = (i, _chunk_of(1, i, ncx, nc))

    @pl.when(i == 0)
    def _():
        s_ref[...] = jnp.zeros_like(s_ref)

    bd2 = ((lax.broadcasted_iota(jnp.int32, (LANES, LANES), 0) < hd)
           == (lax.broadcasted_iota(jnp.int32, (LANES, LANES), 1) < hd))
    pair_ones = jnp.where(bd2, 1.0, 0.0).astype(BF16)
    u_refs = ((uf_ref, ufp_ref, ufn_ref), (ub_ref, ubp_ref, ubn_ref))
    out_refs = (of_ref, ob_ref)
    prep = []
    for d in (0, 1):
        seg_start, seg_end = _seg_edges(chunk[d], ncx, nc)
        u_ref, up_ref, un_ref = u_refs[d]
        prow = jnp.where(seg_start, 0.0, up_ref[0, SUBLANES - 1:SUBLANES, :])
        nrow = jnp.where(seg_end, 0.0, un_ref[0, 0:1, :])
        q = _rwkv_prep(d, u_ref[0], prow, nrow, mix_ref[...], w0_ref[d], w2_ref[d], a0_ref[d], a2_ref[...],
                       kk_ref[...], ka_ref[...], rk_ref[...], pair_ones)
        out_refs[d][0, :, ni:2 * ni] = q["bonus"]
        if d == 0:
            of_ref[0, :, 2 * ni:3 * ni] = jnp.dot(jax.nn.sigmoid(q["g_lo"]).astype(BF16), g2_ref[...],
                                                  preferred_element_type=F32)
        prep.append(q)

    row4 = lax.broadcasted_iota(jnp.int32, (lc, 2 * LANES), 0)
    col4 = lax.broadcasted_iota(jnp.int32, (lc, 2 * LANES), 1) % hd
    strict4 = (col4 < row4, col4 > row4)
    incl4 = (col4 <= row4, col4 >= row4)
    eye2 = (lax.broadcasted_iota(jnp.int32, (lc, LANES), 1) % hd
            == lax.broadcasted_iota(jnp.int32, (lc, LANES), 0)).astype(F32)
    low_half = lax.broadcasted_iota(jnp.int32, (hd, LANES), 1) < hd
    zeros_w = jnp.zeros((LANES, LANES), F32)

    def dot(a, b):
        return jnp.dot(a.astype(BF16), b.astype(BF16), preferred_element_type=F32)

    items = [(d, j) for d in (0, 1) for j in range(RWKV_PAIRS)]
    pl_ = {j: slice(j * LANES, (j + 1) * LANES) for j in range(RWKV_PAIRS)}
    n_it = range(len(items))
    a_t = [prep[d]["a_t"][:, pl_[j]] for d, j in items]
    r_t = [prep[d]["r_t"][:, pl_[j]] for d, j in items]
    v_w = [_bdiag(prep[d]["v"][:, pl_[j]], bd2) for d, j in items]
    w_p = [jnp.concatenate([jnp.where(bd2, prep[d]["bn_t"][pl_[j], :], 0.0),
                            jnp.where(bd2, prep[d]["kn_t"][pl_[j], :], 0.0)], axis=1) for d, j in items]
    p = [dot(jnp.concatenate([a_t[n], r_t[n]], axis=0), w_p[n]) for n in n_it]
    m_a = [jnp.where(strict4[items[n][0]], p[n][:lc], 0.0) for n in n_it]
    m_r = [jnp.where(incl4[items[n][0]], p[n][lc:], 0.0) for n in n_it]
    mv = [dot(m_a[n][:, LANES:], v_w[n]) for n in n_it]
    x = [m_a[n][:, :LANES] for n in n_it]
    t = [eye2 + x[n] for n in n_it]
    xs = [_split2(x[n]) for n in n_it]
    x = [_pair_mm([xs[n]], xs[n], bd2)[0] for n in n_it]
    for _ in range(int(math.log2(lc)) - 2):
        xs = [_split2(x[n]) for n in n_it]
        tx = [_pair_mm([_split2(t[n]), xs[n]], xs[n], bd2) for n in n_it]
        t = [t[n] + tx[n][0] for n in n_it]
        x = [tx[n][1] for n in n_it]
    t = [t[n] + _pair_mm([_split2(t[n])], _split2(x[n]), bd2)[0] for n in n_it]
    au = [dot(t[n], jnp.concatenate([_bdiag(a_t[n], bd2), _bdiag(mv[n], bd2)], axis=1)) for n in n_it]
    w2 = [jnp.concatenate([jnp.concatenate([_bdiag(au[n][:, :LANES], bd2), _bdiag(au[n][:, LANES:], bd2)], axis=1),
                           jnp.concatenate([zeros_w, v_w[n]], axis=1)], axis=0) for n in n_it]
    q_bk = [jnp.concatenate([_pair_rows(prep[d]["be_t"], j, low_half), _pair_rows(prep[d]["ke_t"], j, low_half)],
                            axis=1) for d, j in items]
    big = [dot(jnp.concatenate([q_bk[n], m_r[n]], axis=0), w2[n]) for n in n_it]
    st = [s_ref[d, j] for d, j in items]
    sy = [dot(jnp.concatenate([r_t[n] + big[n][lc:, :LANES], big[n][:lc, :LANES]], axis=0), _bdiag(st[n], bd2))
          for n in n_it]
    for n, (d, j) in enumerate(items):
        out_refs[d][0, :, pl_[j]] = sy[n][:lc] + big[n][lc:, LANES:]
    for n, (d, j) in enumerate(items):
        wc = jnp.where(low_half,
                       jnp.broadcast_to(prep[d]["wtot_col"][j * LANES:j * LANES + hd, :], (hd, LANES)),
                       jnp.broadcast_to(prep[d]["wtot_col"][j * LANES + hd:(j + 1) * LANES, :], (hd, LANES)))
        s_ref[d, j] = wc * st[n] + sy[n][lc:] + big[n][:lc, LANES:]


def rwkv_scan(u_rwkv, mix, w0, w2, a0, a2, g2, k_k, k_a, r_k, ncx_tokens):
    b, ta, _ = u_rwkv.shape
    lc = RWKV_CHUNK
    nc = ta // lc
    ncx = ncx_tokens // lc
    nb = lc // SUBLANES
    nblk = ta // SUBLANES
    ni = RWKV_INNER
    kern = functools.partial(_rwkv_kernel, ncx=ncx, nc=nc)
    const2 = lambda bi, i: (0, 0)
    const3 = lambda bi, i: (0, 0, 0)
    fwd = lambda i: i
    bwd = lambda i: _chunk_of(1, i, ncx, nc)

    def u_specs(ch):
        return [pl.BlockSpec((1, lc, RWKV_IN), lambda bi, i: (bi, ch(i), 0)),
                pl.BlockSpec((1, SUBLANES, RWKV_IN), lambda bi, i: (bi, jnp.maximum(ch(i) * nb - 1, 0), 0)),
                pl.BlockSpec((1, SUBLANES, RWKV_IN), lambda bi, i: (bi, jnp.minimum((ch(i) + 1) * nb, nblk - 1), 0))]

    return pl.pallas_call(
        kern,
        out_shape=(jax.ShapeDtypeStruct((b, ta, 3 * ni), F32), jax.ShapeDtypeStruct((b, ta, 2 * ni), F32)),
        grid=(b, nc),
        in_specs=u_specs(fwd) + u_specs(bwd) + [
            pl.BlockSpec((1, RWKV_IN), const2),
            pl.BlockSpec((2, 1, ni), const3),
            pl.BlockSpec((2, RWKV_DECAY_LORA, ni), const3),
            pl.BlockSpec((2, 1, ni), const3),
            pl.BlockSpec((RWKV_AAA_LORA, ni), const2),
            pl.BlockSpec((RWKV_GATE_LORA, ni), const2),
            pl.BlockSpec((1, ni), const2),
            pl.BlockSpec((1, ni), const2),
            pl.BlockSpec((1, ni), const2),
        ],
        out_specs=(pl.BlockSpec((1, lc, 3 * ni), lambda bi, i: (bi, i, 0)),
                   pl.BlockSpec((1, lc, 2 * ni), lambda bi, i: (bi, bwd(i), 0))),
        scratch_shapes=[pltpu.VMEM((2, RWKV_PAIRS, RWKV_HEAD_DIM, LANES), F32)],
        compiler_params=_cparams(2),
        name="rwkv_scan",
    )(u_rwkv, u_rwkv, u_rwkv, u_rwkv, u_rwkv, u_rwkv, mix, w0, w2, a0, a2, g2, k_k, k_a, r_k)


def _merge_kernel(ssd_ref, z_ref, retf_ref, retb_ref, rg_ref, rwf_ref, rwb_ref, gate_ref, x_ref, mod_ref, nw_ref,
                  ssdnw_ref, lnw_ref, lnb_ref, wso_ref, wro_ref, wwo_ref, wo_ref, o_ref):
    ys = (ssd_ref[0, 0] + ssd_ref[1, 0]) * _silu(z_ref[0][:, :SSD_INNER])
    parts = []
    for g in range(SSD_GROUPS):
        yg = ys[:, g * SSD_GROUP_W:(g + 1) * SSD_GROUP_W]
        parts.append(yg * lax.rsqrt(jnp.mean(yg * yg, axis=-1, keepdims=True) + EPS))
    ys = jnp.concatenate(parts, axis=1) * ssdnw_ref[...]
    o_ssd = jnp.dot(ys.astype(BF16), wso_ref[...], preferred_element_type=F32)

    yr = retf_ref[0] + retb_ref[0]
    parts = []
    for h in range(RET_HEADS):
        yh = yr[:, h * RET_VAL_DIM:(h + 1) * RET_VAL_DIM]
        yc = yh - jnp.mean(yh, axis=-1, keepdims=True)
        parts.append(yc * lax.rsqrt(jnp.mean(yc * yc, axis=-1, keepdims=True) + EPS))
    yr = jnp.concatenate(parts, axis=1) * _silu(rg_ref[0])
    o_ret = jnp.dot(yr.astype(BF16), wro_ref[...], preferred_element_type=F32)

    ni = RWKV_INNER
    rw0 = rwf_ref[0]
    rw1 = rwb_ref[0]
    yw = rw0[:, :ni] + rw1[:, :ni]
    pair_ones = _pair_ones()
    inv_hd = 1.0 / RWKV_HEAD_DIM
    yc = yw - _head_sums([yw], pair_ones)[0] * inv_hd
    var = _head_sums([yc * yc], pair_ones)[0] * inv_hd
    yw = yc * lax.rsqrt(var + RWKV_LN_EPS) * lnw_ref[...] + lnb_ref[...]
    yw = (yw + rw0[:, ni:2 * ni] + rw1[:, ni:2 * ni]) * rw0[:, 2 * ni:3 * ni]
    o_rw = jnp.dot(yw.astype(BF16), wwo_ref[...], preferred_element_type=F32)

    gate = gate_ref[0]
    merged = (jax.nn.sigmoid(gate[:, :D_MODEL]) * o_ssd
              + jax.nn.sigmoid(gate[:, D_MODEL:2 * D_MODEL]) * o_ret
              + jax.nn.sigmoid(gate[:, 2 * D_MODEL:]) * o_rw)
    yx = jnp.dot(merged.astype(BF16), wo_ref[...], preferred_element_type=F32)
    g1 = mod_ref[0, 0, 2:3, :]
    o_ref[0] = x_ref[0] + g1 * _rms(yx, nw_ref[...])


def merge_out(ssd_y, u_zdt, ret_f, ret_b, u_ret, rw_f, rw_b, u_gate, x, mod, nw1, ssd_nw, ln_w, ln_b,
              w_ssd_out, w_ret_out, w_rwkv_out, w_out, n_ctx_tiles):
    b, ta, dm = x.shape
    tm = ROW_TILE
    row = lambda bi, i: (bi, i, 0)
    both = lambda bi, i: (0, bi, i, 0)
    const2 = lambda bi, i: (0, 0)
    return pl.pallas_call(
        _merge_kernel,
        out_shape=jax.ShapeDtypeStruct((b, ta, dm), F32),
        grid=(b, ta // tm),
        in_specs=[
            pl.BlockSpec((2, 1, tm, SSD_INNER), both),
            pl.BlockSpec((1, tm, u_zdt.shape[2]), row),
            pl.BlockSpec((1, tm, RET_INNER), row),
            pl.BlockSpec((1, tm, RET_INNER), row),
            pl.BlockSpec((1, tm, RET_INNER), lambda bi, i: (bi, i, 2)),
            pl.BlockSpec((1, tm, 3 * RWKV_INNER), row),
            pl.BlockSpec((1, tm, 2 * RWKV_INNER), row),
            pl.BlockSpec((1, tm, N_BRANCH * dm), row),
            pl.BlockSpec((1, tm, dm), row),
            pl.BlockSpec((1, 1, SUBLANES, dm), lambda bi, i: (bi, (i >= n_ctx_tiles).astype(jnp.int32), 0, 0)),
            pl.BlockSpec((1, dm), const2),
            pl.BlockSpec((1, SSD_INNER), const2),
            pl.BlockSpec((1, RWKV_INNER), const2),
            pl.BlockSpec((1, RWKV_INNER), const2),
            pl.BlockSpec((SSD_INNER, dm), const2),
            pl.BlockSpec((RET_INNER, dm), const2),
            pl.BlockSpec((RWKV_INNER, dm), const2),
            pl.BlockSpec((dm, dm), const2),
        ],
        out_specs=pl.BlockSpec((1, tm, dm), row),
        compiler_params=_cparams(2),
        name="merge_out",
    )(ssd_y, u_zdt, ret_f, ret_b, u_ret, rw_f, rw_b, u_gate, x, mod, nw1, ssd_nw, ln_w, ln_b,
      w_ssd_out, w_ret_out, w_rwkv_out, w_out)


def _mlp_kernel(x_ref, mod_ref, nw2_ref, nw3_ref, w1_ref, w2_ref, *rest, ff_tile, with_next):
    nb, tm, dm = x_ref.shape
    x = x_ref[...]
    mod = mod_ref[:, 0]
    y = _rms(x, nw2_ref[...])
    h = (y * (1.0 + mod[:, 4:5, :]) + mod[:, 3:4, :]).astype(BF16).reshape(nb * tm, dm)
    acc = jnp.zeros((nb * tm, dm), F32)
    for j in range(D_FF // ff_tile):
        hid = jnp.dot(h, w1_ref[:, j * ff_tile:(j + 1) * ff_tile], preferred_element_type=F32)
        hid = jnp.square(jnp.maximum(hid, 0.0)).astype(BF16)
        acc = acc + jnp.dot(hid, w2_ref[j * ff_tile:(j + 1) * ff_tile, :], preferred_element_type=F32)
    x_new = x + mod[:, 5:6, :] * _rms(acc.reshape(nb, tm, dm), nw3_ref[...])
    if with_next:
        nwn_ref, modn_ref, o_ref, h_ref = rest
        modn = modn_ref[:, 0]
        h_ref[...] = (_rms(x_new, nwn_ref[...]) * (1.0 + modn[:, 1:2, :]) + modn[:, 0:1, :]).astype(BF16)
    else:
        o_ref, = rest
    o_ref[...] = x_new


def mlp_block(x, mod, nw2, nw3, w1, w2, n_ctx_tiles, skip_tiles=0, next_norm=None, ff_tile=1024):
    b, ta, dm = x.shape
    tm = ROW_TILE
    rows = lambda i: (0, i, 0)
    const2 = lambda i: (0, 0)
    mod_spec = pl.BlockSpec((b, 1, SUBLANES, dm),
                            lambda i: (0, (i + skip_tiles >= n_ctx_tiles).astype(jnp.int32), 0, 0))
    resident = pl.Buffered(1)
    in_specs = [
        pl.BlockSpec((b, tm, dm), lambda i: (0, i + skip_tiles, 0)),
        mod_spec,
        pl.BlockSpec((1, dm), const2),
        pl.BlockSpec((1, dm), const2),
        pl.BlockSpec((dm, D_FF), const2, pipeline_mode=resident),
        pl.BlockSpec((D_FF, dm), const2, pipeline_mode=resident),
    ]
    args = [x, mod, nw2, nw3, w1, w2]
    out_rows = ta - skip_tiles * tm
    out_shape = jax.ShapeDtypeStruct((b, out_rows, dm), F32)
    out_specs = pl.BlockSpec((b, tm, dm), rows)
    if next_norm is not None:
        in_specs += [pl.BlockSpec((1, dm), const2), mod_spec]
        args += list(next_norm)
        out_shape = (out_shape, jax.ShapeDtypeStruct((b, out_rows, dm), BF16))
        out_specs = (out_specs, pl.BlockSpec((b, tm, dm), rows))
    return pl.pallas_call(
        functools.partial(_mlp_kernel, ff_tile=ff_tile, with_next=next_norm is not None),
        out_shape=out_shape,
        grid=(ta // tm - skip_tiles,),
        in_specs=in_specs,
        out_specs=out_specs,
        compiler_params=_cparams(1),
        name="mlp",
    )(*args)


def _rope_tables(n_ctx, n_lat):
    rows = n_lat // GRID_W
    row = np.repeat(np.arange(rows), GRID_W).astype(np.float32)
    col = np.tile(np.arange(GRID_W), rows).astype(np.float32)
    n_freq = RET_KEY_DIM // 4
    inv = jnp.power(ROPE_BASE, -jnp.arange(n_freq, dtype=F32) / n_freq)
    ang = jnp.concatenate([jnp.asarray(row)[:, None] * inv, jnp.asarray(col)[:, None] * inv], axis=-1)
    cos = jnp.concatenate([jnp.ones((n_ctx, RET_KEY_DIM // 2), F32), jnp.cos(ang)], axis=0)
    sin = jnp.concatenate([jnp.zeros((n_ctx, RET_KEY_DIM // 2), F32), jnp.sin(ang)], axis=0)
    cos_h = jnp.concatenate([cos, cos], axis=1)
    sin_h = jnp.concatenate([-sin, sin], axis=1)
    reps = 2 * RET_HEADS
    return jnp.tile(cos_h, (1, reps)), jnp.tile(sin_h, (1, reps))


def _pad_lanes(a, width=LANES):
    return jnp.pad(a, [(0, 0)] * (a.ndim - 1) + [(0, width - a.shape[-1])])


def kernel(x, c, ctx, c_ctx, norm_w, ada_w, ada_b, w_in, ssd_conv_w, ssd_conv_b, ssd_dt_bias, ssd_a_log,
           ssd_d, ssd_norm_w, ret_decay, rwkv_mix, rwkv_w0, rwkv_w2, rwkv_a0, rwkv_a2, rwkv_g2, rwkv_k_k,
           rwkv_k_a, rwkv_r_k, rwkv_lnx_w, rwkv_lnx_b, w_ssd_out, w_ret_out, w_rwkv_out, w_out, mlp_w1, mlp_w2):
    b, n_lat, dm = x.shape
    n_ctx = ctx.shape[1]
    ta = n_ctx + n_lat
    depth = norm_w.shape[0]
    n_ctx_tiles = n_ctx // ROW_TILE
    assert n_ctx % ROW_TILE == 0 and n_lat % ROW_TILE == 0 and b + 1 <= SUBLANES

    hexp = np.zeros((LANES, SSD_INNER), np.float32)
    for h in range(SSD_HEADS):
        hexp[h, h * SSD_HEAD_DIM:(h + 1) * SSD_HEAD_DIM] = 1.0
    hexp = jnp.asarray(hexp, BF16)
    cosx, sinx = _rope_tables(n_ctx, n_lat)

    cond = jnp.zeros((SUBLANES, dm), F32).at[:b].set(c).at[b].set(c_ctx)
    mod_all = modulation_all(cond, ada_w, ada_b)

    sizes = (N_BRANCH * dm, SSD_INNER, SSD_CONV_DIM, 2 * SSD_HEADS, RET_QK, RET_QK, RET_INNER, RET_INNER, RWKV_IN)
    offs = np.concatenate([[0], np.cumsum(sizes)])
    perm = np.concatenate([np.concatenate([np.arange(0, RET_KEY_DIM, 2), np.arange(1, RET_KEY_DIM, 2)]) + h * RET_KEY_DIM
                           for h in range(RET_HEADS)])

    xall = jnp.concatenate([ctx, x], axis=1)
    mods = []
    for l in range(depth):
        m = mod_all[l].reshape(SUBLANES, 6, dm)
        m = jnp.pad(m, ((0, 0), (0, SUBLANES - 6), (0, 0)))
        mods.append(jnp.stack([jnp.broadcast_to(m[b], (b,) + m.shape[1:]), m[:b]], axis=1))
    h = norm_modulate(xall, norm_w[0][0:1], mods[0], n_ctx_tiles)
    tm_mm = 1024 if (b * ta) % 1024 == 0 else ROW_TILE
    for l in range(depth):
        wl = w_in[l]
        seg = [wl[:, offs[j]:offs[j + 1]] for j in range(len(sizes))]
        w_gate = seg[0].astype(BF16)
        w_zdt = jnp.concatenate([seg[1], _pad_lanes(seg[3][:, :SSD_HEADS]), _pad_lanes(seg[3][:, SSD_HEADS:])],
                                axis=1).astype(BF16)
        w_xbc = seg[2].astype(BF16)
        w_ret = jnp.concatenate([seg[4][:, perm], seg[5][:, perm], seg[6], seg[7]], axis=1).astype(BF16)
        w_rw = seg[8].astype(BF16)

        mod = mods[l]
        nw = norm_w[l]
        h = h.reshape(b * ta, dm)
        u_gate = matmul(h, w_gate, tm_mm, 1024).reshape(b, ta, -1)
        u_zdt = matmul(h, w_zdt, tm_mm, w_zdt.shape[1]).reshape(b, ta, -1)
        u_xbc = matmul(h, w_xbc, tm_mm, SSD_CONV_DIM).reshape(b, ta, -1)
        u_ret = matmul(h, w_ret, tm_mm, w_ret.shape[1]).reshape(b, ta, -1)
        u_rw = matmul(h, w_rw, tm_mm, RWKV_IN).reshape(b, ta, -1)

        conv_w = jnp.pad(ssd_conv_w[l], ((0, SUBLANES - SSD_CONV), (0, 0)))
        dskip = jnp.repeat(ssd_d[l], SSD_HEAD_DIM)[None, :]
        ssd_y = ssd_scan(u_xbc, u_zdt, conv_w, ssd_conv_b[l][None, :],
                         _pad_lanes(ssd_dt_bias[l])[:, None, :], _pad_lanes(ssd_a_log[l])[:, None, :],
                         dskip, hexp, n_ctx)
        ret_f, ret_b = ret_scan(u_ret, cosx, sinx, _pad_lanes(ret_decay[l])[:, None, :], n_ctx)
        rw_f, rw_b = rwkv_scan(u_rw, rwkv_mix[l][None, :], rwkv_w0[l][:, None, :], rwkv_w2[l], rwkv_a0[l][:, None, :],
                         rwkv_a2[l], rwkv_g2[l].astype(BF16), rwkv_k_k[l][None, :], rwkv_k_a[l][None, :],
                         rwkv_r_k[l].reshape(1, RWKV_INNER), n_ctx)

        xall = merge_out(ssd_y, u_zdt, ret_f, ret_b, u_ret, rw_f, rw_b, u_gate, xall, mod, nw[1:2],
                         ssd_norm_w[l][None, :], rwkv_lnx_w[l][None, :], rwkv_lnx_b[l][None, :],
                         w_ssd_out[l].astype(BF16), w_ret_out[l].astype(BF16), w_rwkv_out[l].astype(BF16),
                         w_out[l].astype(BF16), n_ctx_tiles)
        w1, w2 = mlp_w1[l].astype(BF16), mlp_w2[l].astype(BF16)
        if l + 1 < depth:
            xall, h = mlp_block(xall, mod, nw[2:3], nw[3:4], w1, w2, n_ctx_tiles,
                                next_norm=(norm_w[l + 1][0:1], mods[l + 1]))
        else:
            xall = mlp_block(xall, mod, nw[2:3], nw[3:4], w1, w2, n_ctx_tiles, skip_tiles=n_ctx_tiles)
    return xall
```

```python
import functools
import math

import numpy as np
import jax
import jax.numpy as jnp
from jax import lax
from jax.experimental import pallas as pl
from jax.experimental.pallas import tpu as pltpu

F32 = jnp.float32
BF16 = jnp.bfloat16
HIGHEST = lax.Precision.HIGHEST

D_MODEL = 1024
DEPTH = 4
GRID_W = 64
EPS = 1e-6
N_BRANCH = 3

SSD_HEADS = 16
SSD_HEAD_DIM = 64
SSD_INNER = SSD_HEADS * SSD_HEAD_DIM
SSD_GROUPS = 2
SSD_STATE = 128
SSD_CONV = 5
SSD_CHUNK = 128
SSD_CONV_DIM = SSD_INNER + 2 * SSD_GROUPS * SSD_STATE
SSD_GROUP_W = SSD_INNER // SSD_GROUPS

RET_HEADS = 4
RET_KEY_DIM = 64
RET_VAL_DIM = 128
RET_QK = RET_HEADS * RET_KEY_DIM
RET_INNER = RET_HEADS * RET_VAL_DIM
RET_CHUNK = 128
ROPE_BASE = 10000.0

RWKV_HEADS = 8
RWKV_HEAD_DIM = 64
RWKV_INNER = RWKV_HEADS * RWKV_HEAD_DIM
RWKV_DECAY_LORA = 64
RWKV_AAA_LORA = 64
RWKV_GATE_LORA = 128
RWKV_IN = 3 * RWKV_INNER + RWKV_DECAY_LORA + RWKV_AAA_LORA + RWKV_GATE_LORA
RWKV_LN_EPS = 64e-5
RWKV_CHUNK = 64

D_FF = 4 * D_MODEL

SUBLANES = 8
LANES = 128
VMEM_LIMIT_BYTES = 56 * 1024 * 1024

ROW_TILE = 256
NEG_BIG = -1e30


def _cparams(n_axes):
    return pltpu.CompilerParams(dimension_semantics=("arbitrary",) * n_axes,
                                vmem_limit_bytes=VMEM_LIMIT_BYTES)


def _silu(x):
    return x * jax.nn.sigmoid(x)


def _softplus(x):
    return jnp.maximum(x, 0.0) + jnp.log1p(jnp.exp(-jnp.abs(x)))


def _split3(x):
    x1 = x.astype(BF16)
    r1 = x - x1.astype(F32)
    x2 = r1.astype(BF16)
    r2 = r1 - x2.astype(F32)
    return x1, x2, r2.astype(BF16)


def _dot_sel_rhs(x, sel):
    x1, x2, x3 = _split3(x)
    return (jnp.dot(x1, sel, preferred_element_type=F32)
            + jnp.dot(x2, sel, preferred_element_type=F32)
            + jnp.dot(x3, sel, preferred_element_type=F32))


def _dot_sel_lhs(sel, x):
    x1, x2, x3 = _split3(x)
    return (jnp.dot(sel, x1, preferred_element_type=F32)
            + jnp.dot(sel, x2, preferred_element_type=F32)
            + jnp.dot(sel, x3, preferred_element_type=F32))


def _dot_nt(a, b, precision=None):
    return lax.dot_general(a, b, (((1,), (1,)), ((), ())), preferred_element_type=F32, precision=precision)


def _dot_tn(a, b, precision=None):
    return lax.dot_general(a, b, (((0,), (0,)), ((), ())), preferred_element_type=F32, precision=precision)


def _chunk_of(d, i, ncx, nc):
    bwd = jnp.where(i < ncx, ncx - 1 - i, ncx + nc - 1 - i)
    return jnp.where(d == 0, i, bwd)


def _seg_edges(c, ncx, nc):
    seg_start = jnp.logical_or(c == 0, c == ncx)
    seg_end = jnp.logical_or(c == ncx - 1, c == nc - 1)
    return seg_start, seg_end


def _tri_mask(n, d, strict, reps=1):
    row = lax.broadcasted_iota(jnp.int32, (n, n * reps), 0)
    col = lax.broadcasted_iota(jnp.int32, (n, n * reps), 1) % n
    lead = jnp.where(d == 0, row - col, col - row)
    return lead > 0 if strict else lead >= 0


def _mod_kernel(c_ref, w_ref, b_ref, o_ref):
    s = _silu(c_ref[...])
    o_ref[0] = jnp.dot(s, w_ref[0], preferred_element_type=F32, precision=HIGHEST) + b_ref[0]


def modulation_all(cond, ada_w, ada_b):
    depth = ada_w.shape[0]
    tn = 1536
    return pl.pallas_call(
        _mod_kernel,
        out_shape=jax.ShapeDtypeStruct((depth, SUBLANES, 6 * D_MODEL), F32),
        grid=(depth, 6 * D_MODEL // tn),
        in_specs=[pl.BlockSpec((SUBLANES, D_MODEL), lambda l, j: (0, 0)),
                  pl.BlockSpec((1, D_MODEL, tn), lambda l, j: (l, 0, j)),
                  pl.BlockSpec((1, 1, tn), lambda l, j: (l, 0, j))],
        out_specs=pl.BlockSpec((1, SUBLANES, tn), lambda l, j: (l, 0, j)),
        compiler_params=_cparams(2),
        name="modulation",
    )(cond, ada_w, ada_b.reshape(depth, 1, 6 * D_MODEL))


def _rms(x, w):
    return x * lax.rsqrt(jnp.mean(x * x, axis=-1, keepdims=True) + EPS) * w


def _normmod_kernel(x_ref, nw_ref, mod_ref, h_ref):
    y = _rms(x_ref[0], nw_ref[...])
    shift = mod_ref[0, 0, 0:1, :]
    scale = mod_ref[0, 0, 1:2, :]
    h_ref[0] = (y * (1.0 + scale) + shift).astype(BF16)


def norm_modulate(x, nw, mod, n_ctx_tiles):
    b, ta, dm = x.shape
    return pl.pallas_call(
        _normmod_kernel,
        out_shape=jax.ShapeDtypeStruct((b, ta, dm), BF16),
        grid=(b, ta // ROW_TILE),
        in_specs=[pl.BlockSpec((1, ROW_TILE, dm), lambda bi, i: (bi, i, 0)),
                  pl.BlockSpec((1, dm), lambda bi, i: (0, 0)),
                  pl.BlockSpec((1, 1, SUBLANES, dm), lambda bi, i: (bi, (i >= n_ctx_tiles).astype(jnp.int32), 0, 0))],
        out_specs=pl.BlockSpec((1, ROW_TILE, dm), lambda bi, i: (bi, i, 0)),
        compiler_params=_cparams(2),
        name="norm_modulate",
    )(x, nw, mod)


def _mm_kernel(a_ref, w_ref, o_ref):
    o_ref[...] = jnp.dot(a_ref[...], w_ref[...], preferred_element_type=F32).astype(o_ref.dtype)


def matmul(a, w, tm, tn, out_dtype=F32):
    r, k = a.shape
    n = w.shape[1]
    assert r % tm == 0 and n % tn == 0
    return pl.pallas_call(
        _mm_kernel,
        out_shape=jax.ShapeDtypeStruct((r, n), out_dtype),
        grid=(n // tn, r // tm),
        in_specs=[pl.BlockSpec((tm, k), lambda j, i: (i, 0)),
                  pl.BlockSpec((k, tn), lambda j, i: (0, j))],
        out_specs=pl.BlockSpec((tm, tn), lambda j, i: (i, j)),
        compiler_params=_cparams(2),
        name="in_proj",
    )(a, w)


def _ssd_kernel(xbc_ref, xp_ref, xn_ref, dt_ref, cw_ref, cb_ref, dtb_ref, alog_ref, dskip_ref,
                hexp_ref, y_ref, s_ref, ext_ref, act_ref, *, ncx, nc):
    d = pl.program_id(1)
    i = pl.program_id(2)
    c = _chunk_of(d, i, ncx, nc)
    seg_start, seg_end = _seg_edges(c, ncx, nc)
    lc = SSD_CHUNK
    hb = SUBLANES

    @pl.when(i == 0)
    def _():
        s_ref[...] = jnp.zeros_like(s_ref)

    @pl.when(d == 0)
    def _():
        ext_ref[0:hb, :] = jnp.where(seg_start, 0.0, xp_ref[0])
        ext_ref[hb:hb + lc, :] = xbc_ref[0]
        ext_ref[hb + lc:hb + lc + hb, :] = jnp.where(seg_end, 0.0, xn_ref[0])
        pad = SSD_CONV // 2
        acc = ext_ref[hb - pad:hb - pad + lc, :] * cw_ref[0:1, :]
        for j in range(1, SSD_CONV):
            acc = acc + ext_ref[hb - pad + j:hb - pad + j + lc, :] * cw_ref[j:j + 1, :]
        act = _silu(acc + cb_ref[...])
        act_ref[c] = act.astype(BF16)
        y_ref[0, 0] = act[:, :SSD_INNER] * dskip_ref[...]

    @pl.when(d != 0)
    def _():
        y_ref[0, 0] = jnp.zeros(y_ref.shape[2:], F32)

    xbc = act_ref[c]
    bm = xbc[:, SSD_INNER:SSD_INNER + SSD_GROUPS * SSD_STATE]
    cm = xbc[:, SSD_INNER + SSD_GROUPS * SSD_STATE:]

    dt = _softplus(dt_ref[0] + dtb_ref[0])
    la = dt * (-jnp.exp(alog_ref[0]))
    tri = _tri_mask(lc, d, strict=False)
    ac3 = jnp.dot(tri.astype(BF16), jnp.concatenate(_split3(la), axis=1), preferred_element_type=F32)
    acum = ac3[:, :LANES] + ac3[:, LANES:2 * LANES] + ac3[:, 2 * LANES:]
    atot = jnp.sum(la, axis=0, keepdims=True)
    acum_t = acum.T

    etot8 = jnp.broadcast_to(jnp.exp(atot), (SUBLANES, LANES))
    parts = jnp.concatenate(_split2(jnp.exp(acum)) + _split2(etot8), axis=0)
    ex = jnp.dot(parts, hexp_ref[...], preferred_element_type=F32)
    eacx = ex[0:lc] + ex[lc:2 * lc]
    etotx = (ex[2 * lc:2 * lc + SUBLANES] + ex[2 * lc + SUBLANES:])[0:1, :]
    parts_b = jnp.concatenate([dt.astype(BF16), jnp.exp(atot - acum).astype(BF16)], axis=0)
    ex_b = jnp.dot(parts_b, hexp_ref[...], preferred_element_type=F32).astype(BF16)
    vb = xbc[:, :SSD_INNER] * ex_b[0:lc]
    vend = vb * ex_b[lc:2 * lc]
    hg = SSD_HEADS // SSD_GROUPS
    low_half = lax.broadcasted_iota(jnp.int32, (lc, 2 * SSD_HEAD_DIM), 1) < SSD_HEAD_DIM
    for g in range(SSD_GROUPS):
        gs = slice(g * SSD_STATE, (g + 1) * SSD_STATE)
        gw = slice(g * SSD_GROUP_W, (g + 1) * SSD_GROUP_W)
        cg = cm[:, gs]
        bg = bm[:, gs]
        scores = _dot_nt(cg, bg)
        s_prev = s_ref[g]
        y_off = jnp.dot(cg, s_prev.astype(BF16), preferred_element_type=F32) * eacx[:, gw]
        y_in = []
        for hp in range(hg // 2):
            a_pair = []
            for h in (g * hg + 2 * hp, g * hg + 2 * hp + 1):
                diff = acum[:, h:h + 1] - acum_t[h:h + 1, :]
                a_pair.append((scores * jnp.exp(jnp.where(tri, diff, NEG_BIG))).astype(BF16))
            ps = slice((g * hg + 2 * hp) * SSD_HEAD_DIM, (g * hg + 2 * hp + 2) * SSD_HEAD_DIM)
            v_pair = vb[:, ps]
            w_pair = jnp.concatenate([jnp.where(low_half, v_pair, 0.0), jnp.where(low_half, 0.0, v_pair)], axis=0)
            y_in.append(jnp.dot(jnp.concatenate(a_pair, axis=1), w_pair, preferred_element_type=F32))
        y_ref[0, 0, :, gw] += y_off + jnp.concatenate(y_in, axis=1)
        s_ref[g] = s_prev * etotx[:, gw] + _dot_tn(bg, vend[:, gw])


def ssd_scan(u_xbc, u_zdt, conv_w, conv_b, dt_bias, a_log, dskip, hexp, ncx_tokens):
    b, ta, _ = u_xbc.shape
    lc = SSD_CHUNK
    nc = ta // lc
    ncx = ncx_tokens // lc
    nb = lc // SUBLANES
    nblk = ta // SUBLANES
    ch = functools.partial(_chunk_of, ncx=ncx, nc=nc)
    kern = functools.partial(_ssd_kernel, ncx=ncx, nc=nc)
    zdt_blk0 = SSD_INNER // LANES
    return pl.pallas_call(
        kern,
        out_shape=jax.ShapeDtypeStruct((2, b, ta, SSD_INNER), F32),
        grid=(b, 2, nc),
        in_specs=[
            pl.BlockSpec((1, lc, SSD_CONV_DIM), lambda bi, d, i: (bi, i * (1 - d), 0)),
            pl.BlockSpec((1, SUBLANES, SSD_CONV_DIM), lambda bi, d, i: (bi, jnp.maximum(i * nb - 1, 0) * (1 - d), 0)),
            pl.BlockSpec((1, SUBLANES, SSD_CONV_DIM),
                         lambda bi, d, i: (bi, jnp.minimum((i + 1) * nb, nblk - 1) * (1 - d), 0)),
            pl.BlockSpec((1, lc, LANES), lambda bi, d, i: (bi, ch(d, i), zdt_blk0 + d)),
            pl.BlockSpec((SUBLANES, SSD_CONV_DIM), lambda bi, d, i: (0, 0)),
            pl.BlockSpec((1, SSD_CONV_DIM), lambda bi, d, i: (0, 0)),
            pl.BlockSpec((1, 1, LANES), lambda bi, d, i: (d, 0, 0)),
            pl.BlockSpec((1, 1, LANES), lambda bi, d, i: (d, 0, 0)),
            pl.BlockSpec((1, SSD_INNER), lambda bi, d, i: (0, 0)),
            pl.BlockSpec((LANES, SSD_INNER), lambda bi, d, i: (0, 0)),
        ],
        out_specs=pl.BlockSpec((1, 1, lc, SSD_INNER), lambda bi, d, i: (d, bi, ch(d, i), 0)),
        scratch_shapes=[pltpu.VMEM((SSD_GROUPS, SSD_STATE, SSD_GROUP_W), F32),
                        pltpu.VMEM((lc + 2 * SUBLANES, SSD_CONV_DIM), F32),
                        pltpu.VMEM((nc, lc, SSD_CONV_DIM), BF16)],
        compiler_params=_cparams(3),
        name="ssd_scan",
    )(u_xbc, u_xbc, u_xbc, u_zdt, conv_w, conv_b, dt_bias, a_log, dskip, hexp)


def _ret_kernel(qkf_ref, vf_ref, cosf_ref, sinf_ref, qkb_ref, vb_ref, cosb_ref, sinb_ref, dec_ref,
                yf_ref, yb_ref, s_ref):
    i = pl.program_id(1)
    lc = RET_CHUNK

    @pl.when(i == 0)
    def _():
        s_ref[...] = jnp.zeros_like(s_ref)

    in_refs = ((qkf_ref, vf_ref, cosf_ref, sinf_ref), (qkb_ref, vb_ref, cosb_ref, sinb_ref))
    out_refs = (yf_ref, yb_ref)
    half = RET_KEY_DIM // 2
    width = 2 * RET_QK
    lane = lax.broadcasted_iota(jnp.int32, (lc, width), 1)
    row = lax.broadcasted_iota(jnp.int32, (lc, lc), 0)
    col = lax.broadcasted_iota(jnp.int32, (lc, lc), 1)
    dist = jnp.abs(row - col).astype(F32)
    tri = (col <= row, col >= row)
    t_idx = lax.broadcasted_iota(jnp.int32, (lc, 1), 0).astype(F32)
    pos = (t_idx, lc - 1.0 - t_idx)
    q, k, v, lg = [], [], [], []
    for d in (0, 1):
        qk_ref, v_ref, cos_ref, sin_ref = in_refs[d]
        qk = qk_ref[0]
        swapped = jnp.where((lane % RET_KEY_DIM) < half,
                            pltpu.roll(qk, width - half, axis=1), pltpu.roll(qk, half, axis=1))
        qk = qk * cos_ref[...] + swapped * sin_ref[...]
        q.append(qk[:, :RET_QK].astype(BF16))
        k.append((qk[:, RET_QK:] * (RET_KEY_DIM ** -0.5)).astype(BF16))
        v.append(v_ref[0])
        lg.append(-_softplus(-dec_ref[d]))

    items = [(d, h) for d in (0, 1) for h in range(RET_HEADS)]
    n_it = range(len(items))
    ks = [slice(h * RET_KEY_DIM, (h + 1) * RET_KEY_DIM) for d, h in items]
    vs = [slice(h * RET_VAL_DIM, (h + 1) * RET_VAL_DIM) for d, h in items]
    lgh = [lg[d][:, h:h + 1] for d, h in items]
    qh = [q[d][:, ks[n]] for n, (d, h) in enumerate(items)]
    kh = [k[d][:, ks[n]] for n, (d, h) in enumerate(items)]
    vh = [v[d][:, vs[n]] for n, (d, h) in enumerate(items)]
    scores = [(_dot_nt(qh[n], kh[n]) * jnp.exp(jnp.where(tri[items[n][0]], dist * lgh[n], NEG_BIG))).astype(BF16)
              for n in n_it]
    st = [s_ref[d, h] for d, h in items]
    y_in = [jnp.dot(scores[n], vh[n].astype(BF16), preferred_element_type=F32) for n in n_it]
    y_st = [jnp.dot(qh[n], st[n].astype(BF16), preferred_element_type=F32) for n in n_it]
    for n, (d, h) in enumerate(items):
        out_refs[d][0, :, vs[n]] = y_in[n] + y_st[n] * jnp.exp((pos[d] + 1.0) * lgh[n])
    vend = [(vh[n] * jnp.exp((lc - 1.0 - pos[items[n][0]]) * lgh[n])).astype(BF16) for n in n_it]
    cs = [_dot_tn(kh[n], vend[n]) for n in n_it]
    for n, (d, h) in enumerate(items):
        s_ref[d, h] = st[n] * jnp.exp(lc * lgh[n]) + cs[n]


def ret_scan(u_ret, cosx, sinx, ret_decay, ncx_tokens):
    b, ta, _ = u_ret.shape
    lc = RET_CHUNK
    nc = ta // lc
    ncx = ncx_tokens // lc
    fwd = lambda i: i
    bwd = lambda i: _chunk_of(1, i, ncx, nc)

    def specs(ch):
        return [pl.BlockSpec((1, lc, 2 * RET_QK), lambda bi, i: (bi, ch(i), 0)),
                pl.BlockSpec((1, lc, RET_INNER), lambda bi, i: (bi, ch(i), 1)),
                pl.BlockSpec((lc, 2 * RET_QK), lambda bi, i: (ch(i), 0)),
                pl.BlockSpec((lc, 2 * RET_QK), lambda bi, i: (ch(i), 0))]

    return pl.pallas_call(
        _ret_kernel,
        out_shape=(jax.ShapeDtypeStruct((b, ta, RET_INNER), F32), jax.ShapeDtypeStruct((b, ta, RET_INNER), F32)),
        grid=(b, nc),
        in_specs=specs(fwd) + specs(bwd) + [pl.BlockSpec((2, 1, LANES), lambda bi, i: (0, 0, 0))],
        out_specs=(pl.BlockSpec((1, lc, RET_INNER), lambda bi, i: (bi, i, 0)),
                   pl.BlockSpec((1, lc, RET_INNER), lambda bi, i: (bi, bwd(i), 0))),
        scratch_shapes=[pltpu.VMEM((2, RET_HEADS, RET_KEY_DIM, RET_VAL_DIM), F32)],
        compiler_params=_cparams(2),
        name="ret_scan",
    )(u_ret, u_ret, cosx, sinx, u_ret, u_ret, cosx, sinx, ret_decay)


RWKV_PAIRS = RWKV_HEADS // 2
RWKV_BLOCK = 2 * RWKV_CHUNK


def _bdiag(tile, bd2):
    return jnp.where(bd2, jnp.concatenate([tile, tile], axis=0), 0.0)


def _split2(x):
    hi = x.astype(BF16)
    return hi, (x - hi.astype(F32)).astype(BF16)


def _head_sums(xs, pair_ones):
    rows = xs[0].shape[0]
    parts = jnp.concatenate([p for x in xs for p in _split2(x)], axis=0)
    blk = parts.shape[0]
    n_pair = parts.shape[1] // LANES
    stacked = jnp.concatenate([parts[:, j * LANES:(j + 1) * LANES] for j in range(n_pair)], axis=0)
    y = jnp.dot(stacked, pair_ones, preferred_element_type=F32)
    full = jnp.concatenate([y[j * blk:(j + 1) * blk] for j in range(n_pair)], axis=1)
    return [full[2 * i * rows:(2 * i + 1) * rows] + full[(2 * i + 1) * rows:(2 * i + 2) * rows]
            for i in range(len(xs))]


def _pair_ones():
    hd = RWKV_HEAD_DIM
    return jnp.where((lax.broadcasted_iota(jnp.int32, (LANES, LANES), 0) < hd)
                     == (lax.broadcasted_iota(jnp.int32, (LANES, LANES), 1) < hd), 1.0, 0.0).astype(BF16)


def _dot3(a, b):
    a1, a2 = _split2(a)
    b1, b2 = _split2(b)
    n = a.shape[0]
    o = jnp.dot(jnp.concatenate([a1, a2], axis=0), b1, preferred_element_type=F32)
    return o[:n] + o[n:] + jnp.dot(a1, b2, preferred_element_type=F32)


def _pair_mm(lhs_splits, w_split, bd2):
    hi = jnp.concatenate([s[0] for s in lhs_splits], axis=0)
    lo = jnp.concatenate([s[1] for s in lhs_splits], axis=0)
    n = hi.shape[0]
    o = jnp.dot(jnp.concatenate([hi, lo], axis=0), _bdiag(w_split[0], bd2), preferred_element_type=F32)
    tot = o[:n] + o[n:] + jnp.dot(hi, _bdiag(w_split[1], bd2), preferred_element_type=F32)
    return [tot[i * RWKV_HEAD_DIM:(i + 1) * RWKV_HEAD_DIM] for i in range(len(lhs_splits))]


def _pair_rows(x_t, j, low_half):
    base = j * LANES
    return jnp.where(low_half, x_t[base:base + RWKV_HEAD_DIM], x_t[base + RWKV_HEAD_DIM:base + LANES])


def _rwkv_prep(d, u, prow, nrow, mix, w0, w2, a0, a2, kkw, ka, rk, bd):
    lc = RWKV_CHUNK
    lb = u.shape[0]
    ni = RWKV_INNER
    half_mix = 0.5 * mix
    nbr = pltpu.roll(u, 1, axis=0) + pltpu.roll(u, lb - 1, axis=0)
    sub = lax.broadcasted_iota(jnp.int32, (SUBLANES, 1), 0)
    fix_first = jnp.where(sub == 0, prow - u[lb - 1:lb, :], 0.0)
    fix_last = jnp.where(sub == SUBLANES - 1, nrow - u[0:1, :], 0.0)
    nbr = jnp.concatenate([nbr[:SUBLANES] + fix_first, nbr[SUBLANES:lb - SUBLANES],
                           nbr[lb - SUBLANES:] + fix_last], axis=0)
    u = (1.0 - mix) * u + half_mix * nbr
    r, k, v = u[:, 0:ni], u[:, ni:2 * ni], u[:, 2 * ni:3 * ni]
    o = 3 * ni
    w_lo = u[:, o:o + RWKV_DECAY_LORA]
    a_lo = u[:, o + RWKV_DECAY_LORA:o + RWKV_DECAY_LORA + RWKV_AAA_LORA]
    g_lo = u[:, o + RWKV_DECAY_LORA + RWKV_AAA_LORA:]

    logw = -math.exp(-0.5) * jax.nn.sigmoid(w0 + _dot3(jnp.tanh(w_lo), w2))
    a_gate = jax.nn.sigmoid(a0 + _dot3(a_lo, a2))
    kk = k * kkw
    kd = k * (1.0 + (a_gate - 1.0) * ka)
    kk_ss, rk_sum = _head_sums([kk * kk, r * kd * rk], bd)
    kk = kk / jnp.maximum(jnp.sqrt(kk_ss), 1e-12)
    bvec = kk * a_gate
    bonus = rk_sum * v

    row = lax.broadcasted_iota(jnp.int32, (lb, lb), 0)
    col = lax.broadcasted_iota(jnp.int32, (lb, lb), 1)
    lead = (row - col) if d == 0 else (col - row)
    incl = jnp.where(row // lc == col // lc, lead, -1) >= 0
    l1, l2, l3 = _split3(logw)
    cw3 = jnp.dot(incl.astype(BF16), jnp.concatenate([l1, l2, l3], axis=1), preferred_element_type=F32)
    cw = cw3[:, 0:ni] + cw3[:, ni:2 * ni] + cw3[:, 2 * ni:3 * ni]

    def dup_t(x):
        return jnp.concatenate([x, x], axis=0).T

    chunks = []
    for c in range(lb // lc):
        rs = slice(c * lc, (c + 1) * lc)
        logw_c, cw_c = logw[rs], cw[rs]
        ctot = jnp.sum(logw_c, axis=0, keepdims=True)
        e_neg = jnp.exp(-cw_c)
        e_end = jnp.exp(ctot - cw_c)
        chunks.append(dict(
            a_t=-kk[rs] * jnp.exp(cw_c - logw_c), r_t=r[rs] * jnp.exp(cw_c), v=v[rs],
            bn_t=dup_t(bvec[rs] * e_neg), kn_t=dup_t(kd[rs] * e_neg),
            be_t=dup_t(bvec[rs] * e_end), ke_t=dup_t(kd[rs] * e_end),
            wtot_col=jnp.exp(jnp.sum(logw_c.T, axis=1, keepdims=True))))
    return chunks, bonus, g_lo


def _rwkv_kernel(uf_ref, ufp_ref, ufn_ref, ub_ref, ubp_ref, ubn_ref, mix_ref, w0_ref, w2_ref, a0_ref, a2_ref,
                 g2_ref, kk_ref, ka_ref, rk_ref, of_ref, ob_ref, s_ref, *, ncx, nc):
    i = pl.program_id(1)
    lc = RWKV_CHUNK
    ni = RWKV_INNER
    hd = RWKV_HEAD_DIM
    chunk = (i, _chunk_of(1, i, ncx, nc))

    @pl.when(i == 0)
    def _():
        s_ref[...] = jnp.zeros_like(s_ref)

    bd2 = ((lax.broadcasted_iota(jnp.int32, (LANES, LANES), 0) < hd)
           == (lax.broadcasted_iota(jnp.int32, (LANES, LANES), 1) < hd))
    pair_ones = jnp.where(bd2, 1.0, 0.0).astype(BF16)
    u_refs = ((uf_ref, ufp_ref, ufn_ref), (ub_ref, ubp_ref, ubn_ref))
    out_refs = (of_ref, ob_ref)
    prep = []
    for d in (0, 1):
        seg_start, seg_end = _seg_edges(chunk[d], ncx, nc)
        u_ref, up_ref, un_ref = u_refs[d]
        prow = jnp.where(seg_start, 0.0, up_ref[0, SUBLANES - 1:SUBLANES, :])
        nrow = jnp.where(seg_end, 0.0, un_ref[0, 0:1, :])
        chunks, bonus, g_lo = _rwkv_prep(d, u_ref[0], prow, nrow, mix_ref[...], w0_ref[d], w2_ref[d], a0_ref[d],
                                         a2_ref[...], kk_ref[...], ka_ref[...], rk_ref[...], pair_ones)
        out_refs[d][0, :, ni:2 * ni] = bonus
        if d == 0:
            of_ref[0, :, 2 * ni:3 * ni] = jnp.dot(jax.nn.sigmoid(g_lo).astype(BF16), g2_ref[...],
                                                  preferred_element_type=F32)
        prep.append(chunks)

    row4 = lax.broadcasted_iota(jnp.int32, (lc, 2 * LANES), 0)
    col4 = lax.broadcasted_iota(jnp.int32, (lc, 2 * LANES), 1) % hd
    strict4 = (col4 < row4, col4 > row4)
    incl4 = (col4 <= row4, col4 >= row4)
    eye2 = (lax.broadcasted_iota(jnp.int32, (lc, LANES), 1) % hd
            == lax.broadcasted_iota(jnp.int32, (lc, LANES), 0)).astype(F32)
    low_half = lax.broadcasted_iota(jnp.int32, (hd, LANES), 1) < hd
    zeros_w = jnp.zeros((LANES, LANES), F32)

    def dot(a, b):
        return jnp.dot(a.astype(BF16), b.astype(BF16), preferred_element_type=F32)

    nch = RWKV_BLOCK // lc
    items = [(d, c, j) for d in (0, 1) for c in range(nch) for j in range(RWKV_PAIRS)]
    pl_ = {j: slice(j * LANES, (j + 1) * LANES) for j in range(RWKV_PAIRS)}
    n_it = range(len(items))
    a_t = [prep[d][c]["a_t"][:, pl_[j]] for d, c, j in items]
    r_t = [prep[d][c]["r_t"][:, pl_[j]] for d, c, j in items]
    v_w = [_bdiag(prep[d][c]["v"][:, pl_[j]], bd2) for d, c, j in items]
    w_p = [jnp.concatenate([jnp.where(bd2, prep[d][c]["bn_t"][pl_[j], :], 0.0),
                            jnp.where(bd2, prep[d][c]["kn_t"][pl_[j], :], 0.0)], axis=1) for d, c, j in items]
    p = [dot(jnp.concatenate([a_t[n], r_t[n]], axis=0), w_p[n]) for n in n_it]
    m_a = [jnp.where(strict4[items[n][0]], p[n][:lc], 0.0) for n in n_it]
    m_r = [jnp.where(incl4[items[n][0]], p[n][lc:], 0.0) for n in n_it]
    mv = [dot(m_a[n][:, LANES:], v_w[n]) for n in n_it]
    x = [m_a[n][:, :LANES] for n in n_it]
    t = [eye2 + x[n] for n in n_it]
    xs = [_split2(x[n]) for n in n_it]
    x = [_pair_mm([xs[n]], xs[n], bd2)[0] for n in n_it]
    for _ in range(int(math.log2(lc)) - 2):
        xs = [_split2(x[n]) for n in n_it]
        tx = [_pair_mm([_split2(t[n]), xs[n]], xs[n], bd2) for n in n_it]
        t = [t[n] + tx[n][0] for n in n_it]
        x = [tx[n][1] for n in n_it]
    t = [t[n] + _pair_mm([_split2(t[n])], _split2(x[n]), bd2)[0] for n in n_it]
    au = [dot(t[n], jnp.concatenate([_bdiag(a_t[n], bd2), _bdiag(mv[n], bd2)], axis=1)) for n in n_it]
    w2 = [jnp.concatenate([jnp.concatenate([_bdiag(au[n][:, :LANES], bd2), _bdiag(au[n][:, LANES:], bd2)], axis=1),
                           jnp.concatenate([zeros_w, v_w[n]], axis=1)], axis=0) for n in n_it]
    q_bk = [jnp.concatenate([_pair_rows(prep[d][c]["be_t"], j, low_half),
                             _pair_rows(prep[d][c]["ke_t"], j, low_half)], axis=1)
            for d, c, j in items]
    big = [dot(jnp.concatenate([q_bk[n], m_r[n]], axis=0), w2[n]) for n in n_it]
    wc = [jnp.where(low_half,
                    jnp.broadcast_to(prep[d][c]["wtot_col"][j * LANES:j * LANES + hd, :], (hd, LANES)),
                    jnp.broadcast_to(prep[d][c]["wtot_col"][j * LANES + hd:(j + 1) * LANES, :], (hd, LANES)))
          for d, c, j in items]

    idx = {it: n for n, it in enumerate(items)}
    heads = [(d, j) for d in (0, 1) for j in range(RWKV_PAIRS)]
    st = {dj: s_ref[dj[0], dj[1]] for dj in heads}
    for k in range(nch):
        sel = [(d, (k if d == 0 else nch - 1 - k), j) for d, j in heads]
        sy = [dot(jnp.concatenate([r_t[idx[it]] + big[idx[it]][lc:, :LANES], big[idx[it]][:lc, :LANES]], axis=0),
                  _bdiag(st[(it[0], it[2])], bd2)) for it in sel]
        for m, (d, c, j) in enumerate(sel):
            out_refs[d][0, c * lc:(c + 1) * lc, pl_[j]] = sy[m][:lc] + big[idx[(d, c, j)]][lc:, LANES:]
        for m, (d, c, j) in enumerate(sel):
            n = idx[(d, c, j)]
            st[(d, j)] = wc[n] * st[(d, j)] + sy[m][lc:] + big[n][:lc, LANES:]
    for d, j in heads:
        s_ref[d, j] = st[(d, j)]


def rwkv_scan(u_rwkv, mix, w0, w2, a0, a2, g2, k_k, k_a, r_k, ncx_tokens):
    b, ta, _ = u_rwkv.shape
    lc = RWKV_BLOCK
    nc = ta // lc
    ncx = ncx_tokens // lc
    nb = lc // SUBLANES
    nblk = ta // SUBLANES
    ni = RWKV_INNER
    kern = functools.partial(_rwkv_kernel, ncx=ncx, nc=nc)
    const2 = lambda bi, i: (0, 0)
    const3 = lambda bi, i: (0, 0, 0)
    fwd = lambda i: i
    bwd = lambda i: _chunk_of(1, i, ncx, nc)

    def u_specs(ch):
        return [pl.BlockSpec((1, lc, RWKV_IN), lambda bi, i: (bi, ch(i), 0)),
                pl.BlockSpec((1, SUBLANES, RWKV_IN), lambda bi, i: (bi, jnp.maximum(ch(i) * nb - 1, 0), 0)),
                pl.BlockSpec((1, SUBLANES, RWKV_IN), lambda bi, i: (bi, jnp.minimum((ch(i) + 1) * nb, nblk - 1), 0))]

    return pl.pallas_call(
        kern,
        out_shape=(jax.ShapeDtypeStruct((b, ta, 3 * ni), F32), jax.ShapeDtypeStruct((b, ta, 2 * ni), F32)),
        grid=(b, nc),
        in_specs=u_specs(fwd) + u_specs(bwd) + [
            pl.BlockSpec((1, RWKV_IN), const2),
            pl.BlockSpec((2, 1, ni), const3),
            pl.BlockSpec((2, RWKV_DECAY_LORA, ni), const3),
            pl.BlockSpec((2, 1, ni), const3),
            pl.BlockSpec((RWKV_AAA_LORA, ni), const2),
            pl.BlockSpec((RWKV_GATE_LORA, ni), const2),
            pl.BlockSpec((1, ni), const2),
            pl.BlockSpec((1, ni), const2),
            pl.BlockSpec((1, ni), const2),
        ],
        out_specs=(pl.BlockSpec((1, lc, 3 * ni), lambda bi, i: (bi, i, 0)),
                   pl.BlockSpec((1, lc, 2 * ni), lambda bi, i: (bi, bwd(i), 0))),
        scratch_shapes=[pltpu.VMEM((2, RWKV_PAIRS, RWKV_HEAD_DIM, LANES), F32)],
        compiler_params=_cparams(2),
        name="rwkv_scan",
    )(u_rwkv, u_rwkv, u_rwkv, u_rwkv, u_rwkv, u_rwkv, mix, w0, w2, a0, a2, g2, k_k, k_a, r_k)


def _merge_kernel(ssd_ref, z_ref, retf_ref, retb_ref, rg_ref, rwf_ref, rwb_ref, gate_ref, x_ref, mod_ref, nw_ref,
                  ssdnw_ref, lnw_ref, lnb_ref, wso_ref, wro_ref, wwo_ref, wo_ref, o_ref):
    ys = (ssd_ref[0, 0] + ssd_ref[1, 0]) * _silu(z_ref[0][:, :SSD_INNER])
    parts = []
    for g in range(SSD_GROUPS):
        yg = ys[:, g * SSD_GROUP_W:(g + 1) * SSD_GROUP_W]
        parts.append(yg * lax.rsqrt(jnp.mean(yg * yg, axis=-1, keepdims=True) + EPS))
    ys = jnp.concatenate(parts, axis=1) * ssdnw_ref[...]
    o_ssd = jnp.dot(ys.astype(BF16), wso_ref[...], preferred_element_type=F32)

    yr = retf_ref[0] + retb_ref[0]
    parts = []
    for h in range(RET_HEADS):
        yh = yr[:, h * RET_VAL_DIM:(h + 1) * RET_VAL_DIM]
        yc = yh - jnp.mean(yh, axis=-1, keepdims=True)
        parts.append(yc * lax.rsqrt(jnp.mean(yc * yc, axis=-1, keepdims=True) + EPS))
    yr = jnp.concatenate(parts, axis=1) * _silu(rg_ref[0])
    o_ret = jnp.dot(yr.astype(BF16), wro_ref[...], preferred_element_type=F32)

    ni = RWKV_INNER
    rw0 = rwf_ref[0]
    rw1 = rwb_ref[0]
    yw = rw0[:, :ni] + rw1[:, :ni]
    pair_ones = _pair_ones()
    inv_hd = 1.0 / RWKV_HEAD_DIM
    yc = yw - _head_sums([yw], pair_ones)[0] * inv_hd
    var = _head_sums([yc * yc], pair_ones)[0] * inv_hd
    yw = yc * lax.rsqrt(var + RWKV_LN_EPS) * lnw_ref[...] + lnb_ref[...]
    yw = (yw + rw0[:, ni:2 * ni] + rw1[:, ni:2 * ni]) * rw0[:, 2 * ni:3 * ni]
    o_rw = jnp.dot(yw.astype(BF16), wwo_ref[...], preferred_element_type=F32)

    gate = gate_ref[0]
    merged = (jax.nn.sigmoid(gate[:, :D_MODEL]) * o_ssd
              + jax.nn.sigmoid(gate[:, D_MODEL:2 * D_MODEL]) * o_ret
              + jax.nn.sigmoid(gate[:, 2 * D_MODEL:]) * o_rw)
    yx = jnp.dot(merged.astype(BF16), wo_ref[...], preferred_element_type=F32)
    g1 = mod_ref[0, 0, 2:3, :]
    o_ref[0] = x_ref[0] + g1 * _rms(yx, nw_ref[...])


def merge_out(ssd_y, u_zdt, ret_f, ret_b, u_ret, rw_f, rw_b, u_gate, x, mod, nw1, ssd_nw, ln_w, ln_b,
              w_ssd_out, w_ret_out, w_rwkv_out, w_out, n_ctx_tiles):
    b, ta, dm = x.shape
    tm = ROW_TILE
    row = lambda bi, i: (bi, i, 0)
    both = lambda bi, i: (0, bi, i, 0)
    const2 = lambda bi, i: (0, 0)
    return pl.pallas_call(
        _merge_kernel,
        out_shape=jax.ShapeDtypeStruct((b, ta, dm), F32),
        grid=(b, ta // tm),
        in_specs=[
            pl.BlockSpec((2, 1, tm, SSD_INNER), both),
            pl.BlockSpec((1, tm, u_zdt.shape[2]), row),
            pl.BlockSpec((1, tm, RET_INNER), row),
            pl.BlockSpec((1, tm, RET_INNER), row),
            pl.BlockSpec((1, tm, RET_INNER), lambda bi, i: (bi, i, 2)),
            pl.BlockSpec((1, tm, 3 * RWKV_INNER), row),
            pl.BlockSpec((1, tm, 2 * RWKV_INNER), row),
            pl.BlockSpec((1, tm, N_BRANCH * dm), row),
            pl.BlockSpec((1, tm, dm), row),
            pl.BlockSpec((1, 1, SUBLANES, dm), lambda bi, i: (bi, (i >= n_ctx_tiles).astype(jnp.int32), 0, 0)),
            pl.BlockSpec((1, dm), const2),
            pl.BlockSpec((1, SSD_INNER), const2),
            pl.BlockSpec((1, RWKV_INNER), const2),
            pl.BlockSpec((1, RWKV_INNER), const2),
            pl.BlockSpec((SSD_INNER, dm), const2),
            pl.BlockSpec((RET_INNER, dm), const2),
            pl.BlockSpec((RWKV_INNER, dm), const2),
            pl.BlockSpec((dm, dm), const2),
        ],
        out_specs=pl.BlockSpec((1, tm, dm), row),
        compiler_params=_cparams(2),
        name="merge_out",
    )(ssd_y, u_zdt, ret_f, ret_b, u_ret, rw_f, rw_b, u_gate, x, mod, nw1, ssd_nw, ln_w, ln_b,
      w_ssd_out, w_ret_out, w_rwkv_out, w_out)


def _mlp_kernel(x_ref, mod_ref, nw2_ref, nw3_ref, w1_ref, w2_ref, *rest, ff_tile, with_next):
    nb, tm, dm = x_ref.shape
    x = x_ref[...]
    mod = mod_ref[:, 0]
    y = _rms(x, nw2_ref[...])
    h = (y * (1.0 + mod[:, 4:5, :]) + mod[:, 3:4, :]).astype(BF16).reshape(nb * tm, dm)
    acc = jnp.zeros((nb * tm, dm), F32)
    for j in range(D_FF // ff_tile):
        hid = jnp.dot(h, w1_ref[:, j * ff_tile:(j + 1) * ff_tile], preferred_element_type=F32)
        hid = jnp.square(jnp.maximum(hid, 0.0)).astype(BF16)
        acc = acc + jnp.dot(hid, w2_ref[j * ff_tile:(j + 1) * ff_tile, :], preferred_element_type=F32)
    x_new = x + mod[:, 5:6, :] * _rms(acc.reshape(nb, tm, dm), nw3_ref[...])
    if with_next:
        nwn_ref, modn_ref, o_ref, h_ref = rest
        modn = modn_ref[:, 0]
        h_ref[...] = (_rms(x_new, nwn_ref[...]) * (1.0 + modn[:, 1:2, :]) + modn[:, 0:1, :]).astype(BF16)
    else:
        o_ref, = rest
    o_ref[...] = x_new


def mlp_block(x, mod, nw2, nw3, w1, w2, n_ctx_tiles, skip_tiles=0, next_norm=None, ff_tile=1024):
    b, ta, dm = x.shape
    tm = ROW_TILE
    rows = lambda i: (0, i, 0)
    const2 = lambda i: (0, 0)
    mod_spec = pl.BlockSpec((b, 1, SUBLANES, dm),
                            lambda i: (0, (i + skip_tiles >= n_ctx_tiles).astype(jnp.int32), 0, 0))
    resident = pl.Buffered(1)
    in_specs = [
        pl.BlockSpec((b, tm, dm), lambda i: (0, i + skip_tiles, 0)),
        mod_spec,
        pl.BlockSpec((1, dm), const2),
        pl.BlockSpec((1, dm), const2),
        pl.BlockSpec((dm, D_FF), const2, pipeline_mode=resident),
        pl.BlockSpec((D_FF, dm), const2, pipeline_mode=resident),
    ]
    args = [x, mod, nw2, nw3, w1, w2]
    out_rows = ta - skip_tiles * tm
    out_shape = jax.ShapeDtypeStruct((b, out_rows, dm), F32)
    out_specs = pl.BlockSpec((b, tm, dm), rows)
    if next_norm is not None:
        in_specs += [pl.BlockSpec((1, dm), const2), mod_spec]
        args += list(next_norm)
        out_shape = (out_shape, jax.ShapeDtypeStruct((b, out_rows, dm), BF16))
        out_specs = (out_specs, pl.BlockSpec((b, tm, dm), rows))
    return pl.pallas_call(
        functools.partial(_mlp_kernel, ff_tile=ff_tile, with_next=next_norm is not None),
        out_shape=out_shape,
        grid=(ta // tm - skip_tiles,),
        in_specs=in_specs,
        out_specs=out_specs,
        compiler_params=_cparams(1),
        name="mlp",
    )(*args)


def _rope_tables(n_ctx, n_lat):
    rows = n_lat // GRID_W
    row = np.repeat(np.arange(rows), GRID_W).astype(np.float32)
    col = np.tile(np.arange(GRID_W), rows).astype(np.float32)
    n_freq = RET_KEY_DIM // 4
    inv = jnp.power(ROPE_BASE, -jnp.arange(n_freq, dtype=F32) / n_freq)
    ang = jnp.concatenate([jnp.asarray(row)[:, None] * inv, jnp.asarray(col)[:, None] * inv], axis=-1)
    cos = jnp.concatenate([jnp.ones((n_ctx, RET_KEY_DIM // 2), F32), jnp.cos(ang)], axis=0)
    sin = jnp.concatenate([jnp.zeros((n_ctx, RET_KEY_DIM // 2), F32), jnp.sin(ang)], axis=0)
    cos_h = jnp.concatenate([cos, cos], axis=1)
    sin_h = jnp.concatenate([-sin, sin], axis=1)
    reps = 2 * RET_HEADS
    return jnp.tile(cos_h, (1, reps)), jnp.tile(sin_h, (1, reps))


def _pad_lanes(a, width=LANES):
    return jnp.pad(a, [(0, 0)] * (a.ndim - 1) + [(0, width - a.shape[-1])])


def kernel(x, c, ctx, c_ctx, norm_w, ada_w, ada_b, w_in, ssd_conv_w, ssd_conv_b, ssd_dt_bias, ssd_a_log,
           ssd_d, ssd_norm_w, ret_decay, rwkv_mix, rwkv_w0, rwkv_w2, rwkv_a0, rwkv_a2, rwkv_g2, rwkv_k_k,
           rwkv_k_a, rwkv_r_k, rwkv_lnx_w, rwkv_lnx_b, w_ssd_out, w_ret_out, w_rwkv_out, w_out, mlp_w1, mlp_w2):
    b, n_lat, dm = x.shape
    n_ctx = ctx.shape[1]
    ta = n_ctx + n_lat
    depth = norm_w.shape[0]
    n_ctx_tiles = n_ctx // ROW_TILE
    assert n_ctx % ROW_TILE == 0 and n_lat % ROW_TILE == 0 and b + 1 <= SUBLANES

    hexp = np.zeros((LANES, SSD_INNER), np.float32)
    for h in range(SSD_HEADS):
        hexp[h, h * SSD_HEAD_DIM:(h + 1) * SSD_HEAD_DIM] = 1.0
    hexp = jnp.asarray(hexp, BF16)
    cosx, sinx = _rope_tables(n_ctx, n_lat)

    cond = jnp.zeros((SUBLANES, dm), F32).at[:b].set(c).at[b].set(c_ctx)
    mod_all = modulation_all(cond, ada_w, ada_b)

    sizes = (N_BRANCH * dm, SSD_INNER, SSD_CONV_DIM, 2 * SSD_HEADS, RET_QK, RET_QK, RET_INNER, RET_INNER, RWKV_IN)
    offs = np.concatenate([[0], np.cumsum(sizes)])
    perm = np.concatenate([np.concatenate([np.arange(0, RET_KEY_DIM, 2), np.arange(1, RET_KEY_DIM, 2)]) + h * RET_KEY_DIM
                           for h in range(RET_HEADS)])

    xall = jnp.concatenate([ctx, x], axis=1)
    mods = []
    for l in range(depth):
        m = mod_all[l].reshape(SUBLANES, 6, dm)
        m = jnp.pad(m, ((0, 0), (0, SUBLANES - 6), (0, 0)))
        mods.append(jnp.stack([jnp.broadcast_to(m[b], (b,) + m.shape[1:]), m[:b]], axis=1))
    h = norm_modulate(xall, norm_w[0][0:1], mods[0], n_ctx_tiles)
    tm_mm = 1024 if (b * ta) % 1024 == 0 else ROW_TILE
    for l in range(depth):
        wl = w_in[l]
        seg = [wl[:, offs[j]:offs[j + 1]] for j in range(len(sizes))]
        w_gate = seg[0].astype(BF16)
        w_zdt = jnp.concatenate([seg[1], _pad_lanes(seg[3][:, :SSD_HEADS]), _pad_lanes(seg[3][:, SSD_HEADS:])],
                                axis=1).astype(BF16)
        w_xbc = seg[2].astype(BF16)
        w_ret = jnp.concatenate([seg[4][:, perm], seg[5][:, perm], seg[6], seg[7]], axis=1).astype(BF16)
        w_rw = seg[8].astype(BF16)

        mod = mods[l]
        nw = norm_w[l]
        h = h.reshape(b * ta, dm)
        u_gate = matmul(h, w_gate, tm_mm, 1024).reshape(b, ta, -1)
        u_zdt = matmul(h, w_zdt, tm_mm, w_zdt.shape[1]).reshape(b, ta, -1)
        u_xbc = matmul(h, w_xbc, tm_mm, SSD_CONV_DIM).reshape(b, ta, -1)
        u_ret = matmul(h, w_ret, tm_mm, w_ret.shape[1]).reshape(b, ta, -1)
        u_rw = matmul(h, w_rw, tm_mm, RWKV_IN).reshape(b, ta, -1)

        conv_w = jnp.pad(ssd_conv_w[l], ((0, SUBLANES - SSD_CONV), (0, 0)))
        dskip = jnp.repeat(ssd_d[l], SSD_HEAD_DIM)[None, :]
        ssd_y = ssd_scan(u_xbc, u_zdt, conv_w, ssd_conv_b[l][None, :],
                         _pad_lanes(ssd_dt_bias[l])[:, None, :], _pad_lanes(ssd_a_log[l])[:, None, :],
                         dskip, hexp, n_ctx)
        ret_f, ret_b = ret_scan(u_ret, cosx, sinx, _pad_lanes(ret_decay[l])[:, None, :], n_ctx)
        rw_f, rw_b = rwkv_scan(u_rw, rwkv_mix[l][None, :], rwkv_w0[l][:, None, :], rwkv_w2[l], rwkv_a0[l][:, None, :],
                         rwkv_a2[l], rwkv_g2[l].astype(BF16), rwkv_k_k[l][None, :], rwkv_k_a[l][None, :],
                         rwkv_r_k[l].reshape(1, RWKV_INNER), n_ctx)

        xall = merge_out(ssd_y, u_zdt, ret_f, ret_b, u_ret, rw_f, rw_b, u_gate, xall, mod, nw[1:2],
                         ssd_norm_w[l][None, :], rwkv_lnx_w[l][None, :], rwkv_lnx_b[l][None, :],
                         w_ssd_out[l].astype(BF16), w_ret_out[l].astype(BF16), w_rwkv_out[l].astype(BF16),
                         w_out[l].astype(BF16), n_ctx_tiles)
        w1, w2 = mlp_w1[l].astype(BF16), mlp_w2[l].astype(BF16)
        if l + 1 < depth:
            xall, h = mlp_block(xall, mod, nw[2:3], nw[3:4], w1, w2, n_ctx_tiles,
                                next_norm=(norm_w[l + 1][0:1], mods[l + 1]))
        else:
            xall = mlp_block(xall, mod, nw[2:3], nw[3:4], w1, w2, n_ctx_tiles, skip_tiles=n_ctx_tiles)
    return xall
```

```python
import functools
import math

import numpy as np
import jax
import jax.numpy as jnp
from jax import lax
from jax.experimental import pallas as pl
from jax.experimental.pallas import tpu as pltpu

F32 = jnp.float32
BF16 = jnp.bfloat16
HIGHEST = lax.Precision.HIGHEST

D_MODEL = 1024
DEPTH = 4
GRID_W = 64
EPS = 1e-6
N_BRANCH = 3

SSD_HEADS = 16
SSD_HEAD_DIM = 64
SSD_INNER = SSD_HEADS * SSD_HEAD_DIM
SSD_GROUPS = 2
SSD_STATE = 128
SSD_CONV = 5
SSD_CHUNK = 128
SSD_BLOCK = 2 * SSD_CHUNK
SSD_CONV_DIM = SSD_INNER + 2 * SSD_GROUPS * SSD_STATE
SSD_GROUP_W = SSD_INNER // SSD_GROUPS

RET_HEADS = 4
RET_KEY_DIM = 64
RET_VAL_DIM = 128
RET_QK = RET_HEADS * RET_KEY_DIM
RET_INNER = RET_HEADS * RET_VAL_DIM
RET_CHUNK = 128
ROPE_BASE = 10000.0

RWKV_HEADS = 8
RWKV_HEAD_DIM = 64
RWKV_INNER = RWKV_HEADS * RWKV_HEAD_DIM
RWKV_DECAY_LORA = 64
RWKV_AAA_LORA = 64
RWKV_GATE_LORA = 128
RWKV_IN = 3 * RWKV_INNER + RWKV_DECAY_LORA + RWKV_AAA_LORA + RWKV_GATE_LORA
RWKV_LN_EPS = 64e-5
RWKV_CHUNK = 64

D_FF = 4 * D_MODEL

SUBLANES = 8
LANES = 128
VMEM_LIMIT_BYTES = 56 * 1024 * 1024

ROW_TILE = 256
NEG_BIG = -1e30


def _cparams(n_axes):
    return pltpu.CompilerParams(dimension_semantics=("arbitrary",) * n_axes,
                                vmem_limit_bytes=VMEM_LIMIT_BYTES)


def _silu(x):
    return x * jax.nn.sigmoid(x)


def _softplus(x):
    return jnp.maximum(x, 0.0) + jnp.log1p(jnp.exp(-jnp.abs(x)))


def _split3(x):
    x1 = x.astype(BF16)
    r1 = x - x1.astype(F32)
    x2 = r1.astype(BF16)
    r2 = r1 - x2.astype(F32)
    return x1, x2, r2.astype(BF16)


def _dot_sel_rhs(x, sel):
    x1, x2, x3 = _split3(x)
    return (jnp.dot(x1, sel, preferred_element_type=F32)
            + jnp.dot(x2, sel, preferred_element_type=F32)
            + jnp.dot(x3, sel, preferred_element_type=F32))


def _dot_sel_lhs(sel, x):
    x1, x2, x3 = _split3(x)
    return (jnp.dot(sel, x1, preferred_element_type=F32)
            + jnp.dot(sel, x2, preferred_element_type=F32)
            + jnp.dot(sel, x3, preferred_element_type=F32))


def _dot_nt(a, b, precision=None):
    return lax.dot_general(a, b, (((1,), (1,)), ((), ())), preferred_element_type=F32, precision=precision)


def _dot_tn(a, b, precision=None):
    return lax.dot_general(a, b, (((0,), (0,)), ((), ())), preferred_element_type=F32, precision=precision)


def _chunk_of(d, i, ncx, nc):
    bwd = jnp.where(i < ncx, ncx - 1 - i, ncx + nc - 1 - i)
    return jnp.where(d == 0, i, bwd)


def _seg_edges(c, ncx, nc):
    seg_start = jnp.logical_or(c == 0, c == ncx)
    seg_end = jnp.logical_or(c == ncx - 1, c == nc - 1)
    return seg_start, seg_end


def _tri_mask(n, d, strict, reps=1):
    row = lax.broadcasted_iota(jnp.int32, (n, n * reps), 0)
    col = lax.broadcasted_iota(jnp.int32, (n, n * reps), 1) % n
    lead = jnp.where(d == 0, row - col, col - row)
    return lead > 0 if strict else lead >= 0


def _mod_kernel(c_ref, w_ref, b_ref, o_ref):
    s = _silu(c_ref[...])
    o_ref[0] = jnp.dot(s, w_ref[0], preferred_element_type=F32, precision=HIGHEST) + b_ref[0]


def modulation_all(cond, ada_w, ada_b):
    depth = ada_w.shape[0]
    tn = 1536
    return pl.pallas_call(
        _mod_kernel,
        out_shape=jax.ShapeDtypeStruct((depth, SUBLANES, 6 * D_MODEL), F32),
        grid=(depth, 6 * D_MODEL // tn),
        in_specs=[pl.BlockSpec((SUBLANES, D_MODEL), lambda l, j: (0, 0)),
                  pl.BlockSpec((1, D_MODEL, tn), lambda l, j: (l, 0, j)),
                  pl.BlockSpec((1, 1, tn), lambda l, j: (l, 0, j))],
        out_specs=pl.BlockSpec((1, SUBLANES, tn), lambda l, j: (l, 0, j)),
        compiler_params=_cparams(2),
        name="modulation",
    )(cond, ada_w, ada_b.reshape(depth, 1, 6 * D_MODEL))


def _rms(x, w):
    return x * lax.rsqrt(jnp.mean(x * x, axis=-1, keepdims=True) + EPS) * w


def _normmod_kernel(x_ref, nw_ref, mod_ref, h_ref):
    y = _rms(x_ref[0], nw_ref[...])
    shift = mod_ref[0, 0, 0:1, :]
    scale = mod_ref[0, 0, 1:2, :]
    h_ref[0] = (y * (1.0 + scale) + shift).astype(BF16)


def norm_modulate(x, nw, mod, n_ctx_tiles):
    b, ta, dm = x.shape
    return pl.pallas_call(
        _normmod_kernel,
        out_shape=jax.ShapeDtypeStruct((b, ta, dm), BF16),
        grid=(b, ta // ROW_TILE),
        in_specs=[pl.BlockSpec((1, ROW_TILE, dm), lambda bi, i: (bi, i, 0)),
                  pl.BlockSpec((1, dm), lambda bi, i: (0, 0)),
                  pl.BlockSpec((1, 1, SUBLANES, dm), lambda bi, i: (bi, (i >= n_ctx_tiles).astype(jnp.int32), 0, 0))],
        out_specs=pl.BlockSpec((1, ROW_TILE, dm), lambda bi, i: (bi, i, 0)),
        compiler_params=_cparams(2),
        name="norm_modulate",
    )(x, nw, mod)


def _mm_kernel(a_ref, w_ref, o_ref):
    o_ref[...] = jnp.dot(a_ref[...], w_ref[...], preferred_element_type=F32).astype(o_ref.dtype)


def matmul(a, w, tm, tn, out_dtype=F32):
    r, k = a.shape
    n = w.shape[1]
    assert r % tm == 0 and n % tn == 0
    return pl.pallas_call(
        _mm_kernel,
        out_shape=jax.ShapeDtypeStruct((r, n), out_dtype),
        grid=(n // tn, r // tm),
        in_specs=[pl.BlockSpec((tm, k), lambda j, i: (i, 0)),
                  pl.BlockSpec((k, tn), lambda j, i: (0, j))],
        out_specs=pl.BlockSpec((tm, tn), lambda j, i: (i, j)),
        compiler_params=_cparams(2),
        name="in_proj",
    )(a, w)


def _ssd_kernel(xbc_ref, xp_ref, xn_ref, dt_ref, cw_ref, cb_ref, dtb_ref, alog_ref, dskip_ref,
                hexp_ref, y_ref, s_ref, ext_ref, act_ref, *, ncx, nc):
    d = pl.program_id(1)
    i = pl.program_id(2)
    c = _chunk_of(d, i, ncx, nc)
    seg_start, seg_end = _seg_edges(c, ncx, nc)
    lc = SSD_CHUNK
    lb = SSD_BLOCK
    nch = lb // lc
    hb = SUBLANES

    @pl.when(i == 0)
    def _():
        s_ref[...] = jnp.zeros_like(s_ref)

    @pl.when(d == 0)
    def _():
        ext_ref[0:hb, :] = jnp.where(seg_start, 0.0, xp_ref[0])
        ext_ref[hb:hb + lb, :] = xbc_ref[0]
        ext_ref[hb + lb:hb + lb + hb, :] = jnp.where(seg_end, 0.0, xn_ref[0])
        pad = SSD_CONV // 2
        acc = ext_ref[hb - pad:hb - pad + lb, :] * cw_ref[0:1, :]
        for j in range(1, SSD_CONV):
            acc = acc + ext_ref[hb - pad + j:hb - pad + j + lb, :] * cw_ref[j:j + 1, :]
        act = _silu(acc + cb_ref[...])
        act_ref[c] = act.astype(BF16)
        y_ref[0, 0] = act[:, :SSD_INNER] * dskip_ref[...]

    @pl.when(d != 0)
    def _():
        y_ref[0, 0] = jnp.zeros(y_ref.shape[2:], F32)

    tri = _tri_mask(lc, d, strict=False)
    hg = SSD_HEADS // SSD_GROUPS
    low_half = lax.broadcasted_iota(jnp.int32, (lc, 2 * SSD_HEAD_DIM), 1) < SSD_HEAD_DIM
    neg_a = -jnp.exp(alog_ref[0])

    local = []
    for k in range(nch):
        rs = pl.ds(pl.multiple_of(jnp.where(d == 0, k, nch - 1 - k) * lc, lc), lc)
        xbc = act_ref[c, rs, :]
        bm = xbc[:, SSD_INNER:SSD_INNER + SSD_GROUPS * SSD_STATE]
        cm = xbc[:, SSD_INNER + SSD_GROUPS * SSD_STATE:]
        dt = _softplus(dt_ref[0, rs, :] + dtb_ref[0])
        la = dt * neg_a
        ac3 = jnp.dot(tri.astype(BF16), jnp.concatenate(_split3(la), axis=1), preferred_element_type=F32)
        acum = ac3[:, :LANES] + ac3[:, LANES:2 * LANES] + ac3[:, 2 * LANES:]
        atot = jnp.sum(la, axis=0, keepdims=True)
        acum_t = acum.T
        etot8 = jnp.broadcast_to(jnp.exp(atot), (SUBLANES, LANES))
        parts = jnp.concatenate(_split2(jnp.exp(acum)) + _split2(etot8), axis=0)
        ex = jnp.dot(parts, hexp_ref[...], preferred_element_type=F32)
        eacx = ex[0:lc] + ex[lc:2 * lc]
        etotx = (ex[2 * lc:2 * lc + SUBLANES] + ex[2 * lc + SUBLANES:])[0:1, :]
        parts_b = jnp.concatenate([dt.astype(BF16), jnp.exp(atot - acum).astype(BF16)], axis=0)
        ex_b = jnp.dot(parts_b, hexp_ref[...], preferred_element_type=F32).astype(BF16)
        vb = xbc[:, :SSD_INNER] * ex_b[0:lc]
        vend = vb * ex_b[lc:2 * lc]
        groups = []
        for g in range(SSD_GROUPS):
            gs = slice(g * SSD_STATE, (g + 1) * SSD_STATE)
            gw = slice(g * SSD_GROUP_W, (g + 1) * SSD_GROUP_W)
            cg = cm[:, gs]
            bg = bm[:, gs]
            scores = _dot_nt(cg, bg)
            y_in = []
            for hp in range(hg // 2):
                a_pair = []
                for h in (g * hg + 2 * hp, g * hg + 2 * hp + 1):
                    diff = acum[:, h:h + 1] - acum_t[h:h + 1, :]
                    a_pair.append((scores * jnp.exp(jnp.where(tri, diff, NEG_BIG))).astype(BF16))
                ps = slice((g * hg + 2 * hp) * SSD_HEAD_DIM, (g * hg + 2 * hp + 2) * SSD_HEAD_DIM)
                v_pair = vb[:, ps]
                w_pair = jnp.concatenate([jnp.where(low_half, v_pair, 0.0), jnp.where(low_half, 0.0, v_pair)],
                                         axis=0)
                y_in.append(jnp.dot(jnp.concatenate(a_pair, axis=1), w_pair, preferred_element_type=F32))
            groups.append(dict(cg=cg, y_in=jnp.concatenate(y_in, axis=1), eac=eacx[:, gw], etot=etotx[:, gw],
                               cs=_dot_tn(bg, vend[:, gw])))
        local.append((rs, groups))

    for g in range(SSD_GROUPS):
        gw = slice(g * SSD_GROUP_W, (g + 1) * SSD_GROUP_W)
        st = s_ref[g]
        for rs, groups in local:
            q = groups[g]
            y_ref[0, 0, rs, gw] += q["y_in"] + jnp.dot(q["cg"], st.astype(BF16),
                                                       preferred_element_type=F32) * q["eac"]
            st = st * q["etot"] + q["cs"]
        s_ref[g] = st


def ssd_scan(u_xbc, u_zdt, conv_w, conv_b, dt_bias, a_log, dskip, hexp, ncx_tokens):
    b, ta, _ = u_xbc.shape
    lc = SSD_BLOCK
    nc = ta // lc
    ncx = ncx_tokens // lc
    nb = lc // SUBLANES
    nblk = ta // SUBLANES
    ch = functools.partial(_chunk_of, ncx=ncx, nc=nc)
    kern = functools.partial(_ssd_kernel, ncx=ncx, nc=nc)
    zdt_blk0 = SSD_INNER // LANES
    return pl.pallas_call(
        kern,
        out_shape=jax.ShapeDtypeStruct((2, b, ta, SSD_INNER), F32),
        grid=(b, 2, nc),
        in_specs=[
            pl.BlockSpec((1, lc, SSD_CONV_DIM), lambda bi, d, i: (bi, i * (1 - d), 0)),
            pl.BlockSpec((1, SUBLANES, SSD_CONV_DIM), lambda bi, d, i: (bi, jnp.maximum(i * nb - 1, 0) * (1 - d), 0)),
            pl.BlockSpec((1, SUBLANES, SSD_CONV_DIM),
                         lambda bi, d, i: (bi, jnp.minimum((i + 1) * nb, nblk - 1) * (1 - d), 0)),
            pl.BlockSpec((1, lc, LANES), lambda bi, d, i: (bi, ch(d, i), zdt_blk0 + d)),
            pl.BlockSpec((SUBLANES, SSD_CONV_DIM), lambda bi, d, i: (0, 0)),
            pl.BlockSpec((1, SSD_CONV_DIM), lambda bi, d, i: (0, 0)),
            pl.BlockSpec((1, 1, LANES), lambda bi, d, i: (d, 0, 0)),
            pl.BlockSpec((1, 1, LANES), lambda bi, d, i: (d, 0, 0)),
            pl.BlockSpec((1, SSD_INNER), lambda bi, d, i: (0, 0)),
            pl.BlockSpec((LANES, SSD_INNER), lambda bi, d, i: (0, 0)),
        ],
        out_specs=pl.BlockSpec((1, 1, lc, SSD_INNER), lambda bi, d, i: (d, bi, ch(d, i), 0)),
        scratch_shapes=[pltpu.VMEM((SSD_GROUPS, SSD_STATE, SSD_GROUP_W), F32),
                        pltpu.VMEM((lc + 2 * SUBLANES, SSD_CONV_DIM), F32),
                        pltpu.VMEM((nc, lc, SSD_CONV_DIM), BF16)],
        compiler_params=_cparams(3),
        name="ssd_scan",
    )(u_xbc, u_xbc, u_xbc, u_zdt, conv_w, conv_b, dt_bias, a_log, dskip, hexp)


def _ret_kernel(qkf_ref, vf_ref, cosf_ref, sinf_ref, qkb_ref, vb_ref, cosb_ref, sinb_ref, dec_ref,
                yf_ref, yb_ref, s_ref):
    i = pl.program_id(1)
    lc = RET_CHUNK

    @pl.when(i == 0)
    def _():
        s_ref[...] = jnp.zeros_like(s_ref)

    in_refs = ((qkf_ref, vf_ref, cosf_ref, sinf_ref), (qkb_ref, vb_ref, cosb_ref, sinb_ref))
    out_refs = (yf_ref, yb_ref)
    half = RET_KEY_DIM // 2
    width = 2 * RET_QK
    lane = lax.broadcasted_iota(jnp.int32, (lc, width), 1)
    row = lax.broadcasted_iota(jnp.int32, (lc, lc), 0)
    col = lax.broadcasted_iota(jnp.int32, (lc, lc), 1)
    dist = jnp.abs(row - col).astype(F32)
    tri = (col <= row, col >= row)
    t_idx = lax.broadcasted_iota(jnp.int32, (lc, 1), 0).astype(F32)
    pos = (t_idx, lc - 1.0 - t_idx)
    q, k, v, lg = [], [], [], []
    for d in (0, 1):
        qk_ref, v_ref, cos_ref, sin_ref = in_refs[d]
        qk = qk_ref[0]
        swapped = jnp.where((lane % RET_KEY_DIM) < half,
                            pltpu.roll(qk, width - half, axis=1), pltpu.roll(qk, half, axis=1))
        qk = qk * cos_ref[...] + swapped * sin_ref[...]
        q.append(qk[:, :RET_QK].astype(BF16))
        k.append((qk[:, RET_QK:] * (RET_KEY_DIM ** -0.5)).astype(BF16))
        v.append(v_ref[0])
        lg.append(-_softplus(-dec_ref[d]))

    items = [(d, h) for d in (0, 1) for h in range(RET_HEADS)]
    n_it = range(len(items))
    ks = [slice(h * RET_KEY_DIM, (h + 1) * RET_KEY_DIM) for d, h in items]
    vs = [slice(h * RET_VAL_DIM, (h + 1) * RET_VAL_DIM) for d, h in items]
    lgh = [lg[d][:, h:h + 1] for d, h in items]
    qh = [q[d][:, ks[n]] for n, (d, h) in enumerate(items)]
    kh = [k[d][:, ks[n]] for n, (d, h) in enumerate(items)]
    vh = [v[d][:, vs[n]] for n, (d, h) in enumerate(items)]
    scores = [(_dot_nt(qh[n], kh[n]) * jnp.exp(jnp.where(tri[items[n][0]], dist * lgh[n], NEG_BIG))).astype(BF16)
              for n in n_it]
    st = [s_ref[d, h] for d, h in items]
    y_in = [jnp.dot(scores[n], vh[n].astype(BF16), preferred_element_type=F32) for n in n_it]
    y_st = [jnp.dot(qh[n], st[n].astype(BF16), preferred_element_type=F32) for n in n_it]
    for n, (d, h) in enumerate(items):
        out_refs[d][0, :, vs[n]] = y_in[n] + y_st[n] * jnp.exp((pos[d] + 1.0) * lgh[n])
    vend = [(vh[n] * jnp.exp((lc - 1.0 - pos[items[n][0]]) * lgh[n])).astype(BF16) for n in n_it]
    cs = [_dot_tn(kh[n], vend[n]) for n in n_it]
    for n, (d, h) in enumerate(items):
        s_ref[d, h] = st[n] * jnp.exp(lc * lgh[n]) + cs[n]


def ret_scan(u_ret, cosx, sinx, ret_decay, ncx_tokens):
    b, ta, _ = u_ret.shape
    lc = RET_CHUNK
    nc = ta // lc
    ncx = ncx_tokens // lc
    fwd = lambda i: i
    bwd = lambda i: _chunk_of(1, i, ncx, nc)

    def specs(ch):
        return [pl.BlockSpec((1, lc, 2 * RET_QK), lambda bi, i: (bi, ch(i), 0)),
                pl.BlockSpec((1, lc, RET_INNER), lambda bi, i: (bi, ch(i), 1)),
                pl.BlockSpec((lc, 2 * RET_QK), lambda bi, i: (ch(i), 0)),
                pl.BlockSpec((lc, 2 * RET_QK), lambda bi, i: (ch(i), 0))]

    return pl.pallas_call(
        _ret_kernel,
        out_shape=(jax.ShapeDtypeStruct((b, ta, RET_INNER), F32), jax.ShapeDtypeStruct((b, ta, RET_INNER), F32)),
        grid=(b, nc),
        in_specs=specs(fwd) + specs(bwd) + [pl.BlockSpec((2, 1, LANES), lambda bi, i: (0, 0, 0))],
        out_specs=(pl.BlockSpec((1, lc, RET_INNER), lambda bi, i: (bi, i, 0)),
                   pl.BlockSpec((1, lc, RET_INNER), lambda bi, i: (bi, bwd(i), 0))),
        scratch_shapes=[pltpu.VMEM((2, RET_HEADS, RET_KEY_DIM, RET_VAL_DIM), F32)],
        compiler_params=_cparams(2),
        name="ret_scan",
    )(u_ret, u_ret, cosx, sinx, u_ret, u_ret, cosx, sinx, ret_decay)


RWKV_PAIRS = RWKV_HEADS // 2
RWKV_BLOCK = 2 * RWKV_CHUNK


def _bdiag(tile, bd2):
    return jnp.where(bd2, jnp.concatenate([tile, tile], axis=0), 0.0)


def _split2(x):
    hi = x.astype(BF16)
    return hi, (x - hi.astype(F32)).astype(BF16)


def _head_sums(xs, pair_ones):
    rows = xs[0].shape[0]
    parts = jnp.concatenate([p for x in xs for p in _split2(x)], axis=0)
    blk = parts.shape[0]
    n_pair = parts.shape[1] // LANES
    stacked = jnp.concatenate([parts[:, j * LANES:(j + 1) * LANES] for j in range(n_pair)], axis=0)
    y = jnp.dot(stacked, pair_ones, preferred_element_type=F32)
    full = jnp.concatenate([y[j * blk:(j + 1) * blk] for j in range(n_pair)], axis=1)
    return [full[2 * i * rows:(2 * i + 1) * rows] + full[(2 * i + 1) * rows:(2 * i + 2) * rows]
            for i in range(len(xs))]


def _pair_ones():
    hd = RWKV_HEAD_DIM
    return jnp.where((lax.broadcasted_iota(jnp.int32, (LANES, LANES), 0) < hd)
                     == (lax.broadcasted_iota(jnp.int32, (LANES, LANES), 1) < hd), 1.0, 0.0).astype(BF16)


def _dot3(a, b):
    a1, a2 = _split2(a)
    b1, b2 = _split2(b)
    n = a.shape[0]
    o = jnp.dot(jnp.concatenate([a1, a2], axis=0), b1, preferred_element_type=F32)
    return o[:n] + o[n:] + jnp.dot(a1, b2, preferred_element_type=F32)


def _pair_mm(lhs_splits, w_split, bd2):
    hi = jnp.concatenate([s[0] for s in lhs_splits], axis=0)
    lo = jnp.concatenate([s[1] for s in lhs_splits], axis=0)
    n = hi.shape[0]
    o = jnp.dot(jnp.concatenate([hi, lo], axis=0), _bdiag(w_split[0], bd2), preferred_element_type=F32)
    tot = o[:n] + o[n:] + jnp.dot(hi, _bdiag(w_split[1], bd2), preferred_element_type=F32)
    return [tot[i * RWKV_HEAD_DIM:(i + 1) * RWKV_HEAD_DIM] for i in range(len(lhs_splits))]


def _pair_rows(x_t, j, low_half):
    base = j * LANES
    return jnp.where(low_half, x_t[base:base + RWKV_HEAD_DIM], x_t[base + RWKV_HEAD_DIM:base + LANES])


def _rwkv_prep(d, u, prow, nrow, mix, w0, w2, a0, a2, kkw, ka, rk, bd):
    lc = RWKV_CHUNK
    lb = u.shape[0]
    ni = RWKV_INNER
    half_mix = 0.5 * mix
    nbr = pltpu.roll(u, 1, axis=0) + pltpu.roll(u, lb - 1, axis=0)
    sub = lax.broadcasted_iota(jnp.int32, (SUBLANES, 1), 0)
    fix_first = jnp.where(sub == 0, prow - u[lb - 1:lb, :], 0.0)
    fix_last = jnp.where(sub == SUBLANES - 1, nrow - u[0:1, :], 0.0)
    nbr = jnp.concatenate([nbr[:SUBLANES] + fix_first, nbr[SUBLANES:lb - SUBLANES],
                           nbr[lb - SUBLANES:] + fix_last], axis=0)
    u = (1.0 - mix) * u + half_mix * nbr
    r, k, v = u[:, 0:ni], u[:, ni:2 * ni], u[:, 2 * ni:3 * ni]
    o = 3 * ni
    w_lo = u[:, o:o + RWKV_DECAY_LORA]
    a_lo = u[:, o + RWKV_DECAY_LORA:o + RWKV_DECAY_LORA + RWKV_AAA_LORA]
    g_lo = u[:, o + RWKV_DECAY_LORA + RWKV_AAA_LORA:]

    logw = -math.exp(-0.5) * jax.nn.sigmoid(w0 + _dot3(jnp.tanh(w_lo), w2))
    a_gate = jax.nn.sigmoid(a0 + _dot3(a_lo, a2))
    kk = k * kkw
    kd = k * (1.0 + (a_gate - 1.0) * ka)
    kk_ss, rk_sum = _head_sums([kk * kk, r * kd * rk], bd)
    kk = kk / jnp.maximum(jnp.sqrt(kk_ss), 1e-12)
    bvec = kk * a_gate
    bonus = rk_sum * v

    row = lax.broadcasted_iota(jnp.int32, (lb, lb), 0)
    col = lax.broadcasted_iota(jnp.int32, (lb, lb), 1)
    lead = (row - col) if d == 0 else (col - row)
    incl = jnp.where(row // lc == col // lc, lead, -1) >= 0
    l1, l2, l3 = _split3(logw)
    cw3 = jnp.dot(incl.astype(BF16), jnp.concatenate([l1, l2, l3], axis=1), preferred_element_type=F32)
    cw = cw3[:, 0:ni] + cw3[:, ni:2 * ni] + cw3[:, 2 * ni:3 * ni]

    def dup_t(x):
        return jnp.concatenate([x, x], axis=0).T

    chunks = []
    for c in range(lb // lc):
        rs = slice(c * lc, (c + 1) * lc)
        logw_c, cw_c = logw[rs], cw[rs]
        ctot = jnp.sum(logw_c, axis=0, keepdims=True)
        e_neg = jnp.exp(-cw_c)
        e_end = jnp.exp(ctot - cw_c)
        chunks.append(dict(
            a_t=-kk[rs] * jnp.exp(cw_c - logw_c), r_t=r[rs] * jnp.exp(cw_c), v=v[rs],
            bn_t=dup_t(bvec[rs] * e_neg), kn_t=dup_t(kd[rs] * e_neg),
            be_t=dup_t(bvec[rs] * e_end), ke_t=dup_t(kd[rs] * e_end),
            wtot_col=jnp.exp(jnp.sum(logw_c.T, axis=1, keepdims=True))))
    return chunks, bonus, g_lo


def _rwkv_kernel(uf_ref, ufp_ref, ufn_ref, ub_ref, ubp_ref, ubn_ref, mix_ref, w0_ref, w2_ref, a0_ref, a2_ref,
                 g2_ref, kk_ref, ka_ref, rk_ref, of_ref, ob_ref, s_ref, *, ncx, nc):
    i = pl.program_id(1)
    lc = RWKV_CHUNK
    ni = RWKV_INNER
    hd = RWKV_HEAD_DIM
    chunk = (i, _chunk_of(1, i, ncx, nc))

    @pl.when(i == 0)
    def _():
        s_ref[...] = jnp.zeros_like(s_ref)

    bd2 = ((lax.broadcasted_iota(jnp.int32, (LANES, LANES), 0) < hd)
           == (lax.broadcasted_iota(jnp.int32, (LANES, LANES), 1) < hd))
    pair_ones = jnp.where(bd2, 1.0, 0.0).astype(BF16)
    u_refs = ((uf_ref, ufp_ref, ufn_ref), (ub_ref, ubp_ref, ubn_ref))
    out_refs = (of_ref, ob_ref)
    prep = []
    for d in (0, 1):
        seg_start, seg_end = _seg_edges(chunk[d], ncx, nc)
        u_ref, up_ref, un_ref = u_refs[d]
        prow = jnp.where(seg_start, 0.0, up_ref[0, SUBLANES - 1:SUBLANES, :])
        nrow = jnp.where(seg_end, 0.0, un_ref[0, 0:1, :])
        chunks, bonus, g_lo = _rwkv_prep(d, u_ref[0], prow, nrow, mix_ref[...], w0_ref[d], w2_ref[d], a0_ref[d],
                                         a2_ref[...], kk_ref[...], ka_ref[...], rk_ref[...], pair_ones)
        out_refs[d][0, :, ni:2 * ni] = bonus
        if d == 0:
            of_ref[0, :, 2 * ni:3 * ni] = jnp.dot(jax.nn.sigmoid(g_lo).astype(BF16), g2_ref[...],
                                                  preferred_element_type=F32)
        prep.append(chunks)

    row4 = lax.broadcasted_iota(jnp.int32, (lc, 2 * LANES), 0)
    col4 = lax.broadcasted_iota(jnp.int32, (lc, 2 * LANES), 1) % hd
    strict4 = (col4 < row4, col4 > row4)
    incl4 = (col4 <= row4, col4 >= row4)
    eye2 = (lax.broadcasted_iota(jnp.int32, (lc, LANES), 1) % hd
            == lax.broadcasted_iota(jnp.int32, (lc, LANES), 0)).astype(F32)
    low_half = lax.broadcasted_iota(jnp.int32, (hd, LANES), 1) < hd
    zeros_w = jnp.zeros((LANES, LANES), F32)

    def dot(a, b):
        return jnp.dot(a.astype(BF16), b.astype(BF16), preferred_element_type=F32)

    nch = RWKV_BLOCK // lc
    items = [(d, c, j) for d in (0, 1) for c in range(nch) for j in range(RWKV_PAIRS)]
    pl_ = {j: slice(j * LANES, (j + 1) * LANES) for j in range(RWKV_PAIRS)}
    n_it = range(len(items))
    a_t = [prep[d][c]["a_t"][:, pl_[j]] for d, c, j in items]
    r_t = [prep[d][c]["r_t"][:, pl_[j]] for d, c, j in items]
    v_w = [_bdiag(prep[d][c]["v"][:, pl_[j]], bd2) for d, c, j in items]
    w_p = [jnp.concatenate([jnp.where(bd2, prep[d][c]["bn_t"][pl_[j], :], 0.0),
                            jnp.where(bd2, prep[d][c]["kn_t"][pl_[j], :], 0.0)], axis=1) for d, c, j in items]
    p = [dot(jnp.concatenate([a_t[n], r_t[n]], axis=0), w_p[n]) for n in n_it]
    m_a = [jnp.where(strict4[items[n][0]], p[n][:lc], 0.0) for n in n_it]
    m_r = [jnp.where(incl4[items[n][0]], p[n][lc:], 0.0) for n in n_it]
    mv = [dot(m_a[n][:, LANES:], v_w[n]) for n in n_it]
    x = [m_a[n][:, :LANES] for n in n_it]
    t = [eye2 + x[n] for n in n_it]
    xs = [_split2(x[n]) for n in n_it]
    x = [_pair_mm([xs[n]], xs[n], bd2)[0] for n in n_it]
    for _ in range(int(math.log2(lc)) - 2):
        xs = [_split2(x[n]) for n in n_it]
        tx = [_pair_mm([_split2(t[n]), xs[n]], xs[n], bd2) for n in n_it]
        t = [t[n] + tx[n][0] for n in n_it]
        x = [tx[n][1] for n in n_it]
    t = [t[n] + _pair_mm([_split2(t[n])], _split2(x[n]), bd2)[0] for n in n_it]
    au = [dot(t[n], jnp.concatenate([_bdiag(a_t[n], bd2), _bdiag(mv[n], bd2)], axis=1)) for n in n_it]
    w2 = [jnp.concatenate([jnp.concatenate([_bdiag(au[n][:, :LANES], bd2), _bdiag(au[n][:, LANES:], bd2)], axis=1),
                           jnp.concatenate([zeros_w, v_w[n]], axis=1)], axis=0) for n in n_it]
    q_bk = [jnp.concatenate([_pair_rows(prep[d][c]["be_t"], j, low_half),
                             _pair_rows(prep[d][c]["ke_t"], j, low_half)], axis=1)
            for d, c, j in items]
    big = [dot(jnp.concatenate([q_bk[n], m_r[n]], axis=0), w2[n]) for n in n_it]
    wc = [jnp.where(low_half,
                    jnp.broadcast_to(prep[d][c]["wtot_col"][j * LANES:j * LANES + hd, :], (hd, LANES)),
                    jnp.broadcast_to(prep[d][c]["wtot_col"][j * LANES + hd:(j + 1) * LANES, :], (hd, LANES)))
          for d, c, j in items]

    idx = {it: n for n, it in enumerate(items)}
    heads = [(d, j) for d in (0, 1) for j in range(RWKV_PAIRS)]
    st = {dj: s_ref[dj[0], dj[1]] for dj in heads}
    for k in range(nch):
        sel = [(d, (k if d == 0 else nch - 1 - k), j) for d, j in heads]
        sy = [dot(jnp.concatenate([r_t[idx[it]] + big[idx[it]][lc:, :LANES], big[idx[it]][:lc, :LANES]], axis=0),
                  _bdiag(st[(it[0], it[2])], bd2)) for it in sel]
        for m, (d, c, j) in enumerate(sel):
            out_refs[d][0, c * lc:(c + 1) * lc, pl_[j]] = sy[m][:lc] + big[idx[(d, c, j)]][lc:, LANES:]
        for m, (d, c, j) in enumerate(sel):
            n = idx[(d, c, j)]
            st[(d, j)] = wc[n] * st[(d, j)] + sy[m][lc:] + big[n][:lc, LANES:]
    for d, j in heads:
        s_ref[d, j] = st[(d, j)]


def rwkv_scan(u_rwkv, mix, w0, w2, a0, a2, g2, k_k, k_a, r_k, ncx_tokens):
    b, ta, _ = u_rwkv.shape
    lc = RWKV_BLOCK
    nc = ta // lc
    ncx = ncx_tokens // lc
    nb = lc // SUBLANES
    nblk = ta // SUBLANES
    ni = RWKV_INNER
    kern = functools.partial(_rwkv_kernel, ncx=ncx, nc=nc)
    const2 = lambda bi, i: (0, 0)
    const3 = lambda bi, i: (0, 0, 0)
    fwd = lambda i: i
    bwd = lambda i: _chunk_of(1, i, ncx, nc)

    def u_specs(ch):
        return [pl.BlockSpec((1, lc, RWKV_IN), lambda bi, i: (bi, ch(i), 0)),
                pl.BlockSpec((1, SUBLANES, RWKV_IN), lambda bi, i: (bi, jnp.maximum(ch(i) * nb - 1, 0), 0)),
                pl.BlockSpec((1, SUBLANES, RWKV_IN), lambda bi, i: (bi, jnp.minimum((ch(i) + 1) * nb, nblk - 1), 0))]

    return pl.pallas_call(
        kern,
        out_shape=(jax.ShapeDtypeStruct((b, ta, 3 * ni), F32), jax.ShapeDtypeStruct((b, ta, 2 * ni), F32)),
        grid=(b, nc),
        in_specs=u_specs(fwd) + u_specs(bwd) + [
            pl.BlockSpec((1, RWKV_IN), const2),
            pl.BlockSpec((2, 1, ni), const3),
            pl.BlockSpec((2, RWKV_DECAY_LORA, ni), const3),
            pl.BlockSpec((2, 1, ni), const3),
            pl.BlockSpec((RWKV_AAA_LORA, ni), const2),
            pl.BlockSpec((RWKV_GATE_LORA, ni), const2),
            pl.BlockSpec((1, ni), const2),
            pl.BlockSpec((1, ni), const2),
            pl.BlockSpec((1, ni), const2),
        ],
        out_specs=(pl.BlockSpec((1, lc, 3 * ni), lambda bi, i: (bi, i, 0)),
                   pl.BlockSpec((1, lc, 2 * ni), lambda bi, i: (bi, bwd(i), 0))),
        scratch_shapes=[pltpu.VMEM((2, RWKV_PAIRS, RWKV_HEAD_DIM, LANES), F32)],
        compiler_params=_cparams(2),
        name="rwkv_scan",
    )(u_rwkv, u_rwkv, u_rwkv, u_rwkv, u_rwkv, u_rwkv, mix, w0, w2, a0, a2, g2, k_k, k_a, r_k)


def _merge_kernel(ssd_ref, z_ref, retf_ref, retb_ref, rg_ref, rwf_ref, rwb_ref, gate_ref, x_ref, mod_ref, nw_ref,
                  ssdnw_ref, lnw_ref, lnb_ref, wso_ref, wro_ref, wwo_ref, wo_ref, o_ref):
    ys = (ssd_ref[0, 0] + ssd_ref[1, 0]) * _silu(z_ref[0][:, :SSD_INNER])
    parts = []
    for g in range(SSD_GROUPS):
        yg = ys[:, g * SSD_GROUP_W:(g + 1) * SSD_GROUP_W]
        parts.append(yg * lax.rsqrt(jnp.mean(yg * yg, axis=-1, keepdims=True) + EPS))
    ys = jnp.concatenate(parts, axis=1) * ssdnw_ref[...]
    o_ssd = jnp.dot(ys.astype(BF16), wso_ref[...], preferred_element_type=F32)

    yr = retf_ref[0] + retb_ref[0]
    parts = []
    for h in range(RET_HEADS):
        yh = yr[:, h * RET_VAL_DIM:(h + 1) * RET_VAL_DIM]
        yc = yh - jnp.mean(yh, axis=-1, keepdims=True)
        parts.append(yc * lax.rsqrt(jnp.mean(yc * yc, axis=-1, keepdims=True) + EPS))
    yr = jnp.concatenate(parts, axis=1) * _silu(rg_ref[0])
    o_ret = jnp.dot(yr.astype(BF16), wro_ref[...], preferred_element_type=F32)

    ni = RWKV_INNER
    rw0 = rwf_ref[0]
    rw1 = rwb_ref[0]
    yw = rw0[:, :ni] + rw1[:, :ni]
    pair_ones = _pair_ones()
    inv_hd = 1.0 / RWKV_HEAD_DIM
    yc = yw - _head_sums([yw], pair_ones)[0] * inv_hd
    var = _head_sums([yc * yc], pair_ones)[0] * inv_hd
    yw = yc * lax.rsqrt(var + RWKV_LN_EPS) * lnw_ref[...] + lnb_ref[...]
    yw = (yw + rw0[:, ni:2 * ni] + rw1[:, ni:2 * ni]) * rw0[:, 2 * ni:3 * ni]
    o_rw = jnp.dot(yw.astype(BF16), wwo_ref[...], preferred_element_type=F32)

    gate = gate_ref[0]
    merged = (jax.nn.sigmoid(gate[:, :D_MODEL]) * o_ssd
              + jax.nn.sigmoid(gate[:, D_MODEL:2 * D_MODEL]) * o_ret
              + jax.nn.sigmoid(gate[:, 2 * D_MODEL:]) * o_rw)
    yx = jnp.dot(merged.astype(BF16), wo_ref[...], preferred_element_type=F32)
    g1 = mod_ref[0, 0, 2:3, :]
    o_ref[0] = x_ref[0] + g1 * _rms(yx, nw_ref[...])


def merge_out(ssd_y, u_zdt, ret_f, ret_b, u_ret, rw_f, rw_b, u_gate, x, mod, nw1, ssd_nw, ln_w, ln_b,
              w_ssd_out, w_ret_out, w_rwkv_out, w_out, n_ctx_tiles):
    b, ta, dm = x.shape
    tm = ROW_TILE
    row = lambda bi, i: (bi, i, 0)
    both = lambda bi, i: (0, bi, i, 0)
    const2 = lambda bi, i: (0, 0)
    return pl.pallas_call(
        _merge_kernel,
        out_shape=jax.ShapeDtypeStruct((b, ta, dm), F32),
        grid=(b, ta // tm),
        in_specs=[
            pl.BlockSpec((2, 1, tm, SSD_INNER), both),
            pl.BlockSpec((1, tm, u_zdt.shape[2]), row),
            pl.BlockSpec((1, tm, RET_INNER), row),
            pl.BlockSpec((1, tm, RET_INNER), row),
            pl.BlockSpec((1, tm, RET_INNER), lambda bi, i: (bi, i, 2)),
            pl.BlockSpec((1, tm, 3 * RWKV_INNER), row),
            pl.BlockSpec((1, tm, 2 * RWKV_INNER), row),
            pl.BlockSpec((1, tm, N_BRANCH * dm), row),
            pl.BlockSpec((1, tm, dm), row),
            pl.BlockSpec((1, 1, SUBLANES, dm), lambda bi, i: (bi, (i >= n_ctx_tiles).astype(jnp.int32), 0, 0)),
            pl.BlockSpec((1, dm), const2),
            pl.BlockSpec((1, SSD_INNER), const2),
            pl.BlockSpec((1, RWKV_INNER), const2),
            pl.BlockSpec((1, RWKV_INNER), const2),
            pl.BlockSpec((SSD_INNER, dm), const2),
            pl.BlockSpec((RET_INNER, dm), const2),
            pl.BlockSpec((RWKV_INNER, dm), const2),
            pl.BlockSpec((dm, dm), const2),
        ],
        out_specs=pl.BlockSpec((1, tm, dm), row),
        compiler_params=_cparams(2),
        name="merge_out",
    )(ssd_y, u_zdt, ret_f, ret_b, u_ret, rw_f, rw_b, u_gate, x, mod, nw1, ssd_nw, ln_w, ln_b,
      w_ssd_out, w_ret_out, w_rwkv_out, w_out)


def _mlp_kernel(x_ref, mod_ref, nw2_ref, nw3_ref, w1_ref, w2_ref, *rest, ff_tile, with_next):
    nb, tm, dm = x_ref.shape
    x = x_ref[...]
    mod = mod_ref[:, 0]
    y = _rms(x, nw2_ref[...])
    h = (y * (1.0 + mod[:, 4:5, :]) + mod[:, 3:4, :]).astype(BF16).reshape(nb * tm, dm)
    acc = jnp.zeros((nb * tm, dm), F32)
    for j in range(D_FF // ff_tile):
        hid = jnp.dot(h, w1_ref[:, j * ff_tile:(j + 1) * ff_tile], preferred_element_type=F32)
        hid = jnp.square(jnp.maximum(hid, 0.0)).astype(BF16)
        acc = acc + jnp.dot(hid, w2_ref[j * ff_tile:(j + 1) * ff_tile, :], preferred_element_type=F32)
    x_new = x + mod[:, 5:6, :] * _rms(acc.reshape(nb, tm, dm), nw3_ref[...])
    if with_next:
        nwn_ref, modn_ref, o_ref, h_ref = rest
        modn = modn_ref[:, 0]
        h_ref[...] = (_rms(x_new, nwn_ref[...]) * (1.0 + modn[:, 1:2, :]) + modn[:, 0:1, :]).astype(BF16)
    else:
        o_ref, = rest
    o_ref[...] = x_new


def mlp_block(x, mod, nw2, nw3, w1, w2, n_ctx_tiles, skip_tiles=0, next_norm=None, ff_tile=1024):
    b, ta, dm = x.shape
    tm = ROW_TILE
    rows = lambda i: (0, i, 0)
    const2 = lambda i: (0, 0)
    mod_spec = pl.BlockSpec((b, 1, SUBLANES, dm),
                            lambda i: (0, (i + skip_tiles >= n_ctx_tiles).astype(jnp.int32), 0, 0))
    resident = pl.Buffered(1)
    in_specs = [
        pl.BlockSpec((b, tm, dm), lambda i: (0, i + skip_tiles, 0)),
        mod_spec,
        pl.BlockSpec((1, dm), const2),
        pl.BlockSpec((1, dm), const2),
        pl.BlockSpec((dm, D_FF), const2, pipeline_mode=resident),
        pl.BlockSpec((D_FF, dm), const2, pipeline_mode=resident),
    ]
    args = [x, mod, nw2, nw3, w1, w2]
    out_rows = ta - skip_tiles * tm
    out_shape = jax.ShapeDtypeStruct((b, out_rows, dm), F32)
    out_specs = pl.BlockSpec((b, tm, dm), rows)
    if next_norm is not None:
        in_specs += [pl.BlockSpec((1, dm), const2), mod_spec]
        args += list(next_norm)
        out_shape = (out_shape, jax.ShapeDtypeStruct((b, out_rows, dm), BF16))
        out_specs = (out_specs, pl.BlockSpec((b, tm, dm), rows))
    return pl.pallas_call(
        functools.partial(_mlp_kernel, ff_tile=ff_tile, with_next=next_norm is not None),
        out_shape=out_shape,
        grid=(ta // tm - skip_tiles,),
        in_specs=in_specs,
        out_specs=out_specs,
        compiler_params=_cparams(1),
        name="mlp",
    )(*args)


def _rope_tables(n_ctx, n_lat):
    rows = n_lat // GRID_W
    row = np.repeat(np.arange(rows), GRID_W).astype(np.float32)
    col = np.tile(np.arange(GRID_W), rows).astype(np.float32)
    n_freq = RET_KEY_DIM // 4
    inv = jnp.power(ROPE_BASE, -jnp.arange(n_freq, dtype=F32) / n_freq)
    ang = jnp.concatenate([jnp.asarray(row)[:, None] * inv, jnp.asarray(col)[:, None] * inv], axis=-1)
    cos = jnp.concatenate([jnp.ones((n_ctx, RET_KEY_DIM // 2), F32), jnp.cos(ang)], axis=0)
    sin = jnp.concatenate([jnp.zeros((n_ctx, RET_KEY_DIM // 2), F32), jnp.sin(ang)], axis=0)
    cos_h = jnp.concatenate([cos, cos], axis=1)
    sin_h = jnp.concatenate([-sin, sin], axis=1)
    reps = 2 * RET_HEADS
    return jnp.tile(cos_h, (1, reps)), jnp.tile(sin_h, (1, reps))


def _pad_lanes(a, width=LANES):
    return jnp.pad(a, [(0, 0)] * (a.ndim - 1) + [(0, width - a.shape[-1])])


def kernel(x, c, ctx, c_ctx, norm_w, ada_w, ada_b, w_in, ssd_conv_w, ssd_conv_b, ssd_dt_bias, ssd_a_log,
           ssd_d, ssd_norm_w, ret_decay, rwkv_mix, rwkv_w0, rwkv_w2, rwkv_a0, rwkv_a2, rwkv_g2, rwkv_k_k,
           rwkv_k_a, rwkv_r_k, rwkv_lnx_w, rwkv_lnx_b, w_ssd_out, w_ret_out, w_rwkv_out, w_out, mlp_w1, mlp_w2):
    b, n_lat, dm = x.shape
    n_ctx = ctx.shape[1]
    ta = n_ctx + n_lat
    depth = norm_w.shape[0]
    n_ctx_tiles = n_ctx // ROW_TILE
    assert n_ctx % ROW_TILE == 0 and n_lat % ROW_TILE == 0 and b + 1 <= SUBLANES

    hexp = np.zeros((LANES, SSD_INNER), np.float32)
    for h in range(SSD_HEADS):
        hexp[h, h * SSD_HEAD_DIM:(h + 1) * SSD_HEAD_DIM] = 1.0
    hexp = jnp.asarray(hexp, BF16)
    cosx, sinx = _rope_tables(n_ctx, n_lat)

    cond = jnp.zeros((SUBLANES, dm), F32).at[:b].set(c).at[b].set(c_ctx)
    mod_all = modulation_all(cond, ada_w, ada_b)

    sizes = (N_BRANCH * dm, SSD_INNER, SSD_CONV_DIM, 2 * SSD_HEADS, RET_QK, RET_QK, RET_INNER, RET_INNER, RWKV_IN)
    offs = np.concatenate([[0], np.cumsum(sizes)])
    perm = np.concatenate([np.concatenate([np.arange(0, RET_KEY_DIM, 2), np.arange(1, RET_KEY_DIM, 2)]) + h * RET_KEY_DIM
                           for h in range(RET_HEADS)])

    xall = jnp.concatenate([ctx, x], axis=1)
    mods = []
    for l in range(depth):
        m = mod_all[l].reshape(SUBLANES, 6, dm)
        m = jnp.pad(m, ((0, 0), (0, SUBLANES - 6), (0, 0)))
        mods.append(jnp.stack([jnp.broadcast_to(m[b], (b,) + m.shape[1:]), m[:b]], axis=1))
    h = norm_modulate(xall, norm_w[0][0:1], mods[0], n_ctx_tiles)
    tm_mm = 1024 if (b * ta) % 1024 == 0 else ROW_TILE
    for l in range(depth):
        wl = w_in[l]
        seg = [wl[:, offs[j]:offs[j + 1]] for j in range(len(sizes))]
        w_gate = seg[0].astype(BF16)
        w_zdt = jnp.concatenate([seg[1], _pad_lanes(seg[3][:, :SSD_HEADS]), _pad_lanes(seg[3][:, SSD_HEADS:])],
                                axis=1).astype(BF16)
        w_xbc = seg[2].astype(BF16)
        w_ret = jnp.concatenate([seg[4][:, perm], seg[5][:, perm], seg[6], seg[7]], axis=1).astype(BF16)
        w_rw = seg[8].astype(BF16)

        mod = mods[l]
        nw = norm_w[l]
        h = h.reshape(b * ta, dm)
        u_gate = matmul(h, w_gate, tm_mm, 1024).reshape(b, ta, -1)
        u_zdt = matmul(h, w_zdt, tm_mm, w_zdt.shape[1]).reshape(b, ta, -1)
        u_xbc = matmul(h, w_xbc, tm_mm, SSD_CONV_DIM).reshape(b, ta, -1)
        u_ret = matmul(h, w_ret, tm_mm, w_ret.shape[1]).reshape(b, ta, -1)
        u_rw = matmul(h, w_rw, tm_mm, RWKV_IN).reshape(b, ta, -1)

        conv_w = jnp.pad(ssd_conv_w[l], ((0, SUBLANES - SSD_CONV), (0, 0)))
        dskip = jnp.repeat(ssd_d[l], SSD_HEAD_DIM)[None, :]
        ssd_y = ssd_scan(u_xbc, u_zdt, conv_w, ssd_conv_b[l][None, :],
                         _pad_lanes(ssd_dt_bias[l])[:, None, :], _pad_lanes(ssd_a_log[l])[:, None, :],
                         dskip, hexp, n_ctx)
        ret_f, ret_b = ret_scan(u_ret, cosx, sinx, _pad_lanes(ret_decay[l])[:, None, :], n_ctx)
        rw_f, rw_b = rwkv_scan(u_rw, rwkv_mix[l][None, :], rwkv_w0[l][:, None, :], rwkv_w2[l], rwkv_a0[l][:, None, :],
                         rwkv_a2[l], rwkv_g2[l].astype(BF16), rwkv_k_k[l][None, :], rwkv_k_a[l][None, :],
                         rwkv_r_k[l].reshape(1, RWKV_INNER), n_ctx)

        xall = merge_out(ssd_y, u_zdt, ret_f, ret_b, u_ret, rw_f, rw_b, u_gate, xall, mod, nw[1:2],
                         ssd_norm_w[l][None, :], rwkv_lnx_w[l][None, :], rwkv_lnx_b[l][None, :],
                         w_ssd_out[l].astype(BF16), w_ret_out[l].astype(BF16), w_rwkv_out[l].astype(BF16),
                         w_out[l].astype(BF16), n_ctx_tiles)
        w1, w2 = mlp_w1[l].astype(BF16), mlp_w2[l].astype(BF16)
        if l + 1 < depth:
            xall, h = mlp_block(xall, mod, nw[2:3], nw[3:4], w1, w2, n_ctx_tiles,
                                next_norm=(norm_w[l + 1][0:1], mods[l + 1]))
        else:
            xall = mlp_block(xall, mod, nw[2:3], nw[3:4], w1, w2, n_ctx_tiles, skip_tiles=n_ctx_tiles)
    return xall
```

```python
import functools
import math

import numpy as np
import jax
import jax.numpy as jnp
from jax import lax
from jax.experimental import pallas as pl
from jax.experimental.pallas import tpu as pltpu

F32 = jnp.float32
BF16 = jnp.bfloat16
HIGHEST = lax.Precision.HIGHEST

D_MODEL = 1024
GRID_W = 64
EPS = 1e-6
N_BRANCH = 3

SSD_HEADS = 16
SSD_HEAD_DIM = 64
SSD_INNER = SSD_HEADS * SSD_HEAD_DIM
SSD_GROUPS = 2
SSD_STATE = 128
SSD_CONV = 5
SSD_CHUNK = 128
SSD_BLOCK = 2 * SSD_CHUNK
SSD_CONV_DIM = SSD_INNER + 2 * SSD_GROUPS * SSD_STATE
SSD_GROUP_W = SSD_INNER // SSD_GROUPS

RET_HEADS = 4
RET_KEY_DIM = 64
RET_VAL_DIM = 128
RET_QK = RET_HEADS * RET_KEY_DIM
RET_INNER = RET_HEADS * RET_VAL_DIM
RET_CHUNK = 128
RET_BLOCK = 2 * RET_CHUNK
ROPE_BASE = 10000.0

RWKV_HEADS = 8
RWKV_HEAD_DIM = 64
RWKV_INNER = RWKV_HEADS * RWKV_HEAD_DIM
RWKV_DECAY_LORA = 64
RWKV_AAA_LORA = 64
RWKV_GATE_LORA = 128
RWKV_IN = 3 * RWKV_INNER + RWKV_DECAY_LORA + RWKV_AAA_LORA + RWKV_GATE_LORA
RWKV_LN_EPS = 64e-5
RWKV_CHUNK = 64

D_FF = 4 * D_MODEL

SUBLANES = 8
LANES = 128
VMEM_LIMIT_BYTES = 56 * 1024 * 1024

ROW_TILE = 256
NEG_BIG = -1e30


def _cparams(n_axes):
    return pltpu.CompilerParams(dimension_semantics=("arbitrary",) * n_axes,
                                vmem_limit_bytes=VMEM_LIMIT_BYTES)


def _silu(x):
    return x * jax.nn.sigmoid(x)


def _softplus(x):
    return jnp.maximum(x, 0.0) + jnp.log1p(jnp.exp(-jnp.abs(x)))


def _split3(x):
    x1 = x.astype(BF16)
    r1 = x - x1.astype(F32)
    x2 = r1.astype(BF16)
    r2 = r1 - x2.astype(F32)
    return x1, x2, r2.astype(BF16)


def _dot_nt(a, b, precision=None):
    return lax.dot_general(a, b, (((1,), (1,)), ((), ())), preferred_element_type=F32, precision=precision)


def _dot_tn(a, b, precision=None):
    return lax.dot_general(a, b, (((0,), (0,)), ((), ())), preferred_element_type=F32, precision=precision)


def _chunk_of(d, i, ncx, nc):
    bwd = jnp.where(i < ncx, ncx - 1 - i, ncx + nc - 1 - i)
    return jnp.where(d == 0, i, bwd)


def _seg_edges(c, ncx, nc):
    seg_start = jnp.logical_or(c == 0, c == ncx)
    seg_end = jnp.logical_or(c == ncx - 1, c == nc - 1)
    return seg_start, seg_end


def _tri_mask(n, d, strict, reps=1):
    row = lax.broadcasted_iota(jnp.int32, (n, n * reps), 0)
    col = lax.broadcasted_iota(jnp.int32, (n, n * reps), 1) % n
    lead = jnp.where(d == 0, row - col, col - row)
    return lead > 0 if strict else lead >= 0


def _mod_kernel(c_ref, w_ref, b_ref, o_ref):
    s = _silu(c_ref[...])
    o_ref[0] = jnp.dot(s, w_ref[0], preferred_element_type=F32, precision=HIGHEST) + b_ref[0]


def modulation_all(cond, ada_w, ada_b):
    depth = ada_w.shape[0]
    tn = 1536
    return pl.pallas_call(
        _mod_kernel,
        out_shape=jax.ShapeDtypeStruct((depth, SUBLANES, 6 * D_MODEL), F32),
        grid=(depth, 6 * D_MODEL // tn),
        in_specs=[pl.BlockSpec((SUBLANES, D_MODEL), lambda l, j: (0, 0)),
                  pl.BlockSpec((1, D_MODEL, tn), lambda l, j: (l, 0, j)),
                  pl.BlockSpec((1, 1, tn), lambda l, j: (l, 0, j))],
        out_specs=pl.BlockSpec((1, SUBLANES, tn), lambda l, j: (l, 0, j)),
        compiler_params=_cparams(2),
        name="modulation",
    )(cond, ada_w, ada_b.reshape(depth, 1, 6 * D_MODEL))


def _rms(x, w):
    return x * lax.rsqrt(jnp.mean(x * x, axis=-1, keepdims=True) + EPS) * w


def _normmod_kernel(x_ref, nw_ref, mod_ref, h_ref):
    y = _rms(x_ref[0], nw_ref[...])
    shift = mod_ref[0, 0, 0:1, :]
    scale = mod_ref[0, 0, 1:2, :]
    h_ref[0] = (y * (1.0 + scale) + shift).astype(BF16)


def norm_modulate(x, nw, mod, n_ctx_tiles):
    b, ta, dm = x.shape
    return pl.pallas_call(
        _normmod_kernel,
        out_shape=jax.ShapeDtypeStruct((b, ta, dm), BF16),
        grid=(b, ta // ROW_TILE),
        in_specs=[pl.BlockSpec((1, ROW_TILE, dm), lambda bi, i: (bi, i, 0)),
                  pl.BlockSpec((1, dm), lambda bi, i: (0, 0)),
                  pl.BlockSpec((1, 1, SUBLANES, dm), lambda bi, i: (bi, (i >= n_ctx_tiles).astype(jnp.int32), 0, 0))],
        out_specs=pl.BlockSpec((1, ROW_TILE, dm), lambda bi, i: (bi, i, 0)),
        compiler_params=_cparams(2),
        name="norm_modulate",
    )(x, nw, mod)


def _mm_kernel(a_ref, w_ref, o_ref):
    o_ref[...] = jnp.dot(a_ref[...], w_ref[...], preferred_element_type=F32).astype(o_ref.dtype)


def matmul(a, w, tm, tn, out_dtype=F32):
    r, k = a.shape
    n = w.shape[1]
    assert r % tm == 0 and n % tn == 0
    return pl.pallas_call(
        _mm_kernel,
        out_shape=jax.ShapeDtypeStruct((r, n), out_dtype),
        grid=(n // tn, r // tm),
        in_specs=[pl.BlockSpec((tm, k), lambda j, i: (i, 0)),
                  pl.BlockSpec((k, tn), lambda j, i: (0, j))],
        out_specs=pl.BlockSpec((tm, tn), lambda j, i: (i, j)),
        compiler_params=_cparams(2),
        name="in_proj",
    )(a, w)


def _ssd_kernel(xbc_ref, xp_ref, xn_ref, dt_ref, cw_ref, cb_ref, dtb_ref, alog_ref, dskip_ref,
                hexp_ref, y_ref, s_ref, ext_ref, act_ref, *, ncx, nc):
    d = pl.program_id(1)
    i = pl.program_id(2)
    c = _chunk_of(d, i, ncx, nc)
    seg_start, seg_end = _seg_edges(c, ncx, nc)
    lc = SSD_CHUNK
    lb = SSD_BLOCK
    nch = lb // lc
    hb = SUBLANES

    @pl.when(i == 0)
    def _():
        s_ref[...] = jnp.zeros_like(s_ref)

    @pl.when(d == 0)
    def _():
        ext_ref[0:hb, :] = jnp.where(seg_start, 0.0, xp_ref[0])
        ext_ref[hb:hb + lb, :] = xbc_ref[0]
        ext_ref[hb + lb:hb + lb + hb, :] = jnp.where(seg_end, 0.0, xn_ref[0])
        pad = SSD_CONV // 2
        acc = ext_ref[hb - pad:hb - pad + lb, :] * cw_ref[0:1, :]
        for j in range(1, SSD_CONV):
            acc = acc + ext_ref[hb - pad + j:hb - pad + j + lb, :] * cw_ref[j:j + 1, :]
        act = _silu(acc + cb_ref[...])
        act_ref[c] = act.astype(BF16)
        y_ref[0, 0] = act[:, :SSD_INNER] * dskip_ref[...]

    @pl.when(d != 0)
    def _():
        y_ref[0, 0] = jnp.zeros(y_ref.shape[2:], F32)

    tri = _tri_mask(lc, d, strict=False)
    hg = SSD_HEADS // SSD_GROUPS
    low_half = lax.broadcasted_iota(jnp.int32, (lc, 2 * SSD_HEAD_DIM), 1) < SSD_HEAD_DIM
    neg_a = -jnp.exp(alog_ref[0])

    local = []
    for k in range(nch):
        rs = pl.ds(pl.multiple_of(jnp.where(d == 0, k, nch - 1 - k) * lc, lc), lc)
        xbc = act_ref[c, rs, :]
        bm = xbc[:, SSD_INNER:SSD_INNER + SSD_GROUPS * SSD_STATE]
        cm = xbc[:, SSD_INNER + SSD_GROUPS * SSD_STATE:]
        dt = _softplus(dt_ref[0, rs, :] + dtb_ref[0])
        la = dt * neg_a
        ac3 = jnp.dot(tri.astype(BF16), jnp.concatenate(_split3(la), axis=1), preferred_element_type=F32)
        acum = ac3[:, :LANES] + ac3[:, LANES:2 * LANES] + ac3[:, 2 * LANES:]
        atot = jnp.sum(la, axis=0, keepdims=True)
        acum_t = acum.T
        etot8 = jnp.broadcast_to(jnp.exp(atot), (SUBLANES, LANES))
        parts = jnp.concatenate(_split2(jnp.exp(acum)) + _split2(etot8), axis=0)
        ex = jnp.dot(parts, hexp_ref[...], preferred_element_type=F32)
        eacx = ex[0:lc] + ex[lc:2 * lc]
        etotx = (ex[2 * lc:2 * lc + SUBLANES] + ex[2 * lc + SUBLANES:])[0:1, :]
        parts_b = jnp.concatenate([dt.astype(BF16), jnp.exp(atot - acum).astype(BF16)], axis=0)
        ex_b = jnp.dot(parts_b, hexp_ref[...], preferred_element_type=F32).astype(BF16)
        vb = xbc[:, :SSD_INNER] * ex_b[0:lc]
        vend = vb * ex_b[lc:2 * lc]
        groups = []
        for g in range(SSD_GROUPS):
            gs = slice(g * SSD_STATE, (g + 1) * SSD_STATE)
            gw = slice(g * SSD_GROUP_W, (g + 1) * SSD_GROUP_W)
            cg = cm[:, gs]
            bg = bm[:, gs]
            scores = _dot_nt(cg, bg)
            y_in = []
            for hp in range(hg // 2):
                a_pair = []
                for h in (g * hg + 2 * hp, g * hg + 2 * hp + 1):
                    diff = acum[:, h:h + 1] - acum_t[h:h + 1, :]
                    a_pair.append((scores * jnp.exp(jnp.where(tri, diff, NEG_BIG))).astype(BF16))
                ps = slice((g * hg + 2 * hp) * SSD_HEAD_DIM, (g * hg + 2 * hp + 2) * SSD_HEAD_DIM)
                v_pair = vb[:, ps]
                w_pair = jnp.concatenate([jnp.where(low_half, v_pair, 0.0), jnp.where(low_half, 0.0, v_pair)],
                                         axis=0)
                y_in.append(jnp.dot(jnp.concatenate(a_pair, axis=1), w_pair, preferred_element_type=F32))
            groups.append(dict(cg=cg, y_in=jnp.concatenate(y_in, axis=1), eac=eacx[:, gw], etot=etotx[:, gw],
                               cs=_dot_tn(bg, vend[:, gw])))
        local.append((rs, groups))

    for g in range(SSD_GROUPS):
        gw = slice(g * SSD_GROUP_W, (g + 1) * SSD_GROUP_W)
        st = s_ref[g]
        for rs, groups in local:
            q = groups[g]
            y_ref[0, 0, rs, gw] += q["y_in"] + jnp.dot(q["cg"], st.astype(BF16),
                                                       preferred_element_type=F32) * q["eac"]
            st = st * q["etot"] + q["cs"]
        s_ref[g] = st


def ssd_scan(u_xbc, u_zdt, conv_w, conv_b, dt_bias, a_log, dskip, hexp, ncx_tokens):
    b, ta, _ = u_xbc.shape
    lc = SSD_BLOCK
    nc = ta // lc
    ncx = ncx_tokens // lc
    nb = lc // SUBLANES
    nblk = ta // SUBLANES
    ch = functools.partial(_chunk_of, ncx=ncx, nc=nc)
    kern = functools.partial(_ssd_kernel, ncx=ncx, nc=nc)
    zdt_blk0 = SSD_INNER // LANES
    return pl.pallas_call(
        kern,
        out_shape=jax.ShapeDtypeStruct((2, b, ta, SSD_INNER), F32),
        grid=(b, 2, nc),
        in_specs=[
            pl.BlockSpec((1, lc, SSD_CONV_DIM), lambda bi, d, i: (bi, i * (1 - d), 0)),
            pl.BlockSpec((1, SUBLANES, SSD_CONV_DIM), lambda bi, d, i: (bi, jnp.maximum(i * nb - 1, 0) * (1 - d), 0)),
            pl.BlockSpec((1, SUBLANES, SSD_CONV_DIM),
                         lambda bi, d, i: (bi, jnp.minimum((i + 1) * nb, nblk - 1) * (1 - d), 0)),
            pl.BlockSpec((1, lc, LANES), lambda bi, d, i: (bi, ch(d, i), zdt_blk0 + d)),
            pl.BlockSpec((SUBLANES, SSD_CONV_DIM), lambda bi, d, i: (0, 0)),
            pl.BlockSpec((1, SSD_CONV_DIM), lambda bi, d, i: (0, 0)),
            pl.BlockSpec((1, 1, LANES), lambda bi, d, i: (d, 0, 0)),
            pl.BlockSpec((1, 1, LANES), lambda bi, d, i: (d, 0, 0)),
            pl.BlockSpec((1, SSD_INNER), lambda bi, d, i: (0, 0)),
            pl.BlockSpec((LANES, SSD_INNER), lambda bi, d, i: (0, 0)),
        ],
        out_specs=pl.BlockSpec((1, 1, lc, SSD_INNER), lambda bi, d, i: (d, bi, ch(d, i), 0)),
        scratch_shapes=[pltpu.VMEM((SSD_GROUPS, SSD_STATE, SSD_GROUP_W), F32),
                        pltpu.VMEM((lc + 2 * SUBLANES, SSD_CONV_DIM), F32),
                        pltpu.VMEM((nc, lc, SSD_CONV_DIM), BF16)],
        compiler_params=_cparams(3),
        name="ssd_scan",
    )(u_xbc, u_xbc, u_xbc, u_zdt, conv_w, conv_b, dt_bias, a_log, dskip, hexp)


def _ret_kernel(qkf_ref, vf_ref, cosf_ref, sinf_ref, qkb_ref, vb_ref, cosb_ref, sinb_ref, dec_ref,
                yf_ref, yb_ref, s_ref):
    i = pl.program_id(1)
    lc = RET_CHUNK
    lb = RET_BLOCK
    nch = lb // lc

    @pl.when(i == 0)
    def _():
        s_ref[...] = jnp.zeros_like(s_ref)

    in_refs = ((qkf_ref, vf_ref, cosf_ref, sinf_ref), (qkb_ref, vb_ref, cosb_ref, sinb_ref))
    out_refs = (yf_ref, yb_ref)
    half = RET_KEY_DIM // 2
    width = 2 * RET_QK
    lane = lax.broadcasted_iota(jnp.int32, (lb, width), 1)
    row = lax.broadcasted_iota(jnp.int32, (lc, lc), 0)
    col = lax.broadcasted_iota(jnp.int32, (lc, lc), 1)
    dist = jnp.abs(row - col).astype(F32)
    tri = (col <= row, col >= row)
    t_idx = lax.broadcasted_iota(jnp.int32, (lc, RET_VAL_DIM), 0).astype(F32)
    pos = (t_idx, lc - 1.0 - t_idx)
    q, k_t, v, lg = [], [], [], []
    for d in (0, 1):
        qk_ref, v_ref, cos_ref, sin_ref = in_refs[d]
        qk = qk_ref[0]
        swapped = jnp.where((lane % RET_KEY_DIM) < half,
                            pltpu.roll(qk, width - half, axis=1), pltpu.roll(qk, half, axis=1))
        qk = qk * cos_ref[...] + swapped * sin_ref[...]
        q.append(qk[:, :RET_QK].astype(BF16))
        k_t.append((qk[:, RET_QK:] * (RET_KEY_DIM ** -0.5)).T.astype(BF16))
        v.append(v_ref[0])
        lg.append(-_softplus(-dec_ref[d]))

    items = [(d, c, h) for d in (0, 1) for c in range(nch) for h in range(RET_HEADS)]
    n_it = range(len(items))
    idx = {it: n for n, it in enumerate(items)}
    rs = [slice(c * lc, (c + 1) * lc) for d, c, h in items]
    ks = [slice(h * RET_KEY_DIM, (h + 1) * RET_KEY_DIM) for d, c, h in items]
    vs = [slice(h * RET_VAL_DIM, (h + 1) * RET_VAL_DIM) for d, c, h in items]
    lgh = [lg[d][:, h:h + 1] for d, c, h in items]
    qh = [q[d][rs[n], ks[n]] for n, (d, c, h) in enumerate(items)]
    kh_t = [k_t[d][ks[n], rs[n]] for n, (d, c, h) in enumerate(items)]
    vh = [v[d][rs[n], vs[n]] for n, (d, c, h) in enumerate(items)]
    scores = [(jnp.dot(qh[n], kh_t[n], preferred_element_type=F32)
               * jnp.exp(jnp.where(tri[items[n][0]], dist * lgh[n], NEG_BIG))).astype(BF16) for n in n_it]
    y_in = [jnp.dot(scores[n], vh[n].astype(BF16), preferred_element_type=F32) for n in n_it]
    vend = [(vh[n] * jnp.exp((lc - 1.0 - pos[items[n][0]]) * lgh[n])).astype(BF16) for n in n_it]
    cs = [jnp.dot(kh_t[n], vend[n], preferred_element_type=F32) for n in n_it]
    heads = [(d, h) for d in (0, 1) for h in range(RET_HEADS)]
    st = {dh: s_ref[dh[0], dh[1]] for dh in heads}
    for k in range(nch):
        sel = [idx[(d, (k if d == 0 else nch - 1 - k), h)] for d, h in heads]
        y_st = [jnp.dot(qh[n], st[(items[n][0], items[n][2])].astype(BF16), preferred_element_type=F32)
                for n in sel]
        for m, n in enumerate(sel):
            d, c, h = items[n]
            out_refs[d][0, rs[n], vs[n]] = y_in[n] + y_st[m] * jnp.exp((pos[d] + 1.0) * lgh[n])
        for n in sel:
            d, c, h = items[n]
            st[(d, h)] = st[(d, h)] * jnp.exp(lc * lgh[n]) + cs[n]
    for d, h in heads:
        s_ref[d, h] = st[(d, h)]


def ret_scan(u_ret, cosx, sinx, ret_decay, ncx_tokens):
    b, ta, _ = u_ret.shape
    lc = RET_BLOCK
    nc = ta // lc
    ncx = ncx_tokens // lc
    fwd = lambda i: i
    bwd = lambda i: _chunk_of(1, i, ncx, nc)

    def specs(ch):
        return [pl.BlockSpec((1, lc, 2 * RET_QK), lambda bi, i: (bi, ch(i), 0)),
                pl.BlockSpec((1, lc, RET_INNER), lambda bi, i: (bi, ch(i), 1)),
                pl.BlockSpec((lc, 2 * RET_QK), lambda bi, i: (ch(i), 0)),
                pl.BlockSpec((lc, 2 * RET_QK), lambda bi, i: (ch(i), 0))]

    return pl.pallas_call(
        _ret_kernel,
        out_shape=(jax.ShapeDtypeStruct((b, ta, RET_INNER), F32), jax.ShapeDtypeStruct((b, ta, RET_INNER), F32)),
        grid=(b, nc),
        in_specs=specs(fwd) + specs(bwd) + [pl.BlockSpec((2, 1, LANES), lambda bi, i: (0, 0, 0))],
        out_specs=(pl.BlockSpec((1, lc, RET_INNER), lambda bi, i: (bi, i, 0)),
                   pl.BlockSpec((1, lc, RET_INNER), lambda bi, i: (bi, bwd(i), 0))),
        scratch_shapes=[pltpu.VMEM((2, RET_HEADS, RET_KEY_DIM, RET_VAL_DIM), F32)],
        compiler_params=_cparams(2),
        name="ret_scan",
    )(u_ret, u_ret, cosx, sinx, u_ret, u_ret, cosx, sinx, ret_decay)


RWKV_PAIRS = RWKV_HEADS // 2
RWKV_BLOCK = 2 * RWKV_CHUNK


def _bdiag(tile, bd2):
    return jnp.where(bd2, jnp.concatenate([tile, tile], axis=0), 0.0)


def _split2(x):
    hi = x.astype(BF16)
    return hi, (x - hi.astype(F32)).astype(BF16)


def _head_sums(xs, pair_ones):
    rows = xs[0].shape[0]
    parts = jnp.concatenate([p for x in xs for p in _split2(x)], axis=0)
    blk = parts.shape[0]
    n_pair = parts.shape[1] // LANES
    stacked = jnp.concatenate([parts[:, j * LANES:(j + 1) * LANES] for j in range(n_pair)], axis=0)
    y = jnp.dot(stacked, pair_ones, preferred_element_type=F32)
    full = jnp.concatenate([y[j * blk:(j + 1) * blk] for j in range(n_pair)], axis=1)
    return [full[2 * i * rows:(2 * i + 1) * rows] + full[(2 * i + 1) * rows:(2 * i + 2) * rows]
            for i in range(len(xs))]


def _pair_ones():
    hd = RWKV_HEAD_DIM
    return jnp.where((lax.broadcasted_iota(jnp.int32, (LANES, LANES), 0) < hd)
                     == (lax.broadcasted_iota(jnp.int32, (LANES, LANES), 1) < hd), 1.0, 0.0).astype(BF16)


def _dot3(a, b):
    a1, a2 = _split2(a)
    b1, b2 = _split2(b)
    n = a.shape[0]
    o = jnp.dot(jnp.concatenate([a1, a2], axis=0), b1, preferred_element_type=F32)
    return o[:n] + o[n:] + jnp.dot(a1, b2, preferred_element_type=F32)


def _pair_mm(lhs_splits, w_split, bd2):
    hi = jnp.concatenate([s[0] for s in lhs_splits], axis=0)
    lo = jnp.concatenate([s[1] for s in lhs_splits], axis=0)
    n = hi.shape[0]
    o = jnp.dot(jnp.concatenate([hi, lo], axis=0), _bdiag(w_split[0], bd2), preferred_element_type=F32)
    tot = o[:n] + o[n:] + jnp.dot(hi, _bdiag(w_split[1], bd2), preferred_element_type=F32)
    return [tot[i * RWKV_HEAD_DIM:(i + 1) * RWKV_HEAD_DIM] for i in range(len(lhs_splits))]


def _pair_rows(x_t, j, low_half):
    base = j * LANES
    return jnp.where(low_half, x_t[base:base + RWKV_HEAD_DIM], x_t[base + RWKV_HEAD_DIM:base + LANES])


def _rwkv_prep(d, u, prow, nrow, mix, w0, w2, a0, a2, kkw, ka, rk, bd):
    lc = RWKV_CHUNK
    lb = u.shape[0]
    ni = RWKV_INNER
    half_mix = 0.5 * mix
    nbr = pltpu.roll(u, 1, axis=0) + pltpu.roll(u, lb - 1, axis=0)
    sub = lax.broadcasted_iota(jnp.int32, (SUBLANES, 1), 0)
    fix_first = jnp.where(sub == 0, prow - u[lb - 1:lb, :], 0.0)
    fix_last = jnp.where(sub == SUBLANES - 1, nrow - u[0:1, :], 0.0)
    nbr = jnp.concatenate([nbr[:SUBLANES] + fix_first, nbr[SUBLANES:lb - SUBLANES],
                           nbr[lb - SUBLANES:] + fix_last], axis=0)
    u = (1.0 - mix) * u + half_mix * nbr
    r, k, v = u[:, 0:ni], u[:, ni:2 * ni], u[:, 2 * ni:3 * ni]
    o = 3 * ni
    w_lo = u[:, o:o + RWKV_DECAY_LORA]
    a_lo = u[:, o + RWKV_DECAY_LORA:o + RWKV_DECAY_LORA + RWKV_AAA_LORA]
    g_lo = u[:, o + RWKV_DECAY_LORA + RWKV_AAA_LORA:]

    logw = -math.exp(-0.5) * jax.nn.sigmoid(w0 + _dot3(jnp.tanh(w_lo), w2))
    a_gate = jax.nn.sigmoid(a0 + _dot3(a_lo, a2))
    kk = k * kkw
    kd = k * (1.0 + (a_gate - 1.0) * ka)
    kk_ss, rk_sum = _head_sums([kk * kk, r * kd * rk], bd)
    kk = kk / jnp.maximum(jnp.sqrt(kk_ss), 1e-12)
    bvec = kk * a_gate
    bonus = rk_sum * v

    row = lax.broadcasted_iota(jnp.int32, (lb, lb), 0)
    col = lax.broadcasted_iota(jnp.int32, (lb, lb), 1)
    lead = (row - col) if d == 0 else (col - row)
    incl = jnp.where(row // lc == col // lc, lead, -1) >= 0
    l1, l2, l3 = _split3(logw)
    cw3 = jnp.dot(incl.astype(BF16), jnp.concatenate([l1, l2, l3], axis=1), preferred_element_type=F32)
    cw = cw3[:, 0:ni] + cw3[:, ni:2 * ni] + cw3[:, 2 * ni:3 * ni]

    def dup_t(x):
        return jnp.concatenate([x, x], axis=0).T

    chunks = []
    for c in range(lb // lc):
        rs = slice(c * lc, (c + 1) * lc)
        logw_c, cw_c = logw[rs], cw[rs]
        ctot = jnp.sum(logw_c, axis=0, keepdims=True)
        e_neg = jnp.exp(-cw_c)
        e_end = jnp.exp(ctot - cw_c)
        chunks.append(dict(
            a_t=-kk[rs] * jnp.exp(cw_c - logw_c), r_t=r[rs] * jnp.exp(cw_c), v=v[rs],
            bn_t=dup_t(bvec[rs] * e_neg), kn_t=dup_t(kd[rs] * e_neg),
            be_t=dup_t(bvec[rs] * e_end), ke_t=dup_t(kd[rs] * e_end),
            wtot_col=jnp.exp(jnp.sum(logw_c.T, axis=1, keepdims=True))))
    return chunks, bonus, g_lo


def _rwkv_kernel(uf_ref, ufp_ref, ufn_ref, ub_ref, ubp_ref, ubn_ref, mix_ref, w0_ref, w2_ref, a0_ref, a2_ref,
                 g2_ref, kk_ref, ka_ref, rk_ref, of_ref, ob_ref, s_ref, *, ncx, nc):
    i = pl.program_id(1)
    lc = RWKV_CHUNK
    ni = RWKV_INNER
    hd = RWKV_HEAD_DIM
    chunk = (i, _chunk_of(1, i, ncx, nc))

    @pl.when(i == 0)
    def _():
        s_ref[...] = jnp.zeros_like(s_ref)

    bd2 = ((lax.broadcasted_iota(jnp.int32, (LANES, LANES), 0) < hd)
           == (lax.broadcasted_iota(jnp.int32, (LANES, LANES), 1) < hd))
    pair_ones = jnp.where(bd2, 1.0, 0.0).astype(BF16)
    u_refs = ((uf_ref, ufp_ref, ufn_ref), (ub_ref, ubp_ref, ubn_ref))
    out_refs = (of_ref, ob_ref)
    prep = []
    for d in (0, 1):
        seg_start, seg_end = _seg_edges(chunk[d], ncx, nc)
        u_ref, up_ref, un_ref = u_refs[d]
        prow = jnp.where(seg_start, 0.0, up_ref[0, SUBLANES - 1:SUBLANES, :])
        nrow = jnp.where(seg_end, 0.0, un_ref[0, 0:1, :])
        chunks, bonus, g_lo = _rwkv_prep(d, u_ref[0], prow, nrow, mix_ref[...], w0_ref[d], w2_ref[d], a0_ref[d],
                                         a2_ref[...], kk_ref[...], ka_ref[...], rk_ref[...], pair_ones)
        out_refs[d][0, :, ni:2 * ni] = bonus
        if d == 0:
            of_ref[0, :, 2 * ni:3 * ni] = jnp.dot(jax.nn.sigmoid(g_lo).astype(BF16), g2_ref[...],
                                                  preferred_element_type=F32)
        prep.append(chunks)

    row4 = lax.broadcasted_iota(jnp.int32, (lc, 2 * LANES), 0)
    col4 = lax.broadcasted_iota(jnp.int32, (lc, 2 * LANES), 1) % hd
    strict4 = (col4 < row4, col4 > row4)
    incl4 = (col4 <= row4, col4 >= row4)
    eye2 = (lax.broadcasted_iota(jnp.int32, (lc, LANES), 1) % hd
            == lax.broadcasted_iota(jnp.int32, (lc, LANES), 0)).astype(F32)
    low_half = lax.broadcasted_iota(jnp.int32, (hd, LANES), 1) < hd
    zeros_w = jnp.zeros((LANES, LANES), F32)

    def dot(a, b):
        return jnp.dot(a.astype(BF16), b.astype(BF16), preferred_element_type=F32)

    nch = RWKV_BLOCK // lc
    items = [(d, c, j) for d in (0, 1) for c in range(nch) for j in range(RWKV_PAIRS)]
    pl_ = {j: slice(j * LANES, (j + 1) * LANES) for j in range(RWKV_PAIRS)}
    n_it = range(len(items))
    a_t = [prep[d][c]["a_t"][:, pl_[j]] for d, c, j in items]
    r_t = [prep[d][c]["r_t"][:, pl_[j]] for d, c, j in items]
    v_w = [_bdiag(prep[d][c]["v"][:, pl_[j]], bd2) for d, c, j in items]
    w_p = [jnp.concatenate([jnp.where(bd2, prep[d][c]["bn_t"][pl_[j], :], 0.0),
                            jnp.where(bd2, prep[d][c]["kn_t"][pl_[j], :], 0.0)], axis=1) for d, c, j in items]
    p = [dot(jnp.concatenate([a_t[n], r_t[n]], axis=0), w_p[n]) for n in n_it]
    m_a = [jnp.where(strict4[items[n][0]], p[n][:lc], 0.0) for n in n_it]
    m_r = [jnp.where(incl4[items[n][0]], p[n][lc:], 0.0) for n in n_it]
    mv = [dot(m_a[n][:, LANES:], v_w[n]) for n in n_it]
    x = [m_a[n][:, :LANES] for n in n_it]
    t = [eye2 + x[n] for n in n_it]
    xs = [_split2(x[n]) for n in n_it]
    x = [_pair_mm([xs[n]], xs[n], bd2)[0] for n in n_it]
    for _ in range(int(math.log2(lc)) - 2):
        xs = [_split2(x[n]) for n in n_it]
        tx = [_pair_mm([_split2(t[n]), xs[n]], xs[n], bd2) for n in n_it]
        t = [t[n] + tx[n][0] for n in n_it]
        x = [tx[n][1] for n in n_it]
    t = [t[n] + _pair_mm([_split2(t[n])], _split2(x[n]), bd2)[0] for n in n_it]
    au = [dot(t[n], jnp.concatenate([_bdiag(a_t[n], bd2), _bdiag(mv[n], bd2)], axis=1)) for n in n_it]
    w2 = [jnp.concatenate([jnp.concatenate([_bdiag(au[n][:, :LANES], bd2), _bdiag(au[n][:, LANES:], bd2)], axis=1),
                           jnp.concatenate([zeros_w, v_w[n]], axis=1)], axis=0) for n in n_it]
    q_bk = [jnp.concatenate([_pair_rows(prep[d][c]["be_t"], j, low_half),
                             _pair_rows(prep[d][c]["ke_t"], j, low_half)], axis=1)
            for d, c, j in items]
    big = [dot(jnp.concatenate([q_bk[n], m_r[n]], axis=0), w2[n]) for n in n_it]
    wc = [jnp.where(low_half,
                    jnp.broadcast_to(prep[d][c]["wtot_col"][j * LANES:j * LANES + hd, :], (hd, LANES)),
                    jnp.broadcast_to(prep[d][c]["wtot_col"][j * LANES + hd:(j + 1) * LANES, :], (hd, LANES)))
          for d, c, j in items]

    idx = {it: n for n, it in enumerate(items)}
    heads = [(d, j) for d in (0, 1) for j in range(RWKV_PAIRS)]
    st = {dj: s_ref[dj[0], dj[1]] for dj in heads}
    for k in range(nch):
        sel = [(d, (k if d == 0 else nch - 1 - k), j) for d, j in heads]
        sy = [dot(jnp.concatenate([r_t[idx[it]] + big[idx[it]][lc:, :LANES], big[idx[it]][:lc, :LANES]], axis=0),
                  _bdiag(st[(it[0], it[2])], bd2)) for it in sel]
        for m, (d, c, j) in enumerate(sel):
            out_refs[d][0, c * lc:(c + 1) * lc, pl_[j]] = sy[m][:lc] + big[idx[(d, c, j)]][lc:, LANES:]
        for m, (d, c, j) in enumerate(sel):
            n = idx[(d, c, j)]
            st[(d, j)] = wc[n] * st[(d, j)] + sy[m][lc:] + big[n][:lc, LANES:]
    for d, j in heads:
        s_ref[d, j] = st[(d, j)]


def rwkv_scan(u_rwkv, mix, w0, w2, a0, a2, g2, k_k, k_a, r_k, ncx_tokens):
    b, ta, _ = u_rwkv.shape
    lc = RWKV_BLOCK
    nc = ta // lc
    ncx = ncx_tokens // lc
    nb = lc // SUBLANES
    nblk = ta // SUBLANES
    ni = RWKV_INNER
    kern = functools.partial(_rwkv_kernel, ncx=ncx, nc=nc)
    const2 = lambda bi, i: (0, 0)
    const3 = lambda bi, i: (0, 0, 0)
    fwd = lambda i: i
    bwd = lambda i: _chunk_of(1, i, ncx, nc)

    def u_specs(ch):
        return [pl.BlockSpec((1, lc, RWKV_IN), lambda bi, i: (bi, ch(i), 0)),
                pl.BlockSpec((1, SUBLANES, RWKV_IN), lambda bi, i: (bi, jnp.maximum(ch(i) * nb - 1, 0), 0)),
                pl.BlockSpec((1, SUBLANES, RWKV_IN), lambda bi, i: (bi, jnp.minimum((ch(i) + 1) * nb, nblk - 1), 0))]

    return pl.pallas_call(
        kern,
        out_shape=(jax.ShapeDtypeStruct((b, ta, 3 * ni), F32), jax.ShapeDtypeStruct((b, ta, 2 * ni), F32)),
        grid=(b, nc),
        in_specs=u_specs(fwd) + u_specs(bwd) + [
            pl.BlockSpec((1, RWKV_IN), const2),
            pl.BlockSpec((2, 1, ni), const3),
            pl.BlockSpec((2, RWKV_DECAY_LORA, ni), const3),
            pl.BlockSpec((2, 1, ni), const3),
            pl.BlockSpec((RWKV_AAA_LORA, ni), const2),
            pl.BlockSpec((RWKV_GATE_LORA, ni), const2),
            pl.BlockSpec((1, ni), const2),
            pl.BlockSpec((1, ni), const2),
            pl.BlockSpec((1, ni), const2),
        ],
        out_specs=(pl.BlockSpec((1, lc, 3 * ni), lambda bi, i: (bi, i, 0)),
                   pl.BlockSpec((1, lc, 2 * ni), lambda bi, i: (bi, bwd(i), 0))),
        scratch_shapes=[pltpu.VMEM((2, RWKV_PAIRS, RWKV_HEAD_DIM, LANES), F32)],
        compiler_params=_cparams(2),
        name="rwkv_scan",
    )(u_rwkv, u_rwkv, u_rwkv, u_rwkv, u_rwkv, u_rwkv, mix, w0, w2, a0, a2, g2, k_k, k_a, r_k)


def _merge_kernel(ssd_ref, z_ref, retf_ref, retb_ref, rg_ref, rwf_ref, rwb_ref, gate_ref, x_ref, mod_ref, nw_ref,
                  ssdnw_ref, lnw_ref, lnb_ref, wso_ref, wro_ref, wwo_ref, wo_ref, o_ref):
    ys = (ssd_ref[0, 0] + ssd_ref[1, 0]) * _silu(z_ref[0][:, :SSD_INNER])
    parts = []
    for g in range(SSD_GROUPS):
        yg = ys[:, g * SSD_GROUP_W:(g + 1) * SSD_GROUP_W]
        parts.append(yg * lax.rsqrt(jnp.mean(yg * yg, axis=-1, keepdims=True) + EPS))
    ys = jnp.concatenate(parts, axis=1) * ssdnw_ref[...]
    o_ssd = jnp.dot(ys.astype(BF16), wso_ref[...], preferred_element_type=F32)

    yr = retf_ref[0] + retb_ref[0]
    parts = []
    for h in range(RET_HEADS):
        yh = yr[:, h * RET_VAL_DIM:(h + 1) * RET_VAL_DIM]
        yc = yh - jnp.mean(yh, axis=-1, keepdims=True)
        parts.append(yc * lax.rsqrt(jnp.mean(yc * yc, axis=-1, keepdims=True) + EPS))
    yr = jnp.concatenate(parts, axis=1) * _silu(rg_ref[0])
    o_ret = jnp.dot(yr.astype(BF16), wro_ref[...], preferred_element_type=F32)

    ni = RWKV_INNER
    rw0 = rwf_ref[0]
    rw1 = rwb_ref[0]
    yw = rw0[:, :ni] + rw1[:, :ni]
    pair_ones = _pair_ones()
    inv_hd = 1.0 / RWKV_HEAD_DIM
    yc = yw - _head_sums([yw], pair_ones)[0] * inv_hd
    var = _head_sums([yc * yc], pair_ones)[0] * inv_hd
    yw = yc * lax.rsqrt(var + RWKV_LN_EPS) * lnw_ref[...] + lnb_ref[...]
    yw = (yw + rw0[:, ni:2 * ni] + rw1[:, ni:2 * ni]) * rw0[:, 2 * ni:3 * ni]
    o_rw = jnp.dot(yw.astype(BF16), wwo_ref[...], preferred_element_type=F32)

    gate = gate_ref[0]
    merged = (jax.nn.sigmoid(gate[:, :D_MODEL]) * o_ssd
              + jax.nn.sigmoid(gate[:, D_MODEL:2 * D_MODEL]) * o_ret
              + jax.nn.sigmoid(gate[:, 2 * D_MODEL:]) * o_rw)
    yx = jnp.dot(merged.astype(BF16), wo_ref[...], preferred_element_type=F32)
    g1 = mod_ref[0, 0, 2:3, :]
    o_ref[0] = x_ref[0] + g1 * _rms(yx, nw_ref[...])


def merge_out(ssd_y, u_zdt, ret_f, ret_b, u_ret, rw_f, rw_b, u_gate, x, mod, nw1, ssd_nw, ln_w, ln_b,
              w_ssd_out, w_ret_out, w_rwkv_out, w_out, n_ctx_tiles):
    b, ta, dm = x.shape
    tm = ROW_TILE
    row = lambda bi, i: (bi, i, 0)
    both = lambda bi, i: (0, bi, i, 0)
    const2 = lambda bi, i: (0, 0)
    return pl.pallas_call(
        _merge_kernel,
        out_shape=jax.ShapeDtypeStruct((b, ta, dm), F32),
        grid=(b, ta // tm),
        in_specs=[
            pl.BlockSpec((2, 1, tm, SSD_INNER), both),
            pl.BlockSpec((1, tm, u_zdt.shape[2]), row),
            pl.BlockSpec((1, tm, RET_INNER), row),
            pl.BlockSpec((1, tm, RET_INNER), row),
            pl.BlockSpec((1, tm, RET_INNER), lambda bi, i: (bi, i, 2)),
            pl.BlockSpec((1, tm, 3 * RWKV_INNER), row),
            pl.BlockSpec((1, tm, 2 * RWKV_INNER), row),
            pl.BlockSpec((1, tm, N_BRANCH * dm), row),
            pl.BlockSpec((1, tm, dm), row),
            pl.BlockSpec((1, 1, SUBLANES, dm), lambda bi, i: (bi, (i >= n_ctx_tiles).astype(jnp.int32), 0, 0)),
            pl.BlockSpec((1, dm), const2),
            pl.BlockSpec((1, SSD_INNER), const2),
            pl.BlockSpec((1, RWKV_INNER), const2),
            pl.BlockSpec((1, RWKV_INNER), const2),
            pl.BlockSpec((SSD_INNER, dm), const2),
            pl.BlockSpec((RET_INNER, dm), const2),
            pl.BlockSpec((RWKV_INNER, dm), const2),
            pl.BlockSpec((dm, dm), const2),
        ],
        out_specs=pl.BlockSpec((1, tm, dm), row),
        compiler_params=_cparams(2),
        name="merge_out",
    )(ssd_y, u_zdt, ret_f, ret_b, u_ret, rw_f, rw_b, u_gate, x, mod, nw1, ssd_nw, ln_w, ln_b,
      w_ssd_out, w_ret_out, w_rwkv_out, w_out)


def _mlp_kernel(x_ref, mod_ref, nw2_ref, nw3_ref, w1_ref, w2_ref, *rest, ff_tile, with_next):
    nb, tm, dm = x_ref.shape
    x = x_ref[...]
    mod = mod_ref[:, 0]
    y = _rms(x, nw2_ref[...])
    h = (y * (1.0 + mod[:, 4:5, :]) + mod[:, 3:4, :]).astype(BF16).reshape(nb * tm, dm)
    acc = jnp.zeros((nb * tm, dm), F32)
    for j in range(D_FF // ff_tile):
        hid = jnp.dot(h, w1_ref[:, j * ff_tile:(j + 1) * ff_tile], preferred_element_type=F32)
        hid = jnp.square(jnp.maximum(hid, 0.0)).astype(BF16)
        acc = acc + jnp.dot(hid, w2_ref[j * ff_tile:(j + 1) * ff_tile, :], preferred_element_type=F32)
    x_new = x + mod[:, 5:6, :] * _rms(acc.reshape(nb, tm, dm), nw3_ref[...])
    if with_next:
        nwn_ref, modn_ref, o_ref, h_ref = rest
        modn = modn_ref[:, 0]
        h_ref[...] = (_rms(x_new, nwn_ref[...]) * (1.0 + modn[:, 1:2, :]) + modn[:, 0:1, :]).astype(BF16)
    else:
        o_ref, = rest
    o_ref[...] = x_new


def mlp_block(x, mod, nw2, nw3, w1, w2, n_ctx_tiles, skip_tiles=0, next_norm=None, ff_tile=1024):
    b, ta, dm = x.shape
    tm = ROW_TILE
    rows = lambda i: (0, i, 0)
    const2 = lambda i: (0, 0)
    mod_spec = pl.BlockSpec((b, 1, SUBLANES, dm),
                            lambda i: (0, (i + skip_tiles >= n_ctx_tiles).astype(jnp.int32), 0, 0))
    resident = pl.Buffered(1)
    in_specs = [
        pl.BlockSpec((b, tm, dm), lambda i: (0, i + skip_tiles, 0)),
        mod_spec,
        pl.BlockSpec((1, dm), const2),
        pl.BlockSpec((1, dm), const2),
        pl.BlockSpec((dm, D_FF), const2, pipeline_mode=resident),
        pl.BlockSpec((D_FF, dm), const2, pipeline_mode=resident),
    ]
    args = [x, mod, nw2, nw3, w1, w2]
    out_rows = ta - skip_tiles * tm
    out_shape = jax.ShapeDtypeStruct((b, out_rows, dm), F32)
    out_specs = pl.BlockSpec((b, tm, dm), rows)
    if next_norm is not None:
        in_specs += [pl.BlockSpec((1, dm), const2), mod_spec]
        args += list(next_norm)
        out_shape = (out_shape, jax.ShapeDtypeStruct((b, out_rows, dm), BF16))
        out_specs = (out_specs, pl.BlockSpec((b, tm, dm), rows))
    return pl.pallas_call(
        functools.partial(_mlp_kernel, ff_tile=ff_tile, with_next=next_norm is not None),
        out_shape=out_shape,
        grid=(ta // tm - skip_tiles,),
        in_specs=in_specs,
        out_specs=out_specs,
        compiler_params=_cparams(1),
        name="mlp",
    )(*args)


def _rope_tables(n_ctx, n_lat):
    rows = n_lat // GRID_W
    row = np.repeat(np.arange(rows), GRID_W).astype(np.float32)
    col = np.tile(np.arange(GRID_W), rows).astype(np.float32)
    n_freq = RET_KEY_DIM // 4
    inv = jnp.power(ROPE_BASE, -jnp.arange(n_freq, dtype=F32) / n_freq)
    ang = jnp.concatenate([jnp.asarray(row)[:, None] * inv, jnp.asarray(col)[:, None] * inv], axis=-1)
    cos = jnp.concatenate([jnp.ones((n_ctx, RET_KEY_DIM // 2), F32), jnp.cos(ang)], axis=0)
    sin = jnp.concatenate([jnp.zeros((n_ctx, RET_KEY_DIM // 2), F32), jnp.sin(ang)], axis=0)
    cos_h = jnp.concatenate([cos, cos], axis=1)
    sin_h = jnp.concatenate([-sin, sin], axis=1)
    reps = 2 * RET_HEADS
    return jnp.tile(cos_h, (1, reps)), jnp.tile(sin_h, (1, reps))


def _pad_lanes(a, width=LANES):
    return jnp.pad(a, [(0, 0)] * (a.ndim - 1) + [(0, width - a.shape[-1])])


def kernel(x, c, ctx, c_ctx, norm_w, ada_w, ada_b, w_in, ssd_conv_w, ssd_conv_b, ssd_dt_bias, ssd_a_log,
           ssd_d, ssd_norm_w, ret_decay, rwkv_mix, rwkv_w0, rwkv_w2, rwkv_a0, rwkv_a2, rwkv_g2, rwkv_k_k,
           rwkv_k_a, rwkv_r_k, rwkv_lnx_w, rwkv_lnx_b, w_ssd_out, w_ret_out, w_rwkv_out, w_out, mlp_w1, mlp_w2):
    b, n_lat, dm = x.shape
    n_ctx = ctx.shape[1]
    ta = n_ctx + n_lat
    depth = norm_w.shape[0]
    n_ctx_tiles = n_ctx // ROW_TILE
    assert n_ctx % ROW_TILE == 0 and n_lat % ROW_TILE == 0 and b + 1 <= SUBLANES

    hexp = np.zeros((LANES, SSD_INNER), np.float32)
    for h in range(SSD_HEADS):
        hexp[h, h * SSD_HEAD_DIM:(h + 1) * SSD_HEAD_DIM] = 1.0
    hexp = jnp.asarray(hexp, BF16)
    cosx, sinx = _rope_tables(n_ctx, n_lat)

    cond = jnp.zeros((SUBLANES, dm), F32).at[:b].set(c).at[b].set(c_ctx)
    mod_all = modulation_all(cond, ada_w, ada_b)

    sizes = (N_BRANCH * dm, SSD_INNER, SSD_CONV_DIM, 2 * SSD_HEADS, RET_QK, RET_QK, RET_INNER, RET_INNER, RWKV_IN)
    offs = np.concatenate([[0], np.cumsum(sizes)])
    perm = np.concatenate([np.concatenate([np.arange(0, RET_KEY_DIM, 2), np.arange(1, RET_KEY_DIM, 2)]) + h * RET_KEY_DIM
                           for h in range(RET_HEADS)])

    xall = jnp.concatenate([ctx, x], axis=1)
    mods = []
    for l in range(depth):
        m = mod_all[l].reshape(SUBLANES, 6, dm)
        m = jnp.pad(m, ((0, 0), (0, SUBLANES - 6), (0, 0)))
        mods.append(jnp.stack([jnp.broadcast_to(m[b], (b,) + m.shape[1:]), m[:b]], axis=1))
    h = norm_modulate(xall, norm_w[0][0:1], mods[0], n_ctx_tiles)
    tm_mm = 1024 if (b * ta) % 1024 == 0 else ROW_TILE
    for l in range(depth):
        wl = w_in[l]
        seg = [wl[:, offs[j]:offs[j + 1]] for j in range(len(sizes))]
        w_gate = seg[0].astype(BF16)
        w_zdt = jnp.concatenate([seg[1], _pad_lanes(seg[3][:, :SSD_HEADS]), _pad_lanes(seg[3][:, SSD_HEADS:])],
                                axis=1).astype(BF16)
        w_xbc = seg[2].astype(BF16)
        w_ret = jnp.concatenate([seg[4][:, perm], seg[5][:, perm], seg[6], seg[7]], axis=1).astype(BF16)
        w_rw = seg[8].astype(BF16)

        mod = mods[l]
        nw = norm_w[l]
        h = h.reshape(b * ta, dm)
        u_gate = matmul(h, w_gate, tm_mm, 1024).reshape(b, ta, -1)
        u_zdt = matmul(h, w_zdt, tm_mm, w_zdt.shape[1]).reshape(b, ta, -1)
        u_xbc = matmul(h, w_xbc, tm_mm, SSD_CONV_DIM).reshape(b, ta, -1)
        u_ret = matmul(h, w_ret, tm_mm, w_ret.shape[1]).reshape(b, ta, -1)
        u_rw = matmul(h, w_rw, tm_mm, RWKV_IN).reshape(b, ta, -1)

        conv_w = jnp.pad(ssd_conv_w[l], ((0, SUBLANES - SSD_CONV), (0, 0)))
        dskip = jnp.repeat(ssd_d[l], SSD_HEAD_DIM)[None, :]
        ssd_y = ssd_scan(u_xbc, u_zdt, conv_w, ssd_conv_b[l][None, :],
                         _pad_lanes(ssd_dt_bias[l])[:, None, :], _pad_lanes(ssd_a_log[l])[:, None, :],
                         dskip, hexp, n_ctx)
        ret_f, ret_b = ret_scan(u_ret, cosx, sinx, _pad_lanes(ret_decay[l])[:, None, :], n_ctx)
        rw_f, rw_b = rwkv_scan(u_rw, rwkv_mix[l][None, :], rwkv_w0[l][:, None, :], rwkv_w2[l], rwkv_a0[l][:, None, :],
                         rwkv_a2[l], rwkv_g2[l].astype(BF16), rwkv_k_k[l][None, :], rwkv_k_a[l][None, :],
                         rwkv_r_k[l].reshape(1, RWKV_INNER), n_ctx)

        xall = merge_out(ssd_y, u_zdt, ret_f, ret_b, u_ret, rw_f, rw_b, u_gate, xall, mod, nw[1:2],
                         ssd_norm_w[l][None, :], rwkv_lnx_w[l][None, :], rwkv_lnx_b[l][None, :],
                         w_ssd_out[l].astype(BF16), w_ret_out[l].astype(BF16), w_rwkv_out[l].astype(BF16),
                         w_out[l].astype(BF16), n_ctx_tiles)
        w1, w2 = mlp_w1[l].astype(BF16), mlp_w2[l].astype(BF16)
        if l + 1 < depth:
            xall, h = mlp_block(xall, mod, nw[2:3], nw[3:4], w1, w2, n_ctx_tiles,
                                next_norm=(norm_w[l + 1][0:1], mods[l + 1]))
        else:
            xall = mlp_block(xall, mod, nw[2:3], nw[3:4], w1, w2, n_ctx_tiles, skip_tiles=n_ctx_tiles)
    return xall
```

```python
import functools
import math

import numpy as np
import jax
import jax.numpy as jnp
from jax import lax
from jax.experimental import pallas as pl
from jax.experimental.pallas import tpu as pltpu

F32 = jnp.float32
BF16 = jnp.bfloat16
HIGHEST = lax.Precision.HIGHEST

D_MODEL = 1024
GRID_W = 64
EPS = 1e-6
N_BRANCH = 3

SSD_HEADS = 16
SSD_HEAD_DIM = 64
SSD_INNER = SSD_HEADS * SSD_HEAD_DIM
SSD_GROUPS = 2
SSD_STATE = 128
SSD_CONV = 5
SSD_CHUNK = 128
SSD_BLOCK = 2 * SSD_CHUNK
SSD_CONV_DIM = SSD_INNER + 2 * SSD_GROUPS * SSD_STATE
SSD_GROUP_W = SSD_INNER // SSD_GROUPS

RET_HEADS = 4
RET_KEY_DIM = 64
RET_VAL_DIM = 128
RET_QK = RET_HEADS * RET_KEY_DIM
RET_INNER = RET_HEADS * RET_VAL_DIM
RET_CHUNK = 128
RET_BLOCK = 2 * RET_CHUNK
ROPE_BASE = 10000.0

RWKV_HEADS = 8
RWKV_HEAD_DIM = 64
RWKV_INNER = RWKV_HEADS * RWKV_HEAD_DIM
RWKV_DECAY_LORA = 64
RWKV_AAA_LORA = 64
RWKV_GATE_LORA = 128
RWKV_IN = 3 * RWKV_INNER + RWKV_DECAY_LORA + RWKV_AAA_LORA + RWKV_GATE_LORA
RWKV_LN_EPS = 64e-5
RWKV_CHUNK = 64

D_FF = 4 * D_MODEL

SUBLANES = 8
LANES = 128
VMEM_LIMIT_BYTES = 56 * 1024 * 1024

ROW_TILE = 256
NEG_BIG = -1e30


def _cparams(n_axes):
    return pltpu.CompilerParams(dimension_semantics=("arbitrary",) * n_axes,
                                vmem_limit_bytes=VMEM_LIMIT_BYTES)


def _silu(x):
    return x * jax.nn.sigmoid(x)


def _softplus(x):
    return jnp.maximum(x, 0.0) + jnp.log1p(jnp.exp(-jnp.abs(x)))


def _split3(x):
    x1 = x.astype(BF16)
    r1 = x - x1.astype(F32)
    x2 = r1.astype(BF16)
    r2 = r1 - x2.astype(F32)
    return x1, x2, r2.astype(BF16)


def _dot_nt(a, b, precision=None):
    return lax.dot_general(a, b, (((1,), (1,)), ((), ())), preferred_element_type=F32, precision=precision)


def _dot_tn(a, b, precision=None):
    return lax.dot_general(a, b, (((0,), (0,)), ((), ())), preferred_element_type=F32, precision=precision)


def _chunk_of(d, i, ncx, nc):
    bwd = jnp.where(i < ncx, ncx - 1 - i, ncx + nc - 1 - i)
    return jnp.where(d == 0, i, bwd)


def _seg_edges(c, ncx, nc):
    seg_start = jnp.logical_or(c == 0, c == ncx)
    seg_end = jnp.logical_or(c == ncx - 1, c == nc - 1)
    return seg_start, seg_end


def _tri_mask(n, d, strict, reps=1):
    row = lax.broadcasted_iota(jnp.int32, (n, n * reps), 0)
    col = lax.broadcasted_iota(jnp.int32, (n, n * reps), 1) % n
    lead = jnp.where(d == 0, row - col, col - row)
    return lead > 0 if strict else lead >= 0


def _mod_kernel(c_ref, w_ref, b_ref, o_ref):
    s = _silu(c_ref[...])
    o_ref[0] = jnp.dot(s, w_ref[0], preferred_element_type=F32, precision=HIGHEST) + b_ref[0]


def modulation_all(cond, ada_w, ada_b):
    depth = ada_w.shape[0]
    tn = 1536
    return pl.pallas_call(
        _mod_kernel,
        out_shape=jax.ShapeDtypeStruct((depth, SUBLANES, 6 * D_MODEL), F32),
        grid=(depth, 6 * D_MODEL // tn),
        in_specs=[pl.BlockSpec((SUBLANES, D_MODEL), lambda l, j: (0, 0)),
                  pl.BlockSpec((1, D_MODEL, tn), lambda l, j: (l, 0, j)),
                  pl.BlockSpec((1, 1, tn), lambda l, j: (l, 0, j))],
        out_specs=pl.BlockSpec((1, SUBLANES, tn), lambda l, j: (l, 0, j)),
        compiler_params=_cparams(2),
        name="modulation",
    )(cond, ada_w, ada_b.reshape(depth, 1, 6 * D_MODEL))


def _rms(x, w):
    return x * lax.rsqrt(jnp.mean(x * x, axis=-1, keepdims=True) + EPS) * w


def _normmod_kernel(x_ref, nw_ref, mod_ref, h_ref):
    y = _rms(x_ref[0], nw_ref[...])
    shift = mod_ref[0, 0, 0:1, :]
    scale = mod_ref[0, 0, 1:2, :]
    h_ref[0] = (y * (1.0 + scale) + shift).astype(BF16)


def norm_modulate(x, nw, mod, n_ctx_tiles):
    b, ta, dm = x.shape
    return pl.pallas_call(
        _normmod_kernel,
        out_shape=jax.ShapeDtypeStruct((b, ta, dm), BF16),
        grid=(b, ta // ROW_TILE),
        in_specs=[pl.BlockSpec((1, ROW_TILE, dm), lambda bi, i: (bi, i, 0)),
                  pl.BlockSpec((1, dm), lambda bi, i: (0, 0)),
                  pl.BlockSpec((1, 1, SUBLANES, dm), lambda bi, i: (bi, (i >= n_ctx_tiles).astype(jnp.int32), 0, 0))],
        out_specs=pl.BlockSpec((1, ROW_TILE, dm), lambda bi, i: (bi, i, 0)),
        compiler_params=_cparams(2),
        name="norm_modulate",
    )(x, nw, mod)


def _mm_kernel(a_ref, w_ref, o_ref):
    o_ref[...] = jnp.dot(a_ref[...], w_ref[...], preferred_element_type=F32).astype(o_ref.dtype)


def matmul(a, w, tm, tn, out_dtype=F32):
    r, k = a.shape
    n = w.shape[1]
    assert r % tm == 0 and n % tn == 0
    return pl.pallas_call(
        _mm_kernel,
        out_shape=jax.ShapeDtypeStruct((r, n), out_dtype),
        grid=(n // tn, r // tm),
        in_specs=[pl.BlockSpec((tm, k), lambda j, i: (i, 0)),
                  pl.BlockSpec((k, tn), lambda j, i: (0, j))],
        out_specs=pl.BlockSpec((tm, tn), lambda j, i: (i, j)),
        compiler_params=_cparams(2),
        name="in_proj",
    )(a, w)


def _ssd_kernel(xbc_ref, xp_ref, xn_ref, dt_ref, cw_ref, cb_ref, dtb_ref, alog_ref, dskip_ref,
                hexp_ref, y_ref, s_ref, ext_ref, act_ref, *, ncx, nc):
    d = pl.program_id(1)
    i = pl.program_id(2)
    c = _chunk_of(d, i, ncx, nc)
    seg_start, seg_end = _seg_edges(c, ncx, nc)
    lc = SSD_CHUNK
    lb = SSD_BLOCK
    nch = lb // lc
    hb = SUBLANES

    @pl.when(i == 0)
    def _():
        s_ref[...] = jnp.zeros_like(s_ref)

    @pl.when(d == 0)
    def _():
        ext_ref[0:hb, :] = jnp.where(seg_start, 0.0, xp_ref[0])
        ext_ref[hb:hb + lb, :] = xbc_ref[0]
        ext_ref[hb + lb:hb + lb + hb, :] = jnp.where(seg_end, 0.0, xn_ref[0])
        pad = SSD_CONV // 2
        acc = ext_ref[hb - pad:hb - pad + lb, :] * cw_ref[0:1, :]
        for j in range(1, SSD_CONV):
            acc = acc + ext_ref[hb - pad + j:hb - pad + j + lb, :] * cw_ref[j:j + 1, :]
        act = _silu(acc + cb_ref[...])
        act_ref[c] = act.astype(BF16)
        y_ref[0, 0] = act[:, :SSD_INNER] * dskip_ref[...]

    @pl.when(d != 0)
    def _():
        y_ref[0, 0] = jnp.zeros(y_ref.shape[2:], F32)

    tri = _tri_mask(lc, d, strict=False)
    hg = SSD_HEADS // SSD_GROUPS
    low_half = lax.broadcasted_iota(jnp.int32, (lc, 2 * SSD_HEAD_DIM), 1) < SSD_HEAD_DIM
    neg_a = -jnp.exp(alog_ref[0])

    local = []
    for k in range(nch):
        rs = pl.ds(pl.multiple_of(jnp.where(d == 0, k, nch - 1 - k) * lc, lc), lc)
        xbc = act_ref[c, rs, :]
        bm = xbc[:, SSD_INNER:SSD_INNER + SSD_GROUPS * SSD_STATE]
        cm = xbc[:, SSD_INNER + SSD_GROUPS * SSD_STATE:]
        dt = _softplus(dt_ref[0, rs, :] + dtb_ref[0])
        la = dt * neg_a
        ac3 = jnp.dot(tri.astype(BF16), jnp.concatenate(_split3(la), axis=1), preferred_element_type=F32)
        acum = ac3[:, :LANES] + ac3[:, LANES:2 * LANES] + ac3[:, 2 * LANES:]
        atot = jnp.sum(la, axis=0, keepdims=True)
        acum_t = acum.T
        etot8 = jnp.broadcast_to(jnp.exp(atot), (SUBLANES, LANES))
        parts = jnp.concatenate(_split2(jnp.exp(acum)) + _split2(etot8), axis=0)
        ex = jnp.dot(parts, hexp_ref[...], preferred_element_type=F32)
        eacx = ex[0:lc] + ex[lc:2 * lc]
        etotx = (ex[2 * lc:2 * lc + SUBLANES] + ex[2 * lc + SUBLANES:])[0:1, :]
        parts_b = jnp.concatenate([dt.astype(BF16), jnp.exp(atot - acum).astype(BF16)], axis=0)
        ex_b = jnp.dot(parts_b, hexp_ref[...], preferred_element_type=F32).astype(BF16)
        vb = xbc[:, :SSD_INNER] * ex_b[0:lc]
        vend = vb * ex_b[lc:2 * lc]
        groups = []
        for g in range(SSD_GROUPS):
            gs = slice(g * SSD_STATE, (g + 1) * SSD_STATE)
            gw = slice(g * SSD_GROUP_W, (g + 1) * SSD_GROUP_W)
            cg = cm[:, gs]
            bg = bm[:, gs]
            scores = _dot_nt(cg, bg)
            y_in = []
            for hp in range(hg // 2):
                a_pair = []
                for h in (g * hg + 2 * hp, g * hg + 2 * hp + 1):
                    diff = acum[:, h:h + 1] - acum_t[h:h + 1, :]
                    a_pair.append((scores * jnp.exp(jnp.where(tri, diff, NEG_BIG))).astype(BF16))
                ps = slice((g * hg + 2 * hp) * SSD_HEAD_DIM, (g * hg + 2 * hp + 2) * SSD_HEAD_DIM)
                v_pair = vb[:, ps]
                w_pair = jnp.concatenate([jnp.where(low_half, v_pair, 0.0), jnp.where(low_half, 0.0, v_pair)],
                                         axis=0)
                y_in.append(jnp.dot(jnp.concatenate(a_pair, axis=1), w_pair, preferred_element_type=F32))
            groups.append(dict(cg=cg, y_in=jnp.concatenate(y_in, axis=1), eac=eacx[:, gw], etot=etotx[:, gw],
                               cs=_dot_tn(bg, vend[:, gw])))
        local.append((rs, groups))

    for g in range(SSD_GROUPS):
        gw = slice(g * SSD_GROUP_W, (g + 1) * SSD_GROUP_W)
        st = s_ref[g]
        for rs, groups in local:
            q = groups[g]
            y_ref[0, 0, rs, gw] += q["y_in"] + jnp.dot(q["cg"], st.astype(BF16),
                                                       preferred_element_type=F32) * q["eac"]
            st = st * q["etot"] + q["cs"]
        s_ref[g] = st


def ssd_scan(u_xbc, u_zdt, conv_w, conv_b, dt_bias, a_log, dskip, hexp, ncx_tokens):
    b, ta, _ = u_xbc.shape
    lc = SSD_BLOCK
    nc = ta // lc
    ncx = ncx_tokens // lc
    nb = lc // SUBLANES
    nblk = ta // SUBLANES
    ch = functools.partial(_chunk_of, ncx=ncx, nc=nc)
    kern = functools.partial(_ssd_kernel, ncx=ncx, nc=nc)
    zdt_blk0 = SSD_INNER // LANES
    return pl.pallas_call(
        kern,
        out_shape=jax.ShapeDtypeStruct((2, b, ta, SSD_INNER), F32),
        grid=(b, 2, nc),
        in_specs=[
            pl.BlockSpec((1, lc, SSD_CONV_DIM), lambda bi, d, i: (bi, i * (1 - d), 0)),
            pl.BlockSpec((1, SUBLANES, SSD_CONV_DIM), lambda bi, d, i: (bi, jnp.maximum(i * nb - 1, 0) * (1 - d), 0)),
            pl.BlockSpec((1, SUBLANES, SSD_CONV_DIM),
                         lambda bi, d, i: (bi, jnp.minimum((i + 1) * nb, nblk - 1) * (1 - d), 0)),
            pl.BlockSpec((1, lc, LANES), lambda bi, d, i: (bi, ch(d, i), zdt_blk0 + d)),
            pl.BlockSpec((SUBLANES, SSD_CONV_DIM), lambda bi, d, i: (0, 0)),
            pl.BlockSpec((1, SSD_CONV_DIM), lambda bi, d, i: (0, 0)),
            pl.BlockSpec((1, 1, LANES), lambda bi, d, i: (d, 0, 0)),
            pl.BlockSpec((1, 1, LANES), lambda bi, d, i: (d, 0, 0)),
            pl.BlockSpec((1, SSD_INNER), lambda bi, d, i: (0, 0)),
            pl.BlockSpec((LANES, SSD_INNER), lambda bi, d, i: (0, 0)),
        ],
        out_specs=pl.BlockSpec((1, 1, lc, SSD_INNER), lambda bi, d, i: (d, bi, ch(d, i), 0)),
        scratch_shapes=[pltpu.VMEM((SSD_GROUPS, SSD_STATE, SSD_GROUP_W), F32),
                        pltpu.VMEM((lc + 2 * SUBLANES, SSD_CONV_DIM), F32),
                        pltpu.VMEM((nc, lc, SSD_CONV_DIM), BF16)],
        compiler_params=_cparams(3),
        name="ssd_scan",
    )(u_xbc, u_xbc, u_xbc, u_zdt, conv_w, conv_b, dt_bias, a_log, dskip, hexp)


def _ret_kernel(qkf_ref, vf_ref, cosf_ref, sinf_ref, qkb_ref, vb_ref, cosb_ref, sinb_ref, dec_ref,
                yf_ref, yb_ref, s_ref):
    i = pl.program_id(1)
    lc = RET_CHUNK
    lb = RET_BLOCK
    nch = lb // lc

    @pl.when(i == 0)
    def _():
        s_ref[...] = jnp.zeros_like(s_ref)

    in_refs = ((qkf_ref, vf_ref, cosf_ref, sinf_ref), (qkb_ref, vb_ref, cosb_ref, sinb_ref))
    out_refs = (yf_ref, yb_ref)
    half = RET_KEY_DIM // 2
    width = 2 * RET_QK
    lane = lax.broadcasted_iota(jnp.int32, (lb, width), 1)
    row = lax.broadcasted_iota(jnp.int32, (lc, lc), 0)
    col = lax.broadcasted_iota(jnp.int32, (lc, lc), 1)
    dist = jnp.abs(row - col).astype(F32)
    tri = (col <= row, col >= row)
    t_idx = lax.broadcasted_iota(jnp.int32, (lc, RET_VAL_DIM), 0).astype(F32)
    pos = (t_idx, lc - 1.0 - t_idx)
    q, k_t, v, lg = [], [], [], []
    for d in (0, 1):
        qk_ref, v_ref, cos_ref, sin_ref = in_refs[d]
        qk = qk_ref[0]
        swapped = jnp.where((lane % RET_KEY_DIM) < half,
                            pltpu.roll(qk, width - half, axis=1), pltpu.roll(qk, half, axis=1))
        qk = qk * cos_ref[...] + swapped * sin_ref[...]
        q.append(qk[:, :RET_QK].astype(BF16))
        k_t.append((qk[:, RET_QK:] * (RET_KEY_DIM ** -0.5)).T.astype(BF16))
        v.append(v_ref[0])
        lg.append(-_softplus(-dec_ref[d]))

    items = [(d, c, h) for d in (0, 1) for c in range(nch) for h in range(RET_HEADS)]
    n_it = range(len(items))
    idx = {it: n for n, it in enumerate(items)}
    rs = [slice(c * lc, (c + 1) * lc) for d, c, h in items]
    ks = [slice(h * RET_KEY_DIM, (h + 1) * RET_KEY_DIM) for d, c, h in items]
    vs = [slice(h * RET_VAL_DIM, (h + 1) * RET_VAL_DIM) for d, c, h in items]
    lgh = [lg[d][:, h:h + 1] for d, c, h in items]
    qh = [q[d][rs[n], ks[n]] for n, (d, c, h) in enumerate(items)]
    kh_t = [k_t[d][ks[n], rs[n]] for n, (d, c, h) in enumerate(items)]
    vh = [v[d][rs[n], vs[n]] for n, (d, c, h) in enumerate(items)]
    scores = [(jnp.dot(qh[n], kh_t[n], preferred_element_type=F32)
               * jnp.exp(jnp.where(tri[items[n][0]], dist * lgh[n], NEG_BIG))).astype(BF16) for n in n_it]
    y_in = [jnp.dot(scores[n], vh[n].astype(BF16), preferred_element_type=F32) for n in n_it]
    vend = [(vh[n] * jnp.exp((lc - 1.0 - pos[items[n][0]]) * lgh[n])).astype(BF16) for n in n_it]
    cs = [jnp.dot(kh_t[n], vend[n], preferred_element_type=F32) for n in n_it]
    heads = [(d, h) for d in (0, 1) for h in range(RET_HEADS)]
    st = {dh: s_ref[dh[0], dh[1]] for dh in heads}
    for k in range(nch):
        sel = [idx[(d, (k if d == 0 else nch - 1 - k), h)] for d, h in heads]
        y_st = [jnp.dot(qh[n], st[(items[n][0], items[n][2])].astype(BF16), preferred_element_type=F32)
                for n in sel]
        for m, n in enumerate(sel):
            d, c, h = items[n]
            out_refs[d][0, rs[n], vs[n]] = y_in[n] + y_st[m] * jnp.exp((pos[d] + 1.0) * lgh[n])
        for n in sel:
            d, c, h = items[n]
            st[(d, h)] = st[(d, h)] * jnp.exp(lc * lgh[n]) + cs[n]
    for d, h in heads:
        s_ref[d, h] = st[(d, h)]


def ret_scan(u_ret, cosx, sinx, ret_decay, ncx_tokens):
    b, ta, _ = u_ret.shape
    lc = RET_BLOCK
    nc = ta // lc
    ncx = ncx_tokens // lc
    fwd = lambda i: i
    bwd = lambda i: _chunk_of(1, i, ncx, nc)

    def specs(ch):
        return [pl.BlockSpec((1, lc, 2 * RET_QK), lambda bi, i: (bi, ch(i), 0)),
                pl.BlockSpec((1, lc, RET_INNER), lambda bi, i: (bi, ch(i), 1)),
                pl.BlockSpec((lc, 2 * RET_QK), lambda bi, i: (ch(i), 0)),
                pl.BlockSpec((lc, 2 * RET_QK), lambda bi, i: (ch(i), 0))]

    return pl.pallas_call(
        _ret_kernel,
        out_shape=(jax.ShapeDtypeStruct((b, ta, RET_INNER), F32), jax.ShapeDtypeStruct((b, ta, RET_INNER), F32)),
        grid=(b, nc),
        in_specs=specs(fwd) + specs(bwd) + [pl.BlockSpec((2, 1, LANES), lambda bi, i: (0, 0, 0))],
        out_specs=(pl.BlockSpec((1, lc, RET_INNER), lambda bi, i: (bi, i, 0)),
                   pl.BlockSpec((1, lc, RET_INNER), lambda bi, i: (bi, bwd(i), 0))),
        scratch_shapes=[pltpu.VMEM((2, RET_HEADS, RET_KEY_DIM, RET_VAL_DIM), F32)],
        compiler_params=_cparams(2),
        name="ret_scan",
    )(u_ret, u_ret, cosx, sinx, u_ret, u_ret, cosx, sinx, ret_decay)


RWKV_PAIRS = RWKV_HEADS // 2
RWKV_BLOCK = 4 * RWKV_CHUNK


def _bdiag(tile, bd2):
    return jnp.where(bd2, jnp.concatenate([tile, tile], axis=0), 0.0)


def _split2(x):
    hi = x.astype(BF16)
    return hi, (x - hi.astype(F32)).astype(BF16)


def _head_sums(xs, pair_ones):
    rows = xs[0].shape[0]
    parts = jnp.concatenate([p for x in xs for p in _split2(x)], axis=0)
    blk = parts.shape[0]
    n_pair = parts.shape[1] // LANES
    stacked = jnp.concatenate([parts[:, j * LANES:(j + 1) * LANES] for j in range(n_pair)], axis=0)
    y = jnp.dot(stacked, pair_ones, preferred_element_type=F32)
    full = jnp.concatenate([y[j * blk:(j + 1) * blk] for j in range(n_pair)], axis=1)
    return [full[2 * i * rows:(2 * i + 1) * rows] + full[(2 * i + 1) * rows:(2 * i + 2) * rows]
            for i in range(len(xs))]


def _pair_ones():
    hd = RWKV_HEAD_DIM
    return jnp.where((lax.broadcasted_iota(jnp.int32, (LANES, LANES), 0) < hd)
                     == (lax.broadcasted_iota(jnp.int32, (LANES, LANES), 1) < hd), 1.0, 0.0).astype(BF16)


def _dot3(a, b):
    a1, a2 = _split2(a)
    b1, b2 = _split2(b)
    n = a.shape[0]
    o = jnp.dot(jnp.concatenate([a1, a2], axis=0), b1, preferred_element_type=F32)
    return o[:n] + o[n:] + jnp.dot(a1, b2, preferred_element_type=F32)


def _pair_mm(lhs_splits, w_split, bd2):
    hi = jnp.concatenate([s[0] for s in lhs_splits], axis=0)
    lo = jnp.concatenate([s[1] for s in lhs_splits], axis=0)
    n = hi.shape[0]
    o = jnp.dot(jnp.concatenate([hi, lo], axis=0), _bdiag(w_split[0], bd2), preferred_element_type=F32)
    tot = o[:n] + o[n:] + jnp.dot(hi, _bdiag(w_split[1], bd2), preferred_element_type=F32)
    return [tot[i * RWKV_HEAD_DIM:(i + 1) * RWKV_HEAD_DIM] for i in range(len(lhs_splits))]


def _pair_rows(x_t, j, low_half):
    base = j * LANES
    return jnp.where(low_half, x_t[base:base + RWKV_HEAD_DIM], x_t[base + RWKV_HEAD_DIM:base + LANES])


def _rwkv_prep(d, u, prow, nrow, mix, w0, w2, a0, a2, kkw, ka, rk, bd):
    lc = RWKV_CHUNK
    lb = u.shape[0]
    ni = RWKV_INNER
    half_mix = 0.5 * mix
    nbr = pltpu.roll(u, 1, axis=0) + pltpu.roll(u, lb - 1, axis=0)
    sub = lax.broadcasted_iota(jnp.int32, (SUBLANES, 1), 0)
    fix_first = jnp.where(sub == 0, prow - u[lb - 1:lb, :], 0.0)
    fix_last = jnp.where(sub == SUBLANES - 1, nrow - u[0:1, :], 0.0)
    nbr = jnp.concatenate([nbr[:SUBLANES] + fix_first, nbr[SUBLANES:lb - SUBLANES],
                           nbr[lb - SUBLANES:] + fix_last], axis=0)
    u = (1.0 - mix) * u + half_mix * nbr
    r, k, v = u[:, 0:ni], u[:, ni:2 * ni], u[:, 2 * ni:3 * ni]
    o = 3 * ni
    w_lo = u[:, o:o + RWKV_DECAY_LORA]
    a_lo = u[:, o + RWKV_DECAY_LORA:o + RWKV_DECAY_LORA + RWKV_AAA_LORA]
    g_lo = u[:, o + RWKV_DECAY_LORA + RWKV_AAA_LORA:]

    logw = -math.exp(-0.5) * jax.nn.sigmoid(w0 + _dot3(jnp.tanh(w_lo), w2))
    a_gate = jax.nn.sigmoid(a0 + _dot3(a_lo, a2))
    kk = k * kkw
    kd = k * (1.0 + (a_gate - 1.0) * ka)
    kk_ss, rk_sum = _head_sums([kk * kk, r * kd * rk], bd)
    kk = kk / jnp.maximum(jnp.sqrt(kk_ss), 1e-12)
    bvec = kk * a_gate
    bonus = rk_sum * v

    row = lax.broadcasted_iota(jnp.int32, (lb, lb), 0)
    col = lax.broadcasted_iota(jnp.int32, (lb, lb), 1)
    lead = (row - col) if d == 0 else (col - row)
    incl = jnp.where(row // lc == col // lc, lead, -1) >= 0
    l1, l2, l3 = _split3(logw)
    cw3 = jnp.dot(incl.astype(BF16), jnp.concatenate([l1, l2, l3], axis=1), preferred_element_type=F32)
    cw = cw3[:, 0:ni] + cw3[:, ni:2 * ni] + cw3[:, 2 * ni:3 * ni]

    def dup_t(x):
        return jnp.concatenate([x, x], axis=0).T

    chunks = []
    for c in range(lb // lc):
        rs = slice(c * lc, (c + 1) * lc)
        logw_c, cw_c = logw[rs], cw[rs]
        ctot = jnp.sum(logw_c, axis=0, keepdims=True)
        e_neg = jnp.exp(-cw_c)
        e_end = jnp.exp(ctot - cw_c)
        chunks.append(dict(
            a_t=-kk[rs] * jnp.exp(cw_c - logw_c), r_t=r[rs] * jnp.exp(cw_c), v=v[rs],
            bn_t=dup_t(bvec[rs] * e_neg), kn_t=dup_t(kd[rs] * e_neg),
            be_t=dup_t(bvec[rs] * e_end), ke_t=dup_t(kd[rs] * e_end),
            wtot_col=jnp.exp(jnp.sum(logw_c.T, axis=1, keepdims=True))))
    return chunks, bonus, g_lo


def _rwkv_kernel(uf_ref, ufp_ref, ufn_ref, ub_ref, ubp_ref, ubn_ref, mix_ref, w0_ref, w2_ref, a0_ref, a2_ref,
                 g2_ref, kk_ref, ka_ref, rk_ref, of_ref, ob_ref, s_ref, *, ncx, nc):
    i = pl.program_id(1)
    lc = RWKV_CHUNK
    ni = RWKV_INNER
    hd = RWKV_HEAD_DIM
    chunk = (i, _chunk_of(1, i, ncx, nc))

    @pl.when(i == 0)
    def _():
        s_ref[...] = jnp.zeros_like(s_ref)

    bd2 = ((lax.broadcasted_iota(jnp.int32, (LANES, LANES), 0) < hd)
           == (lax.broadcasted_iota(jnp.int32, (LANES, LANES), 1) < hd))
    pair_ones = jnp.where(bd2, 1.0, 0.0).astype(BF16)
    u_refs = ((uf_ref, ufp_ref, ufn_ref), (ub_ref, ubp_ref, ubn_ref))
    out_refs = (of_ref, ob_ref)
    prep = []
    for d in (0, 1):
        seg_start, seg_end = _seg_edges(chunk[d], ncx, nc)
        u_ref, up_ref, un_ref = u_refs[d]
        prow = jnp.where(seg_start, 0.0, up_ref[0, SUBLANES - 1:SUBLANES, :])
        nrow = jnp.where(seg_end, 0.0, un_ref[0, 0:1, :])
        chunks, bonus, g_lo = _rwkv_prep(d, u_ref[0], prow, nrow, mix_ref[...], w0_ref[d], w2_ref[d], a0_ref[d],
                                         a2_ref[...], kk_ref[...], ka_ref[...], rk_ref[...], pair_ones)
        out_refs[d][0, :, ni:2 * ni] = bonus
        if d == 0:
            of_ref[0, :, 2 * ni:3 * ni] = jnp.dot(jax.nn.sigmoid(g_lo).astype(BF16), g2_ref[...],
                                                  preferred_element_type=F32)
        prep.append(chunks)

    row4 = lax.broadcasted_iota(jnp.int32, (lc, 2 * LANES), 0)
    col4 = lax.broadcasted_iota(jnp.int32, (lc, 2 * LANES), 1) % hd
    strict4 = (col4 < row4, col4 > row4)
    incl4 = (col4 <= row4, col4 >= row4)
    eye2 = (lax.broadcasted_iota(jnp.int32, (lc, LANES), 1) % hd
            == lax.broadcasted_iota(jnp.int32, (lc, LANES), 0)).astype(F32)
    low_half = lax.broadcasted_iota(jnp.int32, (hd, LANES), 1) < hd
    zeros_w = jnp.zeros((LANES, LANES), F32)

    def dot(a, b):
        return jnp.dot(a.astype(BF16), b.astype(BF16), preferred_element_type=F32)

    nch = RWKV_BLOCK // lc
    items = [(d, c, j) for d in (0, 1) for c in range(nch) for j in range(RWKV_PAIRS)]
    pl_ = {j: slice(j * LANES, (j + 1) * LANES) for j in range(RWKV_PAIRS)}
    n_it = range(len(items))
    a_t = [prep[d][c]["a_t"][:, pl_[j]] for d, c, j in items]
    r_t = [prep[d][c]["r_t"][:, pl_[j]] for d, c, j in items]
    v_w = [_bdiag(prep[d][c]["v"][:, pl_[j]], bd2) for d, c, j in items]
    w_p = [jnp.concatenate([jnp.where(bd2, prep[d][c]["bn_t"][pl_[j], :], 0.0),
                            jnp.where(bd2, prep[d][c]["kn_t"][pl_[j], :], 0.0)], axis=1) for d, c, j in items]
    p = [dot(jnp.concatenate([a_t[n], r_t[n]], axis=0), w_p[n]) for n in n_it]
    m_a = [jnp.where(strict4[items[n][0]], p[n][:lc], 0.0) for n in n_it]
    m_r = [jnp.where(incl4[items[n][0]], p[n][lc:], 0.0) for n in n_it]
    mv = [dot(m_a[n][:, LANES:], v_w[n]) for n in n_it]
    x = [m_a[n][:, :LANES] for n in n_it]
    t = [eye2 + x[n] for n in n_it]
    xs = [_split2(x[n]) for n in n_it]
    x = [_pair_mm([xs[n]], xs[n], bd2)[0] for n in n_it]
    for _ in range(int(math.log2(lc)) - 2):
        xs = [_split2(x[n]) for n in n_it]
        tx = [_pair_mm([_split2(t[n]), xs[n]], xs[n], bd2) for n in n_it]
        t = [t[n] + tx[n][0] for n in n_it]
        x = [tx[n][1] for n in n_it]
    t = [t[n] + _pair_mm([_split2(t[n])], _split2(x[n]), bd2)[0] for n in n_it]
    au = [dot(t[n], jnp.concatenate([_bdiag(a_t[n], bd2), _bdiag(mv[n], bd2)], axis=1)) for n in n_it]
    w2 = [jnp.concatenate([jnp.concatenate([_bdiag(au[n][:, :LANES], bd2), _bdiag(au[n][:, LANES:], bd2)], axis=1),
                           jnp.concatenate([zeros_w, v_w[n]], axis=1)], axis=0) for n in n_it]
    q_bk = [jnp.concatenate([_pair_rows(prep[d][c]["be_t"], j, low_half),
                             _pair_rows(prep[d][c]["ke_t"], j, low_half)], axis=1)
            for d, c, j in items]
    big = [dot(jnp.concatenate([q_bk[n], m_r[n]], axis=0), w2[n]) for n in n_it]
    wc = [jnp.where(low_half,
                    jnp.broadcast_to(prep[d][c]["wtot_col"][j * LANES:j * LANES + hd, :], (hd, LANES)),
                    jnp.broadcast_to(prep[d][c]["wtot_col"][j * LANES + hd:(j + 1) * LANES, :], (hd, LANES)))
          for d, c, j in items]

    idx = {it: n for n, it in enumerate(items)}
    heads = [(d, j) for d in (0, 1) for j in range(RWKV_PAIRS)]
    st = {dj: s_ref[dj[0], dj[1]] for dj in heads}
    for k in range(nch):
        sel = [(d, (k if d == 0 else nch - 1 - k), j) for d, j in heads]
        sy = [dot(jnp.concatenate([r_t[idx[it]] + big[idx[it]][lc:, :LANES], big[idx[it]][:lc, :LANES]], axis=0),
                  _bdiag(st[(it[0], it[2])], bd2)) for it in sel]
        for m, (d, c, j) in enumerate(sel):
            out_refs[d][0, c * lc:(c + 1) * lc, pl_[j]] = sy[m][:lc] + big[idx[(d, c, j)]][lc:, LANES:]
        for m, (d, c, j) in enumerate(sel):
            n = idx[(d, c, j)]
            st[(d, j)] = wc[n] * st[(d, j)] + sy[m][lc:] + big[n][:lc, LANES:]
    for d, j in heads:
        s_ref[d, j] = st[(d, j)]


def rwkv_scan(u_rwkv, mix, w0, w2, a0, a2, g2, k_k, k_a, r_k, ncx_tokens):
    b, ta, _ = u_rwkv.shape
    lc = RWKV_BLOCK
    nc = ta // lc
    ncx = ncx_tokens // lc
    nb = lc // SUBLANES
    nblk = ta // SUBLANES
    ni = RWKV_INNER
    kern = functools.partial(_rwkv_kernel, ncx=ncx, nc=nc)
    const2 = lambda bi, i: (0, 0)
    const3 = lambda bi, i: (0, 0, 0)
    fwd = lambda i: i
    bwd = lambda i: _chunk_of(1, i, ncx, nc)

    def u_specs(ch):
        return [pl.BlockSpec((1, lc, RWKV_IN), lambda bi, i: (bi, ch(i), 0)),
                pl.BlockSpec((1, SUBLANES, RWKV_IN), lambda bi, i: (bi, jnp.maximum(ch(i) * nb - 1, 0), 0)),
                pl.BlockSpec((1, SUBLANES, RWKV_IN), lambda bi, i: (bi, jnp.minimum((ch(i) + 1) * nb, nblk - 1), 0))]

    return pl.pallas_call(
        kern,
        out_shape=(jax.ShapeDtypeStruct((b, ta, 3 * ni), F32), jax.ShapeDtypeStruct((b, ta, 2 * ni), F32)),
        grid=(b, nc),
        in_specs=u_specs(fwd) + u_specs(bwd) + [
            pl.BlockSpec((1, RWKV_IN), const2),
            pl.BlockSpec((2, 1, ni), const3),
            pl.BlockSpec((2, RWKV_DECAY_LORA, ni), const3),
            pl.BlockSpec((2, 1, ni), const3),
            pl.BlockSpec((RWKV_AAA_LORA, ni), const2),
            pl.BlockSpec((RWKV_GATE_LORA, ni), const2),
            pl.BlockSpec((1, ni), const2),
            pl.BlockSpec((1, ni), const2),
            pl.BlockSpec((1, ni), const2),
        ],
        out_specs=(pl.BlockSpec((1, lc, 3 * ni), lambda bi, i: (bi, i, 0)),
                   pl.BlockSpec((1, lc, 2 * ni), lambda bi, i: (bi, bwd(i), 0))),
        scratch_shapes=[pltpu.VMEM((2, RWKV_PAIRS, RWKV_HEAD_DIM, LANES), F32)],
        compiler_params=_cparams(2),
        name="rwkv_scan",
    )(u_rwkv, u_rwkv, u_rwkv, u_rwkv, u_rwkv, u_rwkv, mix, w0, w2, a0, a2, g2, k_k, k_a, r_k)


def _merge_kernel(ssd_ref, z_ref, retf_ref, retb_ref, rg_ref, rwf_ref, rwb_ref, gate_ref, x_ref, mod_ref, nw_ref,
                  ssdnw_ref, lnw_ref, lnb_ref, wso_ref, wro_ref, wwo_ref, wo_ref, o_ref):
    ys = (ssd_ref[0, 0] + ssd_ref[1, 0]) * _silu(z_ref[0][:, :SSD_INNER])
    parts = []
    for g in range(SSD_GROUPS):
        yg = ys[:, g * SSD_GROUP_W:(g + 1) * SSD_GROUP_W]
        parts.append(yg * lax.rsqrt(jnp.mean(yg * yg, axis=-1, keepdims=True) + EPS))
    ys = jnp.concatenate(parts, axis=1) * ssdnw_ref[...]
    o_ssd = jnp.dot(ys.astype(BF16), wso_ref[...], preferred_element_type=F32)

    yr = retf_ref[0] + retb_ref[0]
    parts = []
    for h in range(RET_HEADS):
        yh = yr[:, h * RET_VAL_DIM:(h + 1) * RET_VAL_DIM]
        yc = yh - jnp.mean(yh, axis=-1, keepdims=True)
        parts.append(yc * lax.rsqrt(jnp.mean(yc * yc, axis=-1, keepdims=True) + EPS))
    yr = jnp.concatenate(parts, axis=1) * _silu(rg_ref[0])
    o_ret = jnp.dot(yr.astype(BF16), wro_ref[...], preferred_element_type=F32)

    ni = RWKV_INNER
    rw0 = rwf_ref[0]
    rw1 = rwb_ref[0]
    yw = rw0[:, :ni] + rw1[:, :ni]
    pair_ones = _pair_ones()
    inv_hd = 1.0 / RWKV_HEAD_DIM
    yc = yw - _head_sums([yw], pair_ones)[0] * inv_hd
    var = _head_sums([yc * yc], pair_ones)[0] * inv_hd
    yw = yc * lax.rsqrt(var + RWKV_LN_EPS) * lnw_ref[...] + lnb_ref[...]
    yw = (yw + rw0[:, ni:2 * ni] + rw1[:, ni:2 * ni]) * rw0[:, 2 * ni:3 * ni]
    o_rw = jnp.dot(yw.astype(BF16), wwo_ref[...], preferred_element_type=F32)

    gate = gate_ref[0]
    merged = (jax.nn.sigmoid(gate[:, :D_MODEL]) * o_ssd
              + jax.nn.sigmoid(gate[:, D_MODEL:2 * D_MODEL]) * o_ret
              + jax.nn.sigmoid(gate[:, 2 * D_MODEL:]) * o_rw)
    yx = jnp.dot(merged.astype(BF16), wo_ref[...], preferred_element_type=F32)
    g1 = mod_ref[0, 0, 2:3, :]
    o_ref[0] = x_ref[0] + g1 * _rms(yx, nw_ref[...])


def merge_out(ssd_y, u_zdt, ret_f, ret_b, u_ret, rw_f, rw_b, u_gate, x, mod, nw1, ssd_nw, ln_w, ln_b,
              w_ssd_out, w_ret_out, w_rwkv_out, w_out, n_ctx_tiles):
    b, ta, dm = x.shape
    tm = ROW_TILE
    row = lambda bi, i: (bi, i, 0)
    both = lambda bi, i: (0, bi, i, 0)
    const2 = lambda bi, i: (0, 0)
    return pl.pallas_call(
        _merge_kernel,
        out_shape=jax.ShapeDtypeStruct((b, ta, dm), F32),
        grid=(b, ta // tm),
        in_specs=[
            pl.BlockSpec((2, 1, tm, SSD_INNER), both),
            pl.BlockSpec((1, tm, u_zdt.shape[2]), row),
            pl.BlockSpec((1, tm, RET_INNER), row),
            pl.BlockSpec((1, tm, RET_INNER), row),
            pl.BlockSpec((1, tm, RET_INNER), lambda bi, i: (bi, i, 2)),
            pl.BlockSpec((1, tm, 3 * RWKV_INNER), row),
            pl.BlockSpec((1, tm, 2 * RWKV_INNER), row),
            pl.BlockSpec((1, tm, N_BRANCH * dm), row),
            pl.BlockSpec((1, tm, dm), row),
            pl.BlockSpec((1, 1, SUBLANES, dm), lambda bi, i: (bi, (i >= n_ctx_tiles).astype(jnp.int32), 0, 0)),
            pl.BlockSpec((1, dm), const2),
            pl.BlockSpec((1, SSD_INNER), const2),
            pl.BlockSpec((1, RWKV_INNER), const2),
            pl.BlockSpec((1, RWKV_INNER), const2),
            pl.BlockSpec((SSD_INNER, dm), const2),
            pl.BlockSpec((RET_INNER, dm), const2),
            pl.BlockSpec((RWKV_INNER, dm), const2),
            pl.BlockSpec((dm, dm), const2),
        ],
        out_specs=pl.BlockSpec((1, tm, dm), row),
        compiler_params=_cparams(2),
        name="merge_out",
    )(ssd_y, u_zdt, ret_f, ret_b, u_ret, rw_f, rw_b, u_gate, x, mod, nw1, ssd_nw, ln_w, ln_b,
      w_ssd_out, w_ret_out, w_rwkv_out, w_out)


def _mlp_kernel(x_ref, mod_ref, nw2_ref, nw3_ref, w1_ref, w2_ref, *rest, ff_tile, with_next):
    nb, tm, dm = x_ref.shape
    x = x_ref[...]
    mod = mod_ref[:, 0]
    y = _rms(x, nw2_ref[...])
    h = (y * (1.0 + mod[:, 4:5, :]) + mod[:, 3:4, :]).astype(BF16).reshape(nb * tm, dm)
    acc = jnp.zeros((nb * tm, dm), F32)
    for j in range(D_FF // ff_tile):
        hid = jnp.dot(h, w1_ref[:, j * ff_tile:(j + 1) * ff_tile], preferred_element_type=F32)
        hid = jnp.square(jnp.maximum(hid, 0.0)).astype(BF16)
        acc = acc + jnp.dot(hid, w2_ref[j * ff_tile:(j + 1) * ff_tile, :], preferred_element_type=F32)
    x_new = x + mod[:, 5:6, :] * _rms(acc.reshape(nb, tm, dm), nw3_ref[...])
    if with_next:
        nwn_ref, modn_ref, o_ref, h_ref = rest
        modn = modn_ref[:, 0]
        h_ref[...] = (_rms(x_new, nwn_ref[...]) * (1.0 + modn[:, 1:2, :]) + modn[:, 0:1, :]).astype(BF16)
    else:
        o_ref, = rest
    o_ref[...] = x_new


def mlp_block(x, mod, nw2, nw3, w1, w2, n_ctx_tiles, skip_tiles=0, next_norm=None, ff_tile=1024):
    b, ta, dm = x.shape
    tm = ROW_TILE
    rows = lambda i: (0, i, 0)
    const2 = lambda i: (0, 0)
    mod_spec = pl.BlockSpec((b, 1, SUBLANES, dm),
                            lambda i: (0, (i + skip_tiles >= n_ctx_tiles).astype(jnp.int32), 0, 0))
    resident = pl.Buffered(1)
    in_specs = [
        pl.BlockSpec((b, tm, dm), lambda i: (0, i + skip_tiles, 0)),
        mod_spec,
        pl.BlockSpec((1, dm), const2),
        pl.BlockSpec((1, dm), const2),
        pl.BlockSpec((dm, D_FF), const2, pipeline_mode=resident),
        pl.BlockSpec((D_FF, dm), const2, pipeline_mode=resident),
    ]
    args = [x, mod, nw2, nw3, w1, w2]
    out_rows = ta - skip_tiles * tm
    out_shape = jax.ShapeDtypeStruct((b, out_rows, dm), F32)
    out_specs = pl.BlockSpec((b, tm, dm), rows)
    if next_norm is not None:
        in_specs += [pl.BlockSpec((1, dm), const2), mod_spec]
        args += list(next_norm)
        out_shape = (out_shape, jax.ShapeDtypeStruct((b, out_rows, dm), BF16))
        out_specs = (out_specs, pl.BlockSpec((b, tm, dm), rows))
    return pl.pallas_call(
        functools.partial(_mlp_kernel, ff_tile=ff_tile, with_next=next_norm is not None),
        out_shape=out_shape,
        grid=(ta // tm - skip_tiles,),
        in_specs=in_specs,
        out_specs=out_specs,
        compiler_params=_cparams(1),
        name="mlp",
    )(*args)


def _rope_tables(n_ctx, n_lat):
    rows = n_lat // GRID_W
    row = np.repeat(np.arange(rows), GRID_W).astype(np.float32)
    col = np.tile(np.arange(GRID_W), rows).astype(np.float32)
    n_freq = RET_KEY_DIM // 4
    inv = jnp.power(ROPE_BASE, -jnp.arange(n_freq, dtype=F32) / n_freq)
    ang = jnp.concatenate([jnp.asarray(row)[:, None] * inv, jnp.asarray(col)[:, None] * inv], axis=-1)
    cos = jnp.concatenate([jnp.ones((n_ctx, RET_KEY_DIM // 2), F32), jnp.cos(ang)], axis=0)
    sin = jnp.concatenate([jnp.zeros((n_ctx, RET_KEY_DIM // 2), F32), jnp.sin(ang)], axis=0)
    cos_h = jnp.concatenate([cos, cos], axis=1)
    sin_h = jnp.concatenate([-sin, sin], axis=1)
    reps = 2 * RET_HEADS
    return jnp.tile(cos_h, (1, reps)), jnp.tile(sin_h, (1, reps))


def _pad_lanes(a, width=LANES):
    return jnp.pad(a, [(0, 0)] * (a.ndim - 1) + [(0, width - a.shape[-1])])


def kernel(x, c, ctx, c_ctx, norm_w, ada_w, ada_b, w_in, ssd_conv_w, ssd_conv_b, ssd_dt_bias, ssd_a_log,
           ssd_d, ssd_norm_w, ret_decay, rwkv_mix, rwkv_w0, rwkv_w2, rwkv_a0, rwkv_a2, rwkv_g2, rwkv_k_k,
           rwkv_k_a, rwkv_r_k, rwkv_lnx_w, rwkv_lnx_b, w_ssd_out, w_ret_out, w_rwkv_out, w_out, mlp_w1, mlp_w2):
    b, n_lat, dm = x.shape
    n_ctx = ctx.shape[1]
    ta = n_ctx + n_lat
    depth = norm_w.shape[0]
    n_ctx_tiles = n_ctx // ROW_TILE
    assert n_ctx % ROW_TILE == 0 and n_lat % ROW_TILE == 0 and b + 1 <= SUBLANES

    hexp = np.zeros((LANES, SSD_INNER), np.float32)
    for h in range(SSD_HEADS):
        hexp[h, h * SSD_HEAD_DIM:(h + 1) * SSD_HEAD_DIM] = 1.0
    hexp = jnp.asarray(hexp, BF16)
    cosx, sinx = _rope_tables(n_ctx, n_lat)

    cond = jnp.zeros((SUBLANES, dm), F32).at[:b].set(c).at[b].set(c_ctx)
    mod_all = modulation_all(cond, ada_w, ada_b)

    sizes = (N_BRANCH * dm, SSD_INNER, SSD_CONV_DIM, 2 * SSD_HEADS, RET_QK, RET_QK, RET_INNER, RET_INNER, RWKV_IN)
    offs = np.concatenate([[0], np.cumsum(sizes)])
    perm = np.concatenate([np.concatenate([np.arange(0, RET_KEY_DIM, 2), np.arange(1, RET_KEY_DIM, 2)]) + h * RET_KEY_DIM
                           for h in range(RET_HEADS)])

    xall = jnp.concatenate([ctx, x], axis=1)
    mods = []
    for l in range(depth):
        m = mod_all[l].reshape(SUBLANES, 6, dm)
        m = jnp.pad(m, ((0, 0), (0, SUBLANES - 6), (0, 0)))
        mods.append(jnp.stack([jnp.broadcast_to(m[b], (b,) + m.shape[1:]), m[:b]], axis=1))
    h = norm_modulate(xall, norm_w[0][0:1], mods[0], n_ctx_tiles)
    tm_mm = next(t for t in (1536, 1024, ROW_TILE) if (b * ta) % t == 0)
    for l in range(depth):
        wl = w_in[l]
        seg = [wl[:, offs[j]:offs[j + 1]] for j in range(len(sizes))]
        w_gate = seg[0].astype(BF16)
        w_zdt = jnp.concatenate([seg[1], _pad_lanes(seg[3][:, :SSD_HEADS]), _pad_lanes(seg[3][:, SSD_HEADS:])],
                                axis=1).astype(BF16)
        w_xbc = seg[2].astype(BF16)
        w_ret = jnp.concatenate([seg[4][:, perm], seg[5][:, perm], seg[6], seg[7]], axis=1).astype(BF16)
        w_rw = seg[8].astype(BF16)

        mod = mods[l]
        nw = norm_w[l]
        h = h.reshape(b * ta, dm)
        u_gate = matmul(h, w_gate, tm_mm, 1536).reshape(b, ta, -1)
        u_zdt = matmul(h, w_zdt, tm_mm, w_zdt.shape[1]).reshape(b, ta, -1)
        u_xbc = matmul(h, w_xbc, tm_mm, SSD_CONV_DIM).reshape(b, ta, -1)
        u_ret = matmul(h, w_ret, tm_mm, w_ret.shape[1]).reshape(b, ta, -1)
        u_rw = matmul(h, w_rw, tm_mm, RWKV_IN).reshape(b, ta, -1)

        conv_w = jnp.pad(ssd_conv_w[l], ((0, SUBLANES - SSD_CONV), (0, 0)))
        dskip = jnp.repeat(ssd_d[l], SSD_HEAD_DIM)[None, :]
        ssd_y = ssd_scan(u_xbc, u_zdt, conv_w, ssd_conv_b[l][None, :],
                         _pad_lanes(ssd_dt_bias[l])[:, None, :], _pad_lanes(ssd_a_log[l])[:, None, :],
                         dskip, hexp, n_ctx)
        ret_f, ret_b = ret_scan(u_ret, cosx, sinx, _pad_lanes(ret_decay[l])[:, None, :], n_ctx)
        rw_f, rw_b = rwkv_scan(u_rw, rwkv_mix[l][None, :], rwkv_w0[l][:, None, :], rwkv_w2[l], rwkv_a0[l][:, None, :],
                         rwkv_a2[l], rwkv_g2[l].astype(BF16), rwkv_k_k[l][None, :], rwkv_k_a[l][None, :],
                         rwkv_r_k[l].reshape(1, RWKV_INNER), n_ctx)

        xall = merge_out(ssd_y, u_zdt, ret_f, ret_b, u_ret, rw_f, rw_b, u_gate, xall, mod, nw[1:2],
                         ssd_norm_w[l][None, :], rwkv_lnx_w[l][None, :], rwkv_lnx_b[l][None, :],
                         w_ssd_out[l].astype(BF16), w_ret_out[l].astype(BF16), w_rwkv_out[l].astype(BF16),
                         w_out[l].astype(BF16), n_ctx_tiles)
        w1, w2 = mlp_w1[l].astype(BF16), mlp_w2[l].astype(BF16)
        if l + 1 < depth:
            xall, h = mlp_block(xall, mod, nw[2:3], nw[3:4], w1, w2, n_ctx_tiles,
                                next_norm=(norm_w[l + 1][0:1], mods[l + 1]))
        else:
            xall = mlp_block(xall, mod, nw[2:3], nw[3:4], w1, w2, n_ctx_tiles, skip_tiles=n_ctx_tiles)
    return xall
```

```python
import functools
import math

import numpy as np
import jax
import jax.numpy as jnp
from jax import lax
from jax.experimental import pallas as pl
from jax.experimental.pallas import tpu as pltpu

F32 = jnp.float32
BF16 = jnp.bfloat16
HIGHEST = lax.Precision.HIGHEST

D_MODEL = 1024
GRID_W = 64
EPS = 1e-6
N_BRANCH = 3

SSD_HEADS = 16
SSD_HEAD_DIM = 64
SSD_INNER = SSD_HEADS * SSD_HEAD_DIM
SSD_GROUPS = 2
SSD_STATE = 128
SSD_CONV = 5
SSD_CHUNK = 128
SSD_BLOCK = 2 * SSD_CHUNK
SSD_CONV_DIM = SSD_INNER + 2 * SSD_GROUPS * SSD_STATE
SSD_GROUP_W = SSD_INNER // SSD_GROUPS

RET_HEADS = 4
RET_KEY_DIM = 64
RET_VAL_DIM = 128
RET_QK = RET_HEADS * RET_KEY_DIM
RET_INNER = RET_HEADS * RET_VAL_DIM
RET_CHUNK = 128
RET_BLOCK = 2 * RET_CHUNK
ROPE_BASE = 10000.0

RWKV_HEADS = 8
RWKV_HEAD_DIM = 64
RWKV_INNER = RWKV_HEADS * RWKV_HEAD_DIM
RWKV_DECAY_LORA = 64
RWKV_AAA_LORA = 64
RWKV_GATE_LORA = 128
RWKV_IN = 3 * RWKV_INNER + RWKV_DECAY_LORA + RWKV_AAA_LORA + RWKV_GATE_LORA
RWKV_LN_EPS = 64e-5
RWKV_CHUNK = 64

D_FF = 4 * D_MODEL

SUBLANES = 8
LANES = 128
VMEM_LIMIT_BYTES = 56 * 1024 * 1024

ROW_TILE = 256
NEG_BIG = -1e30


def _cparams(n_axes):
    return pltpu.CompilerParams(dimension_semantics=("arbitrary",) * n_axes,
                                vmem_limit_bytes=VMEM_LIMIT_BYTES)


def _sigmoid(x):
    return 0.5 * jnp.tanh(0.5 * x) + 0.5


def _silu(x):
    return x * _sigmoid(x)


def _softplus(x):
    return jnp.maximum(x, 0.0) + jnp.log1p(jnp.exp(-jnp.abs(x)))


def _split3(x):
    x1 = x.astype(BF16)
    r1 = x - x1.astype(F32)
    x2 = r1.astype(BF16)
    r2 = r1 - x2.astype(F32)
    return x1, x2, r2.astype(BF16)


def _dot_nt(a, b, precision=None):
    return lax.dot_general(a, b, (((1,), (1,)), ((), ())), preferred_element_type=F32, precision=precision)


def _dot_tn(a, b, precision=None):
    return lax.dot_general(a, b, (((0,), (0,)), ((), ())), preferred_element_type=F32, precision=precision)


def _chunk_of(d, i, ncx, nc):
    bwd = jnp.where(i < ncx, ncx - 1 - i, ncx + nc - 1 - i)
    return jnp.where(d == 0, i, bwd)


def _seg_edges(c, ncx, nc):
    seg_start = jnp.logical_or(c == 0, c == ncx)
    seg_end = jnp.logical_or(c == ncx - 1, c == nc - 1)
    return seg_start, seg_end


def _tri_mask(n, d, strict, reps=1):
    row = lax.broadcasted_iota(jnp.int32, (n, n * reps), 0)
    col = lax.broadcasted_iota(jnp.int32, (n, n * reps), 1) % n
    lead = jnp.where(d == 0, row - col, col - row)
    return lead > 0 if strict else lead >= 0


def _mod_kernel(c_ref, w_ref, b_ref, o_ref):
    s = _silu(c_ref[...])
    o_ref[0] = jnp.dot(s, w_ref[0], preferred_element_type=F32, precision=HIGHEST) + b_ref[0]


def modulation_all(cond, ada_w, ada_b):
    depth = ada_w.shape[0]
    tn = 1536
    return pl.pallas_call(
        _mod_kernel,
        out_shape=jax.ShapeDtypeStruct((depth, SUBLANES, 6 * D_MODEL), F32),
        grid=(depth, 6 * D_MODEL // tn),
        in_specs=[pl.BlockSpec((SUBLANES, D_MODEL), lambda l, j: (0, 0)),
                  pl.BlockSpec((1, D_MODEL, tn), lambda l, j: (l, 0, j)),
                  pl.BlockSpec((1, 1, tn), lambda l, j: (l, 0, j))],
        out_specs=pl.BlockSpec((1, SUBLANES, tn), lambda l, j: (l, 0, j)),
        compiler_params=_cparams(2),
        name="modulation",
    )(cond, ada_w, ada_b.reshape(depth, 1, 6 * D_MODEL))


def _rms(x, w):
    return x * lax.rsqrt(jnp.mean(x * x, axis=-1, keepdims=True) + EPS) * w


def _normmod_kernel(x_ref, nw_ref, mod_ref, h_ref):
    y = _rms(x_ref[0], nw_ref[...])
    shift = mod_ref[0, 0, 0:1, :]
    scale = mod_ref[0, 0, 1:2, :]
    h_ref[0] = (y * (1.0 + scale) + shift).astype(BF16)


def norm_modulate(x, nw, mod, n_ctx_tiles):
    b, ta, dm = x.shape
    return pl.pallas_call(
        _normmod_kernel,
        out_shape=jax.ShapeDtypeStruct((b, ta, dm), BF16),
        grid=(b, ta // ROW_TILE),
        in_specs=[pl.BlockSpec((1, ROW_TILE, dm), lambda bi, i: (bi, i, 0)),
                  pl.BlockSpec((1, dm), lambda bi, i: (0, 0)),
                  pl.BlockSpec((1, 1, SUBLANES, dm), lambda bi, i: (bi, (i >= n_ctx_tiles).astype(jnp.int32), 0, 0))],
        out_specs=pl.BlockSpec((1, ROW_TILE, dm), lambda bi, i: (bi, i, 0)),
        compiler_params=_cparams(2),
        name="norm_modulate",
    )(x, nw, mod)


def _mm_kernel(a_ref, w_ref, o_ref):
    o_ref[...] = jnp.dot(a_ref[...], w_ref[...], preferred_element_type=F32).astype(o_ref.dtype)


def matmul(a, w, tm, tn, out_dtype=F32):
    r, k = a.shape
    n = w.shape[1]
    assert r % tm == 0 and n % tn == 0
    return pl.pallas_call(
        _mm_kernel,
        out_shape=jax.ShapeDtypeStruct((r, n), out_dtype),
        grid=(n // tn, r // tm),
        in_specs=[pl.BlockSpec((tm, k), lambda j, i: (i, 0)),
                  pl.BlockSpec((k, tn), lambda j, i: (0, j))],
        out_specs=pl.BlockSpec((tm, tn), lambda j, i: (i, j)),
        compiler_params=_cparams(2),
        name="in_proj",
    )(a, w)


def _ssd_kernel(xbc_ref, xp_ref, xn_ref, dt_ref, cw_ref, cb_ref, dtb_ref, alog_ref, dskip_ref,
                hexp_ref, y_ref, s_ref, ext_ref, act_ref, *, ncx, nc):
    d = pl.program_id(1)
    i = pl.program_id(2)
    c = _chunk_of(d, i, ncx, nc)
    seg_start, seg_end = _seg_edges(c, ncx, nc)
    lc = SSD_CHUNK
    lb = SSD_BLOCK
    nch = lb // lc
    hb = SUBLANES

    @pl.when(i == 0)
    def _():
        s_ref[...] = jnp.zeros_like(s_ref)

    @pl.when(d == 0)
    def _():
        ext_ref[0:hb, :] = jnp.where(seg_start, 0.0, xp_ref[0])
        ext_ref[hb:hb + lb, :] = xbc_ref[0]
        ext_ref[hb + lb:hb + lb + hb, :] = jnp.where(seg_end, 0.0, xn_ref[0])
        pad = SSD_CONV // 2
        acc = ext_ref[hb - pad:hb - pad + lb, :] * cw_ref[0:1, :]
        for j in range(1, SSD_CONV):
            acc = acc + ext_ref[hb - pad + j:hb - pad + j + lb, :] * cw_ref[j:j + 1, :]
        act = _silu(acc + cb_ref[...])
        act_ref[c] = act.astype(BF16)
        y_ref[0, 0] = act[:, :SSD_INNER] * dskip_ref[...]

    @pl.when(d != 0)
    def _():
        y_ref[0, 0] = jnp.zeros(y_ref.shape[2:], F32)

    tri = _tri_mask(lc, d, strict=False)
    hg = SSD_HEADS // SSD_GROUPS
    low_half = lax.broadcasted_iota(jnp.int32, (lc, 2 * SSD_HEAD_DIM), 1) < SSD_HEAD_DIM
    neg_a = -jnp.exp(alog_ref[0])

    local = []
    for k in range(nch):
        rs = pl.ds(pl.multiple_of(jnp.where(d == 0, k, nch - 1 - k) * lc, lc), lc)
        xbc = act_ref[c, rs, :]
        bm = xbc[:, SSD_INNER:SSD_INNER + SSD_GROUPS * SSD_STATE]
        cm = xbc[:, SSD_INNER + SSD_GROUPS * SSD_STATE:]
        dt = _softplus(dt_ref[0, rs, :] + dtb_ref[0])
        la = dt * neg_a
        ac3 = jnp.dot(tri.astype(BF16), jnp.concatenate(_split3(la), axis=1), preferred_element_type=F32)
        acum = ac3[:, :LANES] + ac3[:, LANES:2 * LANES] + ac3[:, 2 * LANES:]
        atot = jnp.sum(la, axis=0, keepdims=True)
        acum_t = acum.T
        etot8 = jnp.broadcast_to(jnp.exp(atot), (SUBLANES, LANES))
        parts = jnp.concatenate(_split2(jnp.exp(acum)) + _split2(etot8), axis=0)
        ex = jnp.dot(parts, hexp_ref[...], preferred_element_type=F32)
        eacx = ex[0:lc] + ex[lc:2 * lc]
        etotx = (ex[2 * lc:2 * lc + SUBLANES] + ex[2 * lc + SUBLANES:])[0:1, :]
        parts_b = jnp.concatenate([dt.astype(BF16), jnp.exp(atot - acum).astype(BF16)], axis=0)
        ex_b = jnp.dot(parts_b, hexp_ref[...], preferred_element_type=F32).astype(BF16)
        vb = xbc[:, :SSD_INNER] * ex_b[0:lc]
        vend = vb * ex_b[lc:2 * lc]
        groups = []
        for g in range(SSD_GROUPS):
            gs = slice(g * SSD_STATE, (g + 1) * SSD_STATE)
            gw = slice(g * SSD_GROUP_W, (g + 1) * SSD_GROUP_W)
            cg = cm[:, gs]
            bg = bm[:, gs]
            scores = _dot_nt(cg, bg)
            y_in = []
            for hp in range(hg // 2):
                a_pair = []
                for h in (g * hg + 2 * hp, g * hg + 2 * hp + 1):
                    diff = acum[:, h:h + 1] - acum_t[h:h + 1, :]
                    a_pair.append((scores * jnp.exp(jnp.where(tri, diff, NEG_BIG))).astype(BF16))
                ps = slice((g * hg + 2 * hp) * SSD_HEAD_DIM, (g * hg + 2 * hp + 2) * SSD_HEAD_DIM)
                v_pair = vb[:, ps]
                w_pair = jnp.concatenate([jnp.where(low_half, v_pair, 0.0), jnp.where(low_half, 0.0, v_pair)],
                                         axis=0)
                y_in.append(jnp.dot(jnp.concatenate(a_pair, axis=1), w_pair, preferred_element_type=F32))
            groups.append(dict(cg=cg, y_in=jnp.concatenate(y_in, axis=1), eac=eacx[:, gw], etot=etotx[:, gw],
                               cs=_dot_tn(bg, vend[:, gw])))
        local.append((rs, groups))

    for g in range(SSD_GROUPS):
        gw = slice(g * SSD_GROUP_W, (g + 1) * SSD_GROUP_W)
        st = s_ref[g]
        for rs, groups in local:
            q = groups[g]
            y_ref[0, 0, rs, gw] += q["y_in"] + jnp.dot(q["cg"], st.astype(BF16),
                                                       preferred_element_type=F32) * q["eac"]
            st = st * q["etot"] + q["cs"]
        s_ref[g] = st


def ssd_scan(u_xbc, u_zdt, conv_w, conv_b, dt_bias, a_log, dskip, hexp, ncx_tokens):
    b, ta, _ = u_xbc.shape
    lc = SSD_BLOCK
    nc = ta // lc
    ncx = ncx_tokens // lc
    nb = lc // SUBLANES
    nblk = ta // SUBLANES
    ch = functools.partial(_chunk_of, ncx=ncx, nc=nc)
    kern = functools.partial(_ssd_kernel, ncx=ncx, nc=nc)
    zdt_blk0 = SSD_INNER // LANES
    return pl.pallas_call(
        kern,
        out_shape=jax.ShapeDtypeStruct((2, b, ta, SSD_INNER), F32),
        grid=(b, 2, nc),
        in_specs=[
            pl.BlockSpec((1, lc, SSD_CONV_DIM), lambda bi, d, i: (bi, i * (1 - d), 0)),
            pl.BlockSpec((1, SUBLANES, SSD_CONV_DIM), lambda bi, d, i: (bi, jnp.maximum(i * nb - 1, 0) * (1 - d), 0)),
            pl.BlockSpec((1, SUBLANES, SSD_CONV_DIM),
                         lambda bi, d, i: (bi, jnp.minimum((i + 1) * nb, nblk - 1) * (1 - d), 0)),
            pl.BlockSpec((1, lc, LANES), lambda bi, d, i: (bi, ch(d, i), zdt_blk0 + d)),
            pl.BlockSpec((SUBLANES, SSD_CONV_DIM), lambda bi, d, i: (0, 0)),
            pl.BlockSpec((1, SSD_CONV_DIM), lambda bi, d, i: (0, 0)),
            pl.BlockSpec((1, 1, LANES), lambda bi, d, i: (d, 0, 0)),
            pl.BlockSpec((1, 1, LANES), lambda bi, d, i: (d, 0, 0)),
            pl.BlockSpec((1, SSD_INNER), lambda bi, d, i: (0, 0)),
            pl.BlockSpec((LANES, SSD_INNER), lambda bi, d, i: (0, 0)),
        ],
        out_specs=pl.BlockSpec((1, 1, lc, SSD_INNER), lambda bi, d, i: (d, bi, ch(d, i), 0)),
        scratch_shapes=[pltpu.VMEM((SSD_GROUPS, SSD_STATE, SSD_GROUP_W), F32),
                        pltpu.VMEM((lc + 2 * SUBLANES, SSD_CONV_DIM), F32),
                        pltpu.VMEM((nc, lc, SSD_CONV_DIM), BF16)],
        compiler_params=_cparams(3),
        name="ssd_scan",
    )(u_xbc, u_xbc, u_xbc, u_zdt, conv_w, conv_b, dt_bias, a_log, dskip, hexp)


def _ret_kernel(qkf_ref, vf_ref, cosf_ref, sinf_ref, qkb_ref, vb_ref, cosb_ref, sinb_ref, dec_ref,
                yf_ref, yb_ref, s_ref):
    i = pl.program_id(1)
    lc = RET_CHUNK
    lb = RET_BLOCK
    nch = lb // lc

    @pl.when(i == 0)
    def _():
        s_ref[...] = jnp.zeros_like(s_ref)

    in_refs = ((qkf_ref, vf_ref, cosf_ref, sinf_ref), (qkb_ref, vb_ref, cosb_ref, sinb_ref))
    out_refs = (yf_ref, yb_ref)
    half = RET_KEY_DIM // 2
    width = 2 * RET_QK
    lane = lax.broadcasted_iota(jnp.int32, (lb, width), 1)
    row = lax.broadcasted_iota(jnp.int32, (lc, lc), 0)
    col = lax.broadcasted_iota(jnp.int32, (lc, lc), 1)
    dist = jnp.abs(row - col).astype(F32)
    tri = (col <= row, col >= row)
    t_idx = lax.broadcasted_iota(jnp.int32, (lc, RET_VAL_DIM), 0).astype(F32)
    pos = (t_idx, lc - 1.0 - t_idx)
    q, k_t, v, lg = [], [], [], []
    for d in (0, 1):
        qk_ref, v_ref, cos_ref, sin_ref = in_refs[d]
        qk = qk_ref[0]
        swapped = jnp.where((lane % RET_KEY_DIM) < half,
                            pltpu.roll(qk, width - half, axis=1), pltpu.roll(qk, half, axis=1))
        qk = qk * cos_ref[...] + swapped * sin_ref[...]
        q.append(qk[:, :RET_QK].astype(BF16))
        k_t.append((qk[:, RET_QK:] * (RET_KEY_DIM ** -0.5)).T.astype(BF16))
        v.append(v_ref[0])
        lg.append(-_softplus(-dec_ref[d]))

    items = [(d, c, h) for d in (0, 1) for c in range(nch) for h in range(RET_HEADS)]
    n_it = range(len(items))
    idx = {it: n for n, it in enumerate(items)}
    rs = [slice(c * lc, (c + 1) * lc) for d, c, h in items]
    ks = [slice(h * RET_KEY_DIM, (h + 1) * RET_KEY_DIM) for d, c, h in items]
    vs = [slice(h * RET_VAL_DIM, (h + 1) * RET_VAL_DIM) for d, c, h in items]
    lgh = [lg[d][:, h:h + 1] for d, c, h in items]
    qh = [q[d][rs[n], ks[n]] for n, (d, c, h) in enumerate(items)]
    kh_t = [k_t[d][ks[n], rs[n]] for n, (d, c, h) in enumerate(items)]
    vh = [v[d][rs[n], vs[n]] for n, (d, c, h) in enumerate(items)]
    scores = [(jnp.dot(qh[n], kh_t[n], preferred_element_type=F32)
               * jnp.exp(jnp.where(tri[items[n][0]], dist * lgh[n], NEG_BIG))).astype(BF16) for n in n_it]
    y_in = [jnp.dot(scores[n], vh[n].astype(BF16), preferred_element_type=F32) for n in n_it]
    vend = [(vh[n] * jnp.exp((lc - 1.0 - pos[items[n][0]]) * lgh[n])).astype(BF16) for n in n_it]
    cs = [jnp.dot(kh_t[n], vend[n], preferred_element_type=F32) for n in n_it]
    heads = [(d, h) for d in (0, 1) for h in range(RET_HEADS)]
    st = {dh: s_ref[dh[0], dh[1]] for dh in heads}
    for k in range(nch):
        sel = [idx[(d, (k if d == 0 else nch - 1 - k), h)] for d, h in heads]
        y_st = [jnp.dot(qh[n], st[(items[n][0], items[n][2])].astype(BF16), preferred_element_type=F32)
                for n in sel]
        for m, n in enumerate(sel):
            d, c, h = items[n]
            out_refs[d][0, rs[n], vs[n]] = y_in[n] + y_st[m] * jnp.exp((pos[d] + 1.0) * lgh[n])
        for n in sel:
            d, c, h = items[n]
            st[(d, h)] = st[(d, h)] * jnp.exp(lc * lgh[n]) + cs[n]
    for d, h in heads:
        s_ref[d, h] = st[(d, h)]


def ret_scan(u_ret, cosx, sinx, ret_decay, ncx_tokens):
    b, ta, _ = u_ret.shape
    lc = RET_BLOCK
    nc = ta // lc
    ncx = ncx_tokens // lc
    fwd = lambda i: i
    bwd = lambda i: _chunk_of(1, i, ncx, nc)

    def specs(ch):
        return [pl.BlockSpec((1, lc, 2 * RET_QK), lambda bi, i: (bi, ch(i), 0)),
                pl.BlockSpec((1, lc, RET_INNER), lambda bi, i: (bi, ch(i), 1)),
                pl.BlockSpec((lc, 2 * RET_QK), lambda bi, i: (ch(i), 0)),
                pl.BlockSpec((lc, 2 * RET_QK), lambda bi, i: (ch(i), 0))]

    return pl.pallas_call(
        _ret_kernel,
        out_shape=(jax.ShapeDtypeStruct((b, ta, RET_INNER), F32), jax.ShapeDtypeStruct((b, ta, RET_INNER), F32)),
        grid=(b, nc),
        in_specs=specs(fwd) + specs(bwd) + [pl.BlockSpec((2, 1, LANES), lambda bi, i: (0, 0, 0))],
        out_specs=(pl.BlockSpec((1, lc, RET_INNER), lambda bi, i: (bi, i, 0)),
                   pl.BlockSpec((1, lc, RET_INNER), lambda bi, i: (bi, bwd(i), 0))),
        scratch_shapes=[pltpu.VMEM((2, RET_HEADS, RET_KEY_DIM, RET_VAL_DIM), F32)],
        compiler_params=_cparams(2),
        name="ret_scan",
    )(u_ret, u_ret, cosx, sinx, u_ret, u_ret, cosx, sinx, ret_decay)


RWKV_PAIRS = RWKV_HEADS // 2
RWKV_BLOCK = 4 * RWKV_CHUNK


def _bdiag(tile, bd2):
    return jnp.where(bd2, jnp.concatenate([tile, tile], axis=0), 0.0)


def _split2(x):
    hi = x.astype(BF16)
    return hi, (x - hi.astype(F32)).astype(BF16)


def _head_sums(xs, pair_ones):
    rows = xs[0].shape[0]
    parts = jnp.concatenate([p for x in xs for p in _split2(x)], axis=0)
    blk = parts.shape[0]
    n_pair = parts.shape[1] // LANES
    stacked = jnp.concatenate([parts[:, j * LANES:(j + 1) * LANES] for j in range(n_pair)], axis=0)
    y = jnp.dot(stacked, pair_ones, preferred_element_type=F32)
    full = jnp.concatenate([y[j * blk:(j + 1) * blk] for j in range(n_pair)], axis=1)
    return [full[2 * i * rows:(2 * i + 1) * rows] + full[(2 * i + 1) * rows:(2 * i + 2) * rows]
            for i in range(len(xs))]


def _pair_ones():
    hd = RWKV_HEAD_DIM
    return jnp.where((lax.broadcasted_iota(jnp.int32, (LANES, LANES), 0) < hd)
                     == (lax.broadcasted_iota(jnp.int32, (LANES, LANES), 1) < hd), 1.0, 0.0).astype(BF16)


def _dot3(a, b):
    a1, a2 = _split2(a)
    b1, b2 = _split2(b)
    n = a.shape[0]
    o = jnp.dot(jnp.concatenate([a1, a2], axis=0), b1, preferred_element_type=F32)
    return o[:n] + o[n:] + jnp.dot(a1, b2, preferred_element_type=F32)


def _pair_mm(lhs_splits, w_split, bd2):
    hi = jnp.concatenate([s[0] for s in lhs_splits], axis=0)
    lo = jnp.concatenate([s[1] for s in lhs_splits], axis=0)
    n = hi.shape[0]
    o = jnp.dot(jnp.concatenate([hi, lo], axis=0), _bdiag(w_split[0], bd2), preferred_element_type=F32)
    tot = o[:n] + o[n:] + jnp.dot(hi, _bdiag(w_split[1], bd2), preferred_element_type=F32)
    return [tot[i * RWKV_HEAD_DIM:(i + 1) * RWKV_HEAD_DIM] for i in range(len(lhs_splits))]


def _pair_rows(x_t, j, low_half):
    base = j * LANES
    return jnp.where(low_half, x_t[base:base + RWKV_HEAD_DIM], x_t[base + RWKV_HEAD_DIM:base + LANES])


def _rwkv_prep(d, u, prow, nrow, mix, w0, w2, a0, a2, kkw, ka, rk, bd):
    lc = RWKV_CHUNK
    lb = u.shape[0]
    ni = RWKV_INNER
    half_mix = 0.5 * mix
    nbr = pltpu.roll(u, 1, axis=0) + pltpu.roll(u, lb - 1, axis=0)
    sub = lax.broadcasted_iota(jnp.int32, (SUBLANES, 1), 0)
    fix_first = jnp.where(sub == 0, prow - u[lb - 1:lb, :], 0.0)
    fix_last = jnp.where(sub == SUBLANES - 1, nrow - u[0:1, :], 0.0)
    nbr = jnp.concatenate([nbr[:SUBLANES] + fix_first, nbr[SUBLANES:lb - SUBLANES],
                           nbr[lb - SUBLANES:] + fix_last], axis=0)
    u = (1.0 - mix) * u + half_mix * nbr
    r, k, v = u[:, 0:ni], u[:, ni:2 * ni], u[:, 2 * ni:3 * ni]
    o = 3 * ni
    w_lo = u[:, o:o + RWKV_DECAY_LORA]
    a_lo = u[:, o + RWKV_DECAY_LORA:o + RWKV_DECAY_LORA + RWKV_AAA_LORA]
    g_lo = u[:, o + RWKV_DECAY_LORA + RWKV_AAA_LORA:]

    logw = -math.exp(-0.5) * _sigmoid(w0 + _dot3(jnp.tanh(w_lo), w2))
    a_gate = _sigmoid(a0 + _dot3(a_lo, a2))
    kk = k * kkw
    kd = k * (1.0 + (a_gate - 1.0) * ka)
    kk_ss, rk_sum = _head_sums([kk * kk, r * kd * rk], bd)
    kk = kk / jnp.maximum(jnp.sqrt(kk_ss), 1e-12)
    bvec = kk * a_gate
    bonus = rk_sum * v

    row = lax.broadcasted_iota(jnp.int32, (lb, lb), 0)
    col = lax.broadcasted_iota(jnp.int32, (lb, lb), 1)
    lead = (row - col) if d == 0 else (col - row)
    incl = jnp.where(row // lc == col // lc, lead, -1) >= 0
    l1, l2, l3 = _split3(logw)
    cw3 = jnp.dot(incl.astype(BF16), jnp.concatenate([l1, l2, l3], axis=1), preferred_element_type=F32)
    cw = cw3[:, 0:ni] + cw3[:, ni:2 * ni] + cw3[:, 2 * ni:3 * ni]

    def dup_t(x):
        return jnp.concatenate([x, x], axis=0).T

    chunks = []
    for c in range(lb // lc):
        rs = slice(c * lc, (c + 1) * lc)
        logw_c, cw_c = logw[rs], cw[rs]
        ctot = jnp.sum(logw_c, axis=0, keepdims=True)
        e_neg = jnp.exp(-cw_c)
        e_end = jnp.exp(ctot - cw_c)
        chunks.append(dict(
            a_t=-kk[rs] * jnp.exp(cw_c - logw_c), r_t=r[rs] * jnp.exp(cw_c), v=v[rs],
            bn_t=dup_t(bvec[rs] * e_neg), kn_t=dup_t(kd[rs] * e_neg),
            be_t=dup_t(bvec[rs] * e_end), ke_t=dup_t(kd[rs] * e_end),
            wtot_col=jnp.exp(jnp.sum(logw_c.T, axis=1, keepdims=True))))
    return chunks, bonus, g_lo


def _rwkv_kernel(uf_ref, ufp_ref, ufn_ref, ub_ref, ubp_ref, ubn_ref, mix_ref, w0_ref, w2_ref, a0_ref, a2_ref,
                 g2_ref, kk_ref, ka_ref, rk_ref, of_ref, ob_ref, s_ref, *, ncx, nc):
    i = pl.program_id(1)
    lc = RWKV_CHUNK
    ni = RWKV_INNER
    hd = RWKV_HEAD_DIM
    chunk = (i, _chunk_of(1, i, ncx, nc))

    @pl.when(i == 0)
    def _():
        s_ref[...] = jnp.zeros_like(s_ref)

    bd2 = ((lax.broadcasted_iota(jnp.int32, (LANES, LANES), 0) < hd)
           == (lax.broadcasted_iota(jnp.int32, (LANES, LANES), 1) < hd))
    pair_ones = jnp.where(bd2, 1.0, 0.0).astype(BF16)
    u_refs = ((uf_ref, ufp_ref, ufn_ref), (ub_ref, ubp_ref, ubn_ref))
    out_refs = (of_ref, ob_ref)
    prep = []
    for d in (0, 1):
        seg_start, seg_end = _seg_edges(chunk[d], ncx, nc)
        u_ref, up_ref, un_ref = u_refs[d]
        prow = jnp.where(seg_start, 0.0, up_ref[0, SUBLANES - 1:SUBLANES, :])
        nrow = jnp.where(seg_end, 0.0, un_ref[0, 0:1, :])
        chunks, bonus, g_lo = _rwkv_prep(d, u_ref[0], prow, nrow, mix_ref[...], w0_ref[d], w2_ref[d], a0_ref[d],
                                         a2_ref[...], kk_ref[...], ka_ref[...], rk_ref[...], pair_ones)
        out_refs[d][0, :, ni:2 * ni] = bonus
        if d == 0:
            of_ref[0, :, 2 * ni:3 * ni] = jnp.dot(_sigmoid(g_lo).astype(BF16), g2_ref[...],
                                                  preferred_element_type=F32)
        prep.append(chunks)

    row4 = lax.broadcasted_iota(jnp.int32, (lc, 2 * LANES), 0)
    col4 = lax.broadcasted_iota(jnp.int32, (lc, 2 * LANES), 1) % hd
    strict4 = (col4 < row4, col4 > row4)
    incl4 = (col4 <= row4, col4 >= row4)
    eye2 = (lax.broadcasted_iota(jnp.int32, (lc, LANES), 1) % hd
            == lax.broadcasted_iota(jnp.int32, (lc, LANES), 0)).astype(F32)
    low_half = lax.broadcasted_iota(jnp.int32, (hd, LANES), 1) < hd
    zeros_w = jnp.zeros((LANES, LANES), F32)

    def dot(a, b):
        return jnp.dot(a.astype(BF16), b.astype(BF16), preferred_element_type=F32)

    nch = RWKV_BLOCK // lc
    items = [(d, c, j) for d in (0, 1) for c in range(nch) for j in range(RWKV_PAIRS)]
    pl_ = {j: slice(j * LANES, (j + 1) * LANES) for j in range(RWKV_PAIRS)}
    n_it = range(len(items))
    a_t = [prep[d][c]["a_t"][:, pl_[j]] for d, c, j in items]
    r_t = [prep[d][c]["r_t"][:, pl_[j]] for d, c, j in items]
    v_w = [_bdiag(prep[d][c]["v"][:, pl_[j]], bd2) for d, c, j in items]
    w_p = [jnp.concatenate([jnp.where(bd2, prep[d][c]["bn_t"][pl_[j], :], 0.0),
                            jnp.where(bd2, prep[d][c]["kn_t"][pl_[j], :], 0.0)], axis=1) for d, c, j in items]
    p = [dot(jnp.concatenate([a_t[n], r_t[n]], axis=0), w_p[n]) for n in n_it]
    m_a = [jnp.where(strict4[items[n][0]], p[n][:lc], 0.0) for n in n_it]
    m_r = [jnp.where(incl4[items[n][0]], p[n][lc:], 0.0) for n in n_it]
    mv = [dot(m_a[n][:, LANES:], v_w[n]) for n in n_it]
    x = [m_a[n][:, :LANES] for n in n_it]
    t = [eye2 + x[n] for n in n_it]
    xs = [_split2(x[n]) for n in n_it]
    x = [_pair_mm([xs[n]], xs[n], bd2)[0] for n in n_it]
    for _ in range(int(math.log2(lc)) - 2):
        xs = [_split2(x[n]) for n in n_it]
        tx = [_pair_mm([_split2(t[n]), xs[n]], xs[n], bd2) for n in n_it]
        t = [t[n] + tx[n][0] for n in n_it]
        x = [tx[n][1] for n in n_it]
    t = [t[n] + _pair_mm([_split2(t[n])], _split2(x[n]), bd2)[0] for n in n_it]
    au = [dot(t[n], jnp.concatenate([_bdiag(a_t[n], bd2), _bdiag(mv[n], bd2)], axis=1)) for n in n_it]
    w2 = [jnp.concatenate([jnp.concatenate([_bdiag(au[n][:, :LANES], bd2), _bdiag(au[n][:, LANES:], bd2)], axis=1),
                           jnp.concatenate([zeros_w, v_w[n]], axis=1)], axis=0) for n in n_it]
    q_bk = [jnp.concatenate([_pair_rows(prep[d][c]["be_t"], j, low_half),
                             _pair_rows(prep[d][c]["ke_t"], j, low_half)], axis=1)
            for d, c, j in items]
    big = [dot(jnp.concatenate([q_bk[n], m_r[n]], axis=0), w2[n]) for n in n_it]
    wc = [jnp.where(low_half,
                    jnp.broadcast_to(prep[d][c]["wtot_col"][j * LANES:j * LANES + hd, :], (hd, LANES)),
                    jnp.broadcast_to(prep[d][c]["wtot_col"][j * LANES + hd:(j + 1) * LANES, :], (hd, LANES)))
          for d, c, j in items]

    idx = {it: n for n, it in enumerate(items)}
    heads = [(d, j) for d in (0, 1) for j in range(RWKV_PAIRS)]
    st = {dj: s_ref[dj[0], dj[1]] for dj in heads}
    for k in range(nch):
        sel = [(d, (k if d == 0 else nch - 1 - k), j) for d, j in heads]
        sy = [dot(jnp.concatenate([r_t[idx[it]] + big[idx[it]][lc:, :LANES], big[idx[it]][:lc, :LANES]], axis=0),
                  _bdiag(st[(it[0], it[2])], bd2)) for it in sel]
        for m, (d, c, j) in enumerate(sel):
            out_refs[d][0, c * lc:(c + 1) * lc, pl_[j]] = sy[m][:lc] + big[idx[(d, c, j)]][lc:, LANES:]
        for m, (d, c, j) in enumerate(sel):
            n = idx[(d, c, j)]
            st[(d, j)] = wc[n] * st[(d, j)] + sy[m][lc:] + big[n][:lc, LANES:]
    for d, j in heads:
        s_ref[d, j] = st[(d, j)]


def rwkv_scan(u_rwkv, mix, w0, w2, a0, a2, g2, k_k, k_a, r_k, ncx_tokens):
    b, ta, _ = u_rwkv.shape
    lc = RWKV_BLOCK
    nc = ta // lc
    ncx = ncx_tokens // lc
    nb = lc // SUBLANES
    nblk = ta // SUBLANES
    ni = RWKV_INNER
    kern = functools.partial(_rwkv_kernel, ncx=ncx, nc=nc)
    const2 = lambda bi, i: (0, 0)
    const3 = lambda bi, i: (0, 0, 0)
    fwd = lambda i: i
    bwd = lambda i: _chunk_of(1, i, ncx, nc)

    def u_specs(ch):
        return [pl.BlockSpec((1, lc, RWKV_IN), lambda bi, i: (bi, ch(i), 0)),
                pl.BlockSpec((1, SUBLANES, RWKV_IN), lambda bi, i: (bi, jnp.maximum(ch(i) * nb - 1, 0), 0)),
                pl.BlockSpec((1, SUBLANES, RWKV_IN), lambda bi, i: (bi, jnp.minimum((ch(i) + 1) * nb, nblk - 1), 0))]

    return pl.pallas_call(
        kern,
        out_shape=(jax.ShapeDtypeStruct((b, ta, 3 * ni), F32), jax.ShapeDtypeStruct((b, ta, 2 * ni), F32)),
        grid=(b, nc),
        in_specs=u_specs(fwd) + u_specs(bwd) + [
            pl.BlockSpec((1, RWKV_IN), const2),
            pl.BlockSpec((2, 1, ni), const3),
            pl.BlockSpec((2, RWKV_DECAY_LORA, ni), const3),
            pl.BlockSpec((2, 1, ni), const3),
            pl.BlockSpec((RWKV_AAA_LORA, ni), const2),
            pl.BlockSpec((RWKV_GATE_LORA, ni), const2),
            pl.BlockSpec((1, ni), const2),
            pl.BlockSpec((1, ni), const2),
            pl.BlockSpec((1, ni), const2),
        ],
        out_specs=(pl.BlockSpec((1, lc, 3 * ni), lambda bi, i: (bi, i, 0)),
                   pl.BlockSpec((1, lc, 2 * ni), lambda bi, i: (bi, bwd(i), 0))),
        scratch_shapes=[pltpu.VMEM((2, RWKV_PAIRS, RWKV_HEAD_DIM, LANES), F32)],
        compiler_params=_cparams(2),
        name="rwkv_scan",
    )(u_rwkv, u_rwkv, u_rwkv, u_rwkv, u_rwkv, u_rwkv, mix, w0, w2, a0, a2, g2, k_k, k_a, r_k)


def _merge_kernel(ssd_ref, z_ref, retf_ref, retb_ref, rg_ref, rwf_ref, rwb_ref, gate_ref, x_ref, mod_ref, nw_ref,
                  ssdnw_ref, lnw_ref, lnb_ref, wso_ref, wro_ref, wwo_ref, wo_ref, o_ref):
    ys = (ssd_ref[0, 0] + ssd_ref[1, 0]) * _silu(z_ref[0][:, :SSD_INNER])
    parts = []
    for g in range(SSD_GROUPS):
        yg = ys[:, g * SSD_GROUP_W:(g + 1) * SSD_GROUP_W]
        parts.append(yg * lax.rsqrt(jnp.mean(yg * yg, axis=-1, keepdims=True) + EPS))
    ys = jnp.concatenate(parts, axis=1) * ssdnw_ref[...]
    o_ssd = jnp.dot(ys.astype(BF16), wso_ref[...], preferred_element_type=F32)

    yr = retf_ref[0] + retb_ref[0]
    parts = []
    for h in range(RET_HEADS):
        yh = yr[:, h * RET_VAL_DIM:(h + 1) * RET_VAL_DIM]
        yc = yh - jnp.mean(yh, axis=-1, keepdims=True)
        parts.append(yc * lax.rsqrt(jnp.mean(yc * yc, axis=-1, keepdims=True) + EPS))
    yr = jnp.concatenate(parts, axis=1) * _silu(rg_ref[0])
    o_ret = jnp.dot(yr.astype(BF16), wro_ref[...], preferred_element_type=F32)

    ni = RWKV_INNER
    rw0 = rwf_ref[0]
    rw1 = rwb_ref[0]
    yw = rw0[:, :ni] + rw1[:, :ni]
    pair_ones = _pair_ones()
    inv_hd = 1.0 / RWKV_HEAD_DIM
    yc = yw - _head_sums([yw], pair_ones)[0] * inv_hd
    var = _head_sums([yc * yc], pair_ones)[0] * inv_hd
    yw = yc * lax.rsqrt(var + RWKV_LN_EPS) * lnw_ref[...] + lnb_ref[...]
    yw = (yw + rw0[:, ni:2 * ni] + rw1[:, ni:2 * ni]) * rw0[:, 2 * ni:3 * ni]
    o_rw = jnp.dot(yw.astype(BF16), wwo_ref[...], preferred_element_type=F32)

    gate = gate_ref[0]
    merged = (_sigmoid(gate[:, :D_MODEL]) * o_ssd
              + _sigmoid(gate[:, D_MODEL:2 * D_MODEL]) * o_ret
              + _sigmoid(gate[:, 2 * D_MODEL:]) * o_rw)
    yx = jnp.dot(merged.astype(BF16), wo_ref[...], preferred_element_type=F32)
    g1 = mod_ref[0, 0, 2:3, :]
    o_ref[0] = x_ref[0] + g1 * _rms(yx, nw_ref[...])


def merge_out(ssd_y, u_zdt, ret_f, ret_b, u_ret, rw_f, rw_b, u_gate, x, mod, nw1, ssd_nw, ln_w, ln_b,
              w_ssd_out, w_ret_out, w_rwkv_out, w_out, n_ctx_tiles):
    b, ta, dm = x.shape
    tm = ROW_TILE
    row = lambda bi, i: (bi, i, 0)
    both = lambda bi, i: (0, bi, i, 0)
    const2 = lambda bi, i: (0, 0)
    return pl.pallas_call(
        _merge_kernel,
        out_shape=jax.ShapeDtypeStruct((b, ta, dm), F32),
        grid=(b, ta // tm),
        in_specs=[
            pl.BlockSpec((2, 1, tm, SSD_INNER), both),
            pl.BlockSpec((1, tm, u_zdt.shape[2]), row),
            pl.BlockSpec((1, tm, RET_INNER), row),
            pl.BlockSpec((1, tm, RET_INNER), row),
            pl.BlockSpec((1, tm, RET_INNER), lambda bi, i: (bi, i, 2)),
            pl.BlockSpec((1, tm, 3 * RWKV_INNER), row),
            pl.BlockSpec((1, tm, 2 * RWKV_INNER), row),
            pl.BlockSpec((1, tm, N_BRANCH * dm), row),
            pl.BlockSpec((1, tm, dm), row),
            pl.BlockSpec((1, 1, SUBLANES, dm), lambda bi, i: (bi, (i >= n_ctx_tiles).astype(jnp.int32), 0, 0)),
            pl.BlockSpec((1, dm), const2),
            pl.BlockSpec((1, SSD_INNER), const2),
            pl.BlockSpec((1, RWKV_INNER), const2),
            pl.BlockSpec((1, RWKV_INNER), const2),
            pl.BlockSpec((SSD_INNER, dm), const2),
            pl.BlockSpec((RET_INNER, dm), const2),
            pl.BlockSpec((RWKV_INNER, dm), const2),
            pl.BlockSpec((dm, dm), const2),
        ],
        out_specs=pl.BlockSpec((1, tm, dm), row),
        compiler_params=_cparams(2),
        name="merge_out",
    )(ssd_y, u_zdt, ret_f, ret_b, u_ret, rw_f, rw_b, u_gate, x, mod, nw1, ssd_nw, ln_w, ln_b,
      w_ssd_out, w_ret_out, w_rwkv_out, w_out)


def _mlp_kernel(x_ref, mod_ref, nw2_ref, nw3_ref, w1_ref, w2_ref, *rest, ff_tile, with_next):
    nb, tm, dm = x_ref.shape
    x = x_ref[...]
    mod = mod_ref[:, 0]
    y = _rms(x, nw2_ref[...])
    h = (y * (1.0 + mod[:, 4:5, :]) + mod[:, 3:4, :]).astype(BF16).reshape(nb * tm, dm)
    acc = jnp.zeros((nb * tm, dm), F32)
    for j in range(D_FF // ff_tile):
        hid = jnp.dot(h, w1_ref[:, j * ff_tile:(j + 1) * ff_tile], preferred_element_type=F32)
        hid = jnp.square(jnp.maximum(hid, 0.0)).astype(BF16)
        acc = acc + jnp.dot(hid, w2_ref[j * ff_tile:(j + 1) * ff_tile, :], preferred_element_type=F32)
    x_new = x + mod[:, 5:6, :] * _rms(acc.reshape(nb, tm, dm), nw3_ref[...])
    if with_next:
        nwn_ref, modn_ref, o_ref, h_ref = rest
        modn = modn_ref[:, 0]
        h_ref[...] = (_rms(x_new, nwn_ref[...]) * (1.0 + modn[:, 1:2, :]) + modn[:, 0:1, :]).astype(BF16)
    else:
        o_ref, = rest
    o_ref[...] = x_new


def mlp_block(x, mod, nw2, nw3, w1, w2, n_ctx_tiles, skip_tiles=0, next_norm=None, ff_tile=1024):
    b, ta, dm = x.shape
    tm = ROW_TILE
    rows = lambda i: (0, i, 0)
    const2 = lambda i: (0, 0)
    mod_spec = pl.BlockSpec((b, 1, SUBLANES, dm),
                            lambda i: (0, (i + skip_tiles >= n_ctx_tiles).astype(jnp.int32), 0, 0))
    resident = pl.Buffered(1)
    in_specs = [
        pl.BlockSpec((b, tm, dm), lambda i: (0, i + skip_tiles, 0)),
        mod_spec,
        pl.BlockSpec((1, dm), const2),
        pl.BlockSpec((1, dm), const2),
        pl.BlockSpec((dm, D_FF), const2, pipeline_mode=resident),
        pl.BlockSpec((D_FF, dm), const2, pipeline_mode=resident),
    ]
    args = [x, mod, nw2, nw3, w1, w2]
    out_rows = ta - skip_tiles * tm
    out_shape = jax.ShapeDtypeStruct((b, out_rows, dm), F32)
    out_specs = pl.BlockSpec((b, tm, dm), rows)
    if next_norm is not None:
        in_specs += [pl.BlockSpec((1, dm), const2), mod_spec]
        args += list(next_norm)
        out_shape = (out_shape, jax.ShapeDtypeStruct((b, out_rows, dm), BF16))
        out_specs = (out_specs, pl.BlockSpec((b, tm, dm), rows))
    return pl.pallas_call(
        functools.partial(_mlp_kernel, ff_tile=ff_tile, with_next=next_norm is not None),
        out_shape=out_shape,
        grid=(ta // tm - skip_tiles,),
        in_specs=in_specs,
        out_specs=out_specs,
        compiler_params=_cparams(1),
        name="mlp",
    )(*args)


def _rope_tables(n_ctx, n_lat):
    rows = n_lat // GRID_W
    row = np.repeat(np.arange(rows), GRID_W).astype(np.float32)
    col = np.tile(np.arange(GRID_W), rows).astype(np.float32)
    n_freq = RET_KEY_DIM // 4
    inv = jnp.power(ROPE_BASE, -jnp.arange(n_freq, dtype=F32) / n_freq)
    ang = jnp.concatenate([jnp.asarray(row)[:, None] * inv, jnp.asarray(col)[:, None] * inv], axis=-1)
    cos = jnp.concatenate([jnp.ones((n_ctx, RET_KEY_DIM // 2), F32), jnp.cos(ang)], axis=0)
    sin = jnp.concatenate([jnp.zeros((n_ctx, RET_KEY_DIM // 2), F32), jnp.sin(ang)], axis=0)
    cos_h = jnp.concatenate([cos, cos], axis=1)
    sin_h = jnp.concatenate([-sin, sin], axis=1)
    reps = 2 * RET_HEADS
    return jnp.tile(cos_h, (1, reps)), jnp.tile(sin_h, (1, reps))


def _pad_lanes(a, width=LANES):
    return jnp.pad(a, [(0, 0)] * (a.ndim - 1) + [(0, width - a.shape[-1])])


def kernel(x, c, ctx, c_ctx, norm_w, ada_w, ada_b, w_in, ssd_conv_w, ssd_conv_b, ssd_dt_bias, ssd_a_log,
           ssd_d, ssd_norm_w, ret_decay, rwkv_mix, rwkv_w0, rwkv_w2, rwkv_a0, rwkv_a2, rwkv_g2, rwkv_k_k,
           rwkv_k_a, rwkv_r_k, rwkv_lnx_w, rwkv_lnx_b, w_ssd_out, w_ret_out, w_rwkv_out, w_out, mlp_w1, mlp_w2):
    b, n_lat, dm = x.shape
    n_ctx = ctx.shape[1]
    ta = n_ctx + n_lat
    depth = norm_w.shape[0]
    n_ctx_tiles = n_ctx // ROW_TILE
    assert n_ctx % ROW_TILE == 0 and n_lat % ROW_TILE == 0 and b + 1 <= SUBLANES

    hexp = np.zeros((LANES, SSD_INNER), np.float32)
    for h in range(SSD_HEADS):
        hexp[h, h * SSD_HEAD_DIM:(h + 1) * SSD_HEAD_DIM] = 1.0
    hexp = jnp.asarray(hexp, BF16)
    cosx, sinx = _rope_tables(n_ctx, n_lat)

    cond = jnp.zeros((SUBLANES, dm), F32).at[:b].set(c).at[b].set(c_ctx)
    mod_all = modulation_all(cond, ada_w, ada_b)

    sizes = (N_BRANCH * dm, SSD_INNER, SSD_CONV_DIM, 2 * SSD_HEADS, RET_QK, RET_QK, RET_INNER, RET_INNER, RWKV_IN)
    offs = np.concatenate([[0], np.cumsum(sizes)])
    perm = np.concatenate([np.concatenate([np.arange(0, RET_KEY_DIM, 2), np.arange(1, RET_KEY_DIM, 2)]) + h * RET_KEY_DIM
                           for h in range(RET_HEADS)])

    xall = jnp.concatenate([ctx, x], axis=1)
    mods = []
    for l in range(depth):
        m = mod_all[l].reshape(SUBLANES, 6, dm)
        m = jnp.pad(m, ((0, 0), (0, SUBLANES - 6), (0, 0)))
        mods.append(jnp.stack([jnp.broadcast_to(m[b], (b,) + m.shape[1:]), m[:b]], axis=1))
    h = norm_modulate(xall, norm_w[0][0:1], mods[0], n_ctx_tiles)
    tm_mm = next(t for t in (1536, 1024, ROW_TILE) if (b * ta) % t == 0)
    for l in range(depth):
        wl = w_in[l]
        seg = [wl[:, offs[j]:offs[j + 1]] for j in range(len(sizes))]
        w_gate = seg[0].astype(BF16)
        w_zdt = jnp.concatenate([seg[1], _pad_lanes(seg[3][:, :SSD_HEADS]), _pad_lanes(seg[3][:, SSD_HEADS:])],
                                axis=1).astype(BF16)
        w_xbc = seg[2].astype(BF16)
        w_ret = jnp.concatenate([seg[4][:, perm], seg[5][:, perm], seg[6], seg[7]], axis=1).astype(BF16)
        w_rw = seg[8].astype(BF16)

        mod = mods[l]
        nw = norm_w[l]
        h = h.reshape(b * ta, dm)
        u_gate = matmul(h, w_gate, tm_mm, 1536).reshape(b, ta, -1)
        u_zdt = matmul(h, w_zdt, tm_mm, w_zdt.shape[1]).reshape(b, ta, -1)
        u_xbc = matmul(h, w_xbc, tm_mm, SSD_CONV_DIM).reshape(b, ta, -1)
        u_ret = matmul(h, w_ret, tm_mm, w_ret.shape[1]).reshape(b, ta, -1)
        u_rw = matmul(h, w_rw, tm_mm, RWKV_IN).reshape(b, ta, -1)

        conv_w = jnp.pad(ssd_conv_w[l], ((0, SUBLANES - SSD_CONV), (0, 0)))
        dskip = jnp.repeat(ssd_d[l], SSD_HEAD_DIM)[None, :]
        ssd_y = ssd_scan(u_xbc, u_zdt, conv_w, ssd_conv_b[l][None, :],
                         _pad_lanes(ssd_dt_bias[l])[:, None, :], _pad_lanes(ssd_a_log[l])[:, None, :],
                         dskip, hexp, n_ctx)
        ret_f, ret_b = ret_scan(u_ret, cosx, sinx, _pad_lanes(ret_decay[l])[:, None, :], n_ctx)
        rw_f, rw_b = rwkv_scan(u_rw, rwkv_mix[l][None, :], rwkv_w0[l][:, None, :], rwkv_w2[l], rwkv_a0[l][:, None, :],
                         rwkv_a2[l], rwkv_g2[l].astype(BF16), rwkv_k_k[l][None, :], rwkv_k_a[l][None, :],
                         rwkv_r_k[l].reshape(1, RWKV_INNER), n_ctx)

        xall = merge_out(ssd_y, u_zdt, ret_f, ret_b, u_ret, rw_f, rw_b, u_gate, xall, mod, nw[1:2],
                         ssd_norm_w[l][None, :], rwkv_lnx_w[l][None, :], rwkv_lnx_b[l][None, :],
                         w_ssd_out[l].astype(BF16), w_ret_out[l].astype(BF16), w_rwkv_out[l].astype(BF16),
                         w_out[l].astype(BF16), n_ctx_tiles)
        w1, w2 = mlp_w1[l].astype(BF16), mlp_w2[l].astype(BF16)
        if l + 1 < depth:
            xall, h = mlp_block(xall, mod, nw[2:3], nw[3:4], w1, w2, n_ctx_tiles,
                                next_norm=(norm_w[l + 1][0:1], mods[l + 1]))
        else:
            xall = mlp_block(xall, mod, nw[2:3], nw[3:4], w1, w2, n_ctx_tiles, skip_tiles=n_ctx_tiles)
    return xall
```

```python
import functools
import math

import numpy as np
import jax
import jax.numpy as jnp
from jax import lax
from jax.experimental import pallas as pl
from jax.experimental.pallas import tpu as pltpu

F32 = jnp.float32
BF16 = jnp.bfloat16
HIGHEST = lax.Precision.HIGHEST

D_MODEL = 1024
GRID_W = 64
EPS = 1e-6
N_BRANCH = 3

SSD_HEADS = 16
SSD_HEAD_DIM = 64
SSD_INNER = SSD_HEADS * SSD_HEAD_DIM
SSD_GROUPS = 2
SSD_STATE = 128
SSD_CONV = 5
SSD_CHUNK = 128
SSD_BLOCK = 2 * SSD_CHUNK
SSD_CONV_DIM = SSD_INNER + 2 * SSD_GROUPS * SSD_STATE
SSD_GROUP_W = SSD_INNER // SSD_GROUPS

RET_HEADS = 4
RET_KEY_DIM = 64
RET_VAL_DIM = 128
RET_QK = RET_HEADS * RET_KEY_DIM
RET_INNER = RET_HEADS * RET_VAL_DIM
RET_CHUNK = 128
RET_BLOCK = 2 * RET_CHUNK
ROPE_BASE = 10000.0

RWKV_HEADS = 8
RWKV_HEAD_DIM = 64
RWKV_INNER = RWKV_HEADS * RWKV_HEAD_DIM
RWKV_DECAY_LORA = 64
RWKV_AAA_LORA = 64
RWKV_GATE_LORA = 128
RWKV_IN = 3 * RWKV_INNER + RWKV_DECAY_LORA + RWKV_AAA_LORA + RWKV_GATE_LORA
RWKV_LN_EPS = 64e-5
RWKV_CHUNK = 64

D_FF = 4 * D_MODEL

SUBLANES = 8
LANES = 128
VMEM_LIMIT_BYTES = 56 * 1024 * 1024

ROW_TILE = 256
NEG_BIG = -1e30


def _cparams(n_axes):
    return pltpu.CompilerParams(dimension_semantics=("arbitrary",) * n_axes,
                                vmem_limit_bytes=VMEM_LIMIT_BYTES)


def _sigmoid(x):
    return 0.5 * jnp.tanh(0.5 * x) + 0.5


def _silu(x):
    return x * _sigmoid(x)


def _softplus(x):
    return jnp.maximum(x, 0.0) + jnp.log(1.0 + jnp.exp(-jnp.abs(x)))


def _split3(x):
    x1 = x.astype(BF16)
    r1 = x - x1.astype(F32)
    x2 = r1.astype(BF16)
    r2 = r1 - x2.astype(F32)
    return x1, x2, r2.astype(BF16)


def _dot_nt(a, b, precision=None):
    return lax.dot_general(a, b, (((1,), (1,)), ((), ())), preferred_element_type=F32, precision=precision)


def _dot_tn(a, b, precision=None):
    return lax.dot_general(a, b, (((0,), (0,)), ((), ())), preferred_element_type=F32, precision=precision)


def _chunk_of(d, i, ncx, nc):
    bwd = jnp.where(i < ncx, ncx - 1 - i, ncx + nc - 1 - i)
    return jnp.where(d == 0, i, bwd)


def _seg_edges(c, ncx, nc):
    seg_start = jnp.logical_or(c == 0, c == ncx)
    seg_end = jnp.logical_or(c == ncx - 1, c == nc - 1)
    return seg_start, seg_end


def _tri_mask(n, d, strict, reps=1):
    row = lax.broadcasted_iota(jnp.int32, (n, n * reps), 0)
    col = lax.broadcasted_iota(jnp.int32, (n, n * reps), 1) % n
    lead = jnp.where(d == 0, row - col, col - row)
    return lead > 0 if strict else lead >= 0


def _mod_kernel(c_ref, w_ref, b_ref, o_ref):
    s = _silu(c_ref[...])
    o_ref[0] = jnp.dot(s, w_ref[0], preferred_element_type=F32, precision=HIGHEST) + b_ref[0]


def modulation_all(cond, ada_w, ada_b):
    depth = ada_w.shape[0]
    tn = 1536
    return pl.pallas_call(
        _mod_kernel,
        out_shape=jax.ShapeDtypeStruct((depth, SUBLANES, 6 * D_MODEL), F32),
        grid=(depth, 6 * D_MODEL // tn),
        in_specs=[pl.BlockSpec((SUBLANES, D_MODEL), lambda l, j: (0, 0)),
                  pl.BlockSpec((1, D_MODEL, tn), lambda l, j: (l, 0, j)),
                  pl.BlockSpec((1, 1, tn), lambda l, j: (l, 0, j))],
        out_specs=pl.BlockSpec((1, SUBLANES, tn), lambda l, j: (l, 0, j)),
        compiler_params=_cparams(2),
        name="modulation",
    )(cond, ada_w, ada_b.reshape(depth, 1, 6 * D_MODEL))


def _rms(x, w):
    return x * lax.rsqrt(jnp.mean(x * x, axis=-1, keepdims=True) + EPS) * w


def _normmod_kernel(x_ref, nw_ref, mod_ref, h_ref):
    y = _rms(x_ref[0], nw_ref[...])
    shift = mod_ref[0, 0, 0:1, :]
    scale = mod_ref[0, 0, 1:2, :]
    h_ref[0] = (y * (1.0 + scale) + shift).astype(BF16)


def norm_modulate(x, nw, mod, n_ctx_tiles):
    b, ta, dm = x.shape
    return pl.pallas_call(
        _normmod_kernel,
        out_shape=jax.ShapeDtypeStruct((b, ta, dm), BF16),
        grid=(b, ta // ROW_TILE),
        in_specs=[pl.BlockSpec((1, ROW_TILE, dm), lambda bi, i: (bi, i, 0)),
                  pl.BlockSpec((1, dm), lambda bi, i: (0, 0)),
                  pl.BlockSpec((1, 1, SUBLANES, dm), lambda bi, i: (bi, (i >= n_ctx_tiles).astype(jnp.int32), 0, 0))],
        out_specs=pl.BlockSpec((1, ROW_TILE, dm), lambda bi, i: (bi, i, 0)),
        compiler_params=_cparams(2),
        name="norm_modulate",
    )(x, nw, mod)


def _mm_kernel(a_ref, w_ref, o_ref):
    o_ref[...] = jnp.dot(a_ref[...], w_ref[...], preferred_element_type=F32).astype(o_ref.dtype)


def matmul(a, w, tm, tn, out_dtype=F32):
    r, k = a.shape
    n = w.shape[1]
    assert r % tm == 0 and n % tn == 0
    return pl.pallas_call(
        _mm_kernel,
        out_shape=jax.ShapeDtypeStruct((r, n), out_dtype),
        grid=(n // tn, r // tm),
        in_specs=[pl.BlockSpec((tm, k), lambda j, i: (i, 0)),
                  pl.BlockSpec((k, tn), lambda j, i: (0, j))],
        out_specs=pl.BlockSpec((tm, tn), lambda j, i: (i, j)),
        compiler_params=_cparams(2),
        name="in_proj",
    )(a, w)


def _ssd_kernel(xbc_ref, xp_ref, xn_ref, dt_ref, cw_ref, cb_ref, dtb_ref, alog_ref, dskip_ref,
                hexp_ref, y_ref, s_ref, ext_ref, act_ref, *, ncx, nc):
    d = pl.program_id(1)
    i = pl.program_id(2)
    c = _chunk_of(d, i, ncx, nc)
    seg_start, seg_end = _seg_edges(c, ncx, nc)
    lc = SSD_CHUNK
    lb = SSD_BLOCK
    nch = lb // lc
    hb = SUBLANES

    @pl.when(i == 0)
    def _():
        s_ref[...] = jnp.zeros_like(s_ref)

    @pl.when(d == 0)
    def _():
        ext_ref[0:hb, :] = jnp.where(seg_start, 0.0, xp_ref[0])
        ext_ref[hb:hb + lb, :] = xbc_ref[0]
        ext_ref[hb + lb:hb + lb + hb, :] = jnp.where(seg_end, 0.0, xn_ref[0])
        pad = SSD_CONV // 2
        acc = ext_ref[hb - pad:hb - pad + lb, :] * cw_ref[0:1, :]
        for j in range(1, SSD_CONV):
            acc = acc + ext_ref[hb - pad + j:hb - pad + j + lb, :] * cw_ref[j:j + 1, :]
        act = _silu(acc + cb_ref[...])
        act_ref[c] = act.astype(BF16)
        y_ref[0, 0] = act[:, :SSD_INNER] * dskip_ref[...]

    @pl.when(d != 0)
    def _():
        y_ref[0, 0] = jnp.zeros(y_ref.shape[2:], F32)

    tri = _tri_mask(lc, d, strict=False)
    hg = SSD_HEADS // SSD_GROUPS
    low_half = lax.broadcasted_iota(jnp.int32, (lc, 2 * SSD_HEAD_DIM), 1) < SSD_HEAD_DIM
    neg_a = -jnp.exp(alog_ref[0])

    local = []
    for k in range(nch):
        rs = pl.ds(pl.multiple_of(jnp.where(d == 0, k, nch - 1 - k) * lc, lc), lc)
        xbc = act_ref[c, rs, :]
        bm = xbc[:, SSD_INNER:SSD_INNER + SSD_GROUPS * SSD_STATE]
        cm = xbc[:, SSD_INNER + SSD_GROUPS * SSD_STATE:]
        dt = _softplus(dt_ref[0, rs, :] + dtb_ref[0])
        la = dt * neg_a
        ac3 = jnp.dot(tri.astype(BF16), jnp.concatenate(_split3(la), axis=1), preferred_element_type=F32)
        acum = ac3[:, :LANES] + ac3[:, LANES:2 * LANES] + ac3[:, 2 * LANES:]
        atot = jnp.sum(la, axis=0, keepdims=True)
        acum_t = acum.T
        etot8 = jnp.broadcast_to(jnp.exp(atot), (SUBLANES, LANES))
        parts = jnp.concatenate(_split2(jnp.exp(acum)) + _split2(etot8), axis=0)
        ex = jnp.dot(parts, hexp_ref[...], preferred_element_type=F32)
        eacx = ex[0:lc] + ex[lc:2 * lc]
        etotx = (ex[2 * lc:2 * lc + SUBLANES] + ex[2 * lc + SUBLANES:])[0:1, :]
        parts_b = jnp.concatenate([dt.astype(BF16), jnp.exp(atot - acum).astype(BF16)], axis=0)
        ex_b = jnp.dot(parts_b, hexp_ref[...], preferred_element_type=F32).astype(BF16)
        vb = xbc[:, :SSD_INNER] * ex_b[0:lc]
        vend = vb * ex_b[lc:2 * lc]
        groups = []
        for g in range(SSD_GROUPS):
            gs = slice(g * SSD_STATE, (g + 1) * SSD_STATE)
            gw = slice(g * SSD_GROUP_W, (g + 1) * SSD_GROUP_W)
            cg = cm[:, gs]
            bg = bm[:, gs]
            scores = _dot_nt(cg, bg)
            y_in = []
            for hp in range(hg // 2):
                a_pair = []
                for h in (g * hg + 2 * hp, g * hg + 2 * hp + 1):
                    diff = acum[:, h:h + 1] - acum_t[h:h + 1, :]
                    a_pair.append((scores * jnp.exp(jnp.where(tri, diff, NEG_BIG))).astype(BF16))
                ps = slice((g * hg + 2 * hp) * SSD_HEAD_DIM, (g * hg + 2 * hp + 2) * SSD_HEAD_DIM)
                v_pair = vb[:, ps]
                w_pair = jnp.concatenate([jnp.where(low_half, v_pair, 0.0), jnp.where(low_half, 0.0, v_pair)],
                                         axis=0)
                y_in.append(jnp.dot(jnp.concatenate(a_pair, axis=1), w_pair, preferred_element_type=F32))
            groups.append(dict(cg=cg, y_in=jnp.concatenate(y_in, axis=1), eac=eacx[:, gw], etot=etotx[:, gw],
                               cs=_dot_tn(bg, vend[:, gw])))
        local.append((rs, groups))

    for g in range(SSD_GROUPS):
        gw = slice(g * SSD_GROUP_W, (g + 1) * SSD_GROUP_W)
        st = s_ref[g]
        for rs, groups in local:
            q = groups[g]
            y_ref[0, 0, rs, gw] += q["y_in"] + jnp.dot(q["cg"], st.astype(BF16),
                                                       preferred_element_type=F32) * q["eac"]
            st = st * q["etot"] + q["cs"]
        s_ref[g] = st


def ssd_scan(u_xbc, u_zdt, conv_w, conv_b, dt_bias, a_log, dskip, hexp, ncx_tokens):
    b, ta, _ = u_xbc.shape
    lc = SSD_BLOCK
    nc = ta // lc
    ncx = ncx_tokens // lc
    nb = lc // SUBLANES
    nblk = ta // SUBLANES
    ch = functools.partial(_chunk_of, ncx=ncx, nc=nc)
    kern = functools.partial(_ssd_kernel, ncx=ncx, nc=nc)
    zdt_blk0 = SSD_INNER // LANES
    return pl.pallas_call(
        kern,
        out_shape=jax.ShapeDtypeStruct((2, b, ta, SSD_INNER), F32),
        grid=(b, 2, nc),
        in_specs=[
            pl.BlockSpec((1, lc, SSD_CONV_DIM), lambda bi, d, i: (bi, i * (1 - d), 0)),
            pl.BlockSpec((1, SUBLANES, SSD_CONV_DIM), lambda bi, d, i: (bi, jnp.maximum(i * nb - 1, 0) * (1 - d), 0)),
            pl.BlockSpec((1, SUBLANES, SSD_CONV_DIM),
                         lambda bi, d, i: (bi, jnp.minimum((i + 1) * nb, nblk - 1) * (1 - d), 0)),
            pl.BlockSpec((1, lc, LANES), lambda bi, d, i: (bi, ch(d, i), zdt_blk0 + d)),
            pl.BlockSpec((SUBLANES, SSD_CONV_DIM), lambda bi, d, i: (0, 0)),
            pl.BlockSpec((1, SSD_CONV_DIM), lambda bi, d, i: (0, 0)),
            pl.BlockSpec((1, 1, LANES), lambda bi, d, i: (d, 0, 0)),
            pl.BlockSpec((1, 1, LANES), lambda bi, d, i: (d, 0, 0)),
            pl.BlockSpec((1, SSD_INNER), lambda bi, d, i: (0, 0)),
            pl.BlockSpec((LANES, SSD_INNER), lambda bi, d, i: (0, 0)),
        ],
        out_specs=pl.BlockSpec((1, 1, lc, SSD_INNER), lambda bi, d, i: (d, bi, ch(d, i), 0)),
        scratch_shapes=[pltpu.VMEM((SSD_GROUPS, SSD_STATE, SSD_GROUP_W), F32),
                        pltpu.VMEM((lc + 2 * SUBLANES, SSD_CONV_DIM), F32),
                        pltpu.VMEM((nc, lc, SSD_CONV_DIM), BF16)],
        compiler_params=_cparams(3),
        name="ssd_scan",
    )(u_xbc, u_xbc, u_xbc, u_zdt, conv_w, conv_b, dt_bias, a_log, dskip, hexp)


def _ret_kernel(qkf_ref, vf_ref, cosf_ref, sinf_ref, qkb_ref, vb_ref, cosb_ref, sinb_ref, dec_ref,
                yf_ref, yb_ref, s_ref):
    i = pl.program_id(1)
    lc = RET_CHUNK
    lb = RET_BLOCK
    nch = lb // lc

    @pl.when(i == 0)
    def _():
        s_ref[...] = jnp.zeros_like(s_ref)

    in_refs = ((qkf_ref, vf_ref, cosf_ref, sinf_ref), (qkb_ref, vb_ref, cosb_ref, sinb_ref))
    out_refs = (yf_ref, yb_ref)
    half = RET_KEY_DIM // 2
    width = 2 * RET_QK
    lane = lax.broadcasted_iota(jnp.int32, (lb, width), 1)
    row = lax.broadcasted_iota(jnp.int32, (lc, lc), 0)
    col = lax.broadcasted_iota(jnp.int32, (lc, lc), 1)
    dist = jnp.abs(row - col).astype(F32)
    tri = (col <= row, col >= row)
    t_idx = lax.broadcasted_iota(jnp.int32, (lc, RET_VAL_DIM), 0).astype(F32)
    pos = (t_idx, lc - 1.0 - t_idx)
    q, k_t, v, lg = [], [], [], []
    for d in (0, 1):
        qk_ref, v_ref, cos_ref, sin_ref = in_refs[d]
        qk = qk_ref[0]
        swapped = jnp.where((lane % RET_KEY_DIM) < half,
                            pltpu.roll(qk, width - half, axis=1), pltpu.roll(qk, half, axis=1))
        qk = qk * cos_ref[...] + swapped * sin_ref[...]
        q.append(qk[:, :RET_QK].astype(BF16))
        k_t.append((qk[:, RET_QK:] * (RET_KEY_DIM ** -0.5)).T.astype(BF16))
        v.append(v_ref[0])
        lg.append(-_softplus(-dec_ref[d]))

    items = [(d, c, h) for d in (0, 1) for c in range(nch) for h in range(RET_HEADS)]
    n_it = range(len(items))
    idx = {it: n for n, it in enumerate(items)}
    rs = [slice(c * lc, (c + 1) * lc) for d, c, h in items]
    ks = [slice(h * RET_KEY_DIM, (h + 1) * RET_KEY_DIM) for d, c, h in items]
    vs = [slice(h * RET_VAL_DIM, (h + 1) * RET_VAL_DIM) for d, c, h in items]
    lgh = [lg[d][:, h:h + 1] for d, c, h in items]
    qh = [q[d][rs[n], ks[n]] for n, (d, c, h) in enumerate(items)]
    kh_t = [k_t[d][ks[n], rs[n]] for n, (d, c, h) in enumerate(items)]
    vh = [v[d][rs[n], vs[n]] for n, (d, c, h) in enumerate(items)]
    scores = [(jnp.dot(qh[n], kh_t[n], preferred_element_type=F32)
               * jnp.exp(jnp.where(tri[items[n][0]], dist * lgh[n], NEG_BIG))).astype(BF16) for n in n_it]
    y_in = [jnp.dot(scores[n], vh[n].astype(BF16), preferred_element_type=F32) for n in n_it]
    vend = [(vh[n] * jnp.exp((lc - 1.0 - pos[items[n][0]]) * lgh[n])).astype(BF16) for n in n_it]
    cs = [jnp.dot(kh_t[n], vend[n], preferred_element_type=F32) for n in n_it]
    heads = [(d, h) for d in (0, 1) for h in range(RET_HEADS)]
    st = {dh: s_ref[dh[0], dh[1]] for dh in heads}
    for k in range(nch):
        sel = [idx[(d, (k if d == 0 else nch - 1 - k), h)] for d, h in heads]
        y_st = [jnp.dot(qh[n], st[(items[n][0], items[n][2])].astype(BF16), preferred_element_type=F32)
                for n in sel]
        for m, n in enumerate(sel):
            d, c, h = items[n]
            out_refs[d][0, rs[n], vs[n]] = y_in[n] + y_st[m] * jnp.exp((pos[d] + 1.0) * lgh[n])
        for n in sel:
            d, c, h = items[n]
            st[(d, h)] = st[(d, h)] * jnp.exp(lc * lgh[n]) + cs[n]
    for d, h in heads:
        s_ref[d, h] = st[(d, h)]


def ret_scan(u_ret, cosx, sinx, ret_decay, ncx_tokens):
    b, ta, _ = u_ret.shape
    lc = RET_BLOCK
    nc = ta // lc
    ncx = ncx_tokens // lc
    fwd = lambda i: i
    bwd = lambda i: _chunk_of(1, i, ncx, nc)

    def specs(ch):
        return [pl.BlockSpec((1, lc, 2 * RET_QK), lambda bi, i: (bi, ch(i), 0)),
                pl.BlockSpec((1, lc, RET_INNER), lambda bi, i: (bi, ch(i), 1)),
                pl.BlockSpec((lc, 2 * RET_QK), lambda bi, i: (ch(i), 0)),
                pl.BlockSpec((lc, 2 * RET_QK), lambda bi, i: (ch(i), 0))]

    return pl.pallas_call(
        _ret_kernel,
        out_shape=(jax.ShapeDtypeStruct((b, ta, RET_INNER), F32), jax.ShapeDtypeStruct((b, ta, RET_INNER), F32)),
        grid=(b, nc),
        in_specs=specs(fwd) + specs(bwd) + [pl.BlockSpec((2, 1, LANES), lambda bi, i: (0, 0, 0))],
        out_specs=(pl.BlockSpec((1, lc, RET_INNER), lambda bi, i: (bi, i, 0)),
                   pl.BlockSpec((1, lc, RET_INNER), lambda bi, i: (bi, bwd(i), 0))),
        scratch_shapes=[pltpu.VMEM((2, RET_HEADS, RET_KEY_DIM, RET_VAL_DIM), F32)],
        compiler_params=_cparams(2),
        name="ret_scan",
    )(u_ret, u_ret, cosx, sinx, u_ret, u_ret, cosx, sinx, ret_decay)


RWKV_PAIRS = RWKV_HEADS // 2
RWKV_BLOCK = 4 * RWKV_CHUNK


def _bdiag(tile, bd2):
    return jnp.where(bd2, jnp.concatenate([tile, tile], axis=0), 0.0)


def _split2(x):
    hi = x.astype(BF16)
    return hi, (x - hi.astype(F32)).astype(BF16)


def _head_sums(xs, pair_ones):
    rows = xs[0].shape[0]
    parts = jnp.concatenate([p for x in xs for p in _split2(x)], axis=0)
    blk = parts.shape[0]
    n_pair = parts.shape[1] // LANES
    stacked = jnp.concatenate([parts[:, j * LANES:(j + 1) * LANES] for j in range(n_pair)], axis=0)
    y = jnp.dot(stacked, pair_ones, preferred_element_type=F32)
    full = jnp.concatenate([y[j * blk:(j + 1) * blk] for j in range(n_pair)], axis=1)
    return [full[2 * i * rows:(2 * i + 1) * rows] + full[(2 * i + 1) * rows:(2 * i + 2) * rows]
            for i in range(len(xs))]


def _pair_ones():
    hd = RWKV_HEAD_DIM
    return jnp.where((lax.broadcasted_iota(jnp.int32, (LANES, LANES), 0) < hd)
                     == (lax.broadcasted_iota(jnp.int32, (LANES, LANES), 1) < hd), 1.0, 0.0).astype(BF16)


def _dot3(a, b):
    a1, a2 = _split2(a)
    b1, b2 = _split2(b)
    n = a.shape[0]
    o = jnp.dot(jnp.concatenate([a1, a2], axis=0), b1, preferred_element_type=F32)
    return o[:n] + o[n:] + jnp.dot(a1, b2, preferred_element_type=F32)


def _pair_mm(lhs_splits, w_split, bd2):
    hi = jnp.concatenate([s[0] for s in lhs_splits], axis=0)
    lo = jnp.concatenate([s[1] for s in lhs_splits], axis=0)
    n = hi.shape[0]
    o = jnp.dot(jnp.concatenate([hi, lo], axis=0), _bdiag(w_split[0], bd2), preferred_element_type=F32)
    tot = o[:n] + o[n:] + jnp.dot(hi, _bdiag(w_split[1], bd2), preferred_element_type=F32)
    return [tot[i * RWKV_HEAD_DIM:(i + 1) * RWKV_HEAD_DIM] for i in range(len(lhs_splits))]


def _pair_rows(x_t, j, low_half):
    base = j * LANES
    return jnp.where(low_half, x_t[base:base + RWKV_HEAD_DIM], x_t[base + RWKV_HEAD_DIM:base + LANES])


def _rwkv_prep(d, u, prow, nrow, mix, w0, w2, a0, a2, kkw, ka, rk, bd):
    lc = RWKV_CHUNK
    lb = u.shape[0]
    ni = RWKV_INNER
    half_mix = 0.5 * mix
    nbr = pltpu.roll(u, 1, axis=0) + pltpu.roll(u, lb - 1, axis=0)
    sub = lax.broadcasted_iota(jnp.int32, (SUBLANES, 1), 0)
    fix_first = jnp.where(sub == 0, prow - u[lb - 1:lb, :], 0.0)
    fix_last = jnp.where(sub == SUBLANES - 1, nrow - u[0:1, :], 0.0)
    nbr = jnp.concatenate([nbr[:SUBLANES] + fix_first, nbr[SUBLANES:lb - SUBLANES],
                           nbr[lb - SUBLANES:] + fix_last], axis=0)
    u = (1.0 - mix) * u + half_mix * nbr
    r, k, v = u[:, 0:ni], u[:, ni:2 * ni], u[:, 2 * ni:3 * ni]
    o = 3 * ni
    w_lo = u[:, o:o + RWKV_DECAY_LORA]
    a_lo = u[:, o + RWKV_DECAY_LORA:o + RWKV_DECAY_LORA + RWKV_AAA_LORA]
    g_lo = u[:, o + RWKV_DECAY_LORA + RWKV_AAA_LORA:]

    logw = -math.exp(-0.5) * _sigmoid(w0 + _dot3(jnp.tanh(w_lo), w2))
    a_gate = _sigmoid(a0 + _dot3(a_lo, a2))
    kk = k * kkw
    kd = k * (1.0 + (a_gate - 1.0) * ka)
    kk_ss, rk_sum = _head_sums([kk * kk, r * kd * rk], bd)
    kk = kk * jnp.minimum(lax.rsqrt(kk_ss), 1e12)
    bvec = kk * a_gate
    bonus = rk_sum * v

    row = lax.broadcasted_iota(jnp.int32, (lb, lb), 0)
    col = lax.broadcasted_iota(jnp.int32, (lb, lb), 1)
    lead = (row - col) if d == 0 else (col - row)
    incl = jnp.where(row // lc == col // lc, lead, -1) >= 0
    l1, l2, l3 = _split3(logw)
    cw3 = jnp.dot(incl.astype(BF16), jnp.concatenate([l1, l2, l3], axis=1), preferred_element_type=F32)
    cw = cw3[:, 0:ni] + cw3[:, ni:2 * ni] + cw3[:, 2 * ni:3 * ni]

    def dup_t(x):
        return jnp.concatenate([x, x], axis=0).T

    chunks = []
    for c in range(lb // lc):
        rs = slice(c * lc, (c + 1) * lc)
        logw_c, cw_c = logw[rs], cw[rs]
        ctot = jnp.sum(logw_c, axis=0, keepdims=True)
        e_neg = jnp.exp(-cw_c)
        e_end = jnp.exp(ctot - cw_c)
        chunks.append(dict(
            a_t=-kk[rs] * jnp.exp(cw_c - logw_c), r_t=r[rs] * jnp.exp(cw_c), v=v[rs],
            bn_t=dup_t(bvec[rs] * e_neg), kn_t=dup_t(kd[rs] * e_neg),
            be_t=dup_t(bvec[rs] * e_end), ke_t=dup_t(kd[rs] * e_end),
            wtot_col=jnp.exp(jnp.sum(logw_c.T, axis=1, keepdims=True))))
    return chunks, bonus, g_lo


def _rwkv_kernel(uf_ref, ufp_ref, ufn_ref, ub_ref, ubp_ref, ubn_ref, mix_ref, w0_ref, w2_ref, a0_ref, a2_ref,
                 g2_ref, kk_ref, ka_ref, rk_ref, of_ref, ob_ref, s_ref, *, ncx, nc):
    i = pl.program_id(1)
    lc = RWKV_CHUNK
    ni = RWKV_INNER
    hd = RWKV_HEAD_DIM
    chunk = (i, _chunk_of(1, i, ncx, nc))

    @pl.when(i == 0)
    def _():
        s_ref[...] = jnp.zeros_like(s_ref)

    bd2 = ((lax.broadcasted_iota(jnp.int32, (LANES, LANES), 0) < hd)
           == (lax.broadcasted_iota(jnp.int32, (LANES, LANES), 1) < hd))
    pair_ones = jnp.where(bd2, 1.0, 0.0).astype(BF16)
    u_refs = ((uf_ref, ufp_ref, ufn_ref), (ub_ref, ubp_ref, ubn_ref))
    out_refs = (of_ref, ob_ref)
    prep = []
    for d in (0, 1):
        seg_start, seg_end = _seg_edges(chunk[d], ncx, nc)
        u_ref, up_ref, un_ref = u_refs[d]
        prow = jnp.where(seg_start, 0.0, up_ref[0, SUBLANES - 1:SUBLANES, :])
        nrow = jnp.where(seg_end, 0.0, un_ref[0, 0:1, :])
        chunks, bonus, g_lo = _rwkv_prep(d, u_ref[0], prow, nrow, mix_ref[...], w0_ref[d], w2_ref[d], a0_ref[d],
                                         a2_ref[...], kk_ref[...], ka_ref[...], rk_ref[...], pair_ones)
        out_refs[d][0, :, ni:2 * ni] = bonus
        if d == 0:
            of_ref[0, :, 2 * ni:3 * ni] = jnp.dot(_sigmoid(g_lo).astype(BF16), g2_ref[...],
                                                  preferred_element_type=F32)
        prep.append(chunks)

    row4 = lax.broadcasted_iota(jnp.int32, (lc, 2 * LANES), 0)
    col4 = lax.broadcasted_iota(jnp.int32, (lc, 2 * LANES), 1) % hd
    strict4 = (col4 < row4, col4 > row4)
    incl4 = (col4 <= row4, col4 >= row4)
    eye2 = (lax.broadcasted_iota(jnp.int32, (lc, LANES), 1) % hd
            == lax.broadcasted_iota(jnp.int32, (lc, LANES), 0)).astype(F32)
    low_half = lax.broadcasted_iota(jnp.int32, (hd, LANES), 1) < hd
    zeros_w = jnp.zeros((LANES, LANES), F32)

    def dot(a, b):
        return jnp.dot(a.astype(BF16), b.astype(BF16), preferred_element_type=F32)

    nch = RWKV_BLOCK // lc
    items = [(d, c, j) for d in (0, 1) for c in range(nch) for j in range(RWKV_PAIRS)]
    pl_ = {j: slice(j * LANES, (j + 1) * LANES) for j in range(RWKV_PAIRS)}
    n_it = range(len(items))
    a_t = [prep[d][c]["a_t"][:, pl_[j]] for d, c, j in items]
    r_t = [prep[d][c]["r_t"][:, pl_[j]] for d, c, j in items]
    v_w = [_bdiag(prep[d][c]["v"][:, pl_[j]], bd2) for d, c, j in items]
    w_p = [jnp.concatenate([jnp.where(bd2, prep[d][c]["bn_t"][pl_[j], :], 0.0),
                            jnp.where(bd2, prep[d][c]["kn_t"][pl_[j], :], 0.0)], axis=1) for d, c, j in items]
    p = [dot(jnp.concatenate([a_t[n], r_t[n]], axis=0), w_p[n]) for n in n_it]
    m_a = [jnp.where(strict4[items[n][0]], p[n][:lc], 0.0) for n in n_it]
    m_r = [jnp.where(incl4[items[n][0]], p[n][lc:], 0.0) for n in n_it]
    mv = [dot(m_a[n][:, LANES:], v_w[n]) for n in n_it]
    x = [m_a[n][:, :LANES] for n in n_it]
    t = [eye2 + x[n] for n in n_it]
    xs = [_split2(x[n]) for n in n_it]
    x = [_pair_mm([xs[n]], xs[n], bd2)[0] for n in n_it]
    for _ in range(int(math.log2(lc)) - 2):
        xs = [_split2(x[n]) for n in n_it]
        tx = [_pair_mm([_split2(t[n]), xs[n]], xs[n], bd2) for n in n_it]
        t = [t[n] + tx[n][0] for n in n_it]
        x = [tx[n][1] for n in n_it]
    t = [t[n] + _pair_mm([_split2(t[n])], _split2(x[n]), bd2)[0] for n in n_it]
    au = [dot(t[n], jnp.concatenate([_bdiag(a_t[n], bd2), _bdiag(mv[n], bd2)], axis=1)) for n in n_it]
    w2 = [jnp.concatenate([jnp.concatenate([_bdiag(au[n][:, :LANES], bd2), _bdiag(au[n][:, LANES:], bd2)], axis=1),
                           jnp.concatenate([zeros_w, v_w[n]], axis=1)], axis=0) for n in n_it]
    q_bk = [jnp.concatenate([_pair_rows(prep[d][c]["be_t"], j, low_half),
                             _pair_rows(prep[d][c]["ke_t"], j, low_half)], axis=1)
            for d, c, j in items]
    big = [dot(jnp.concatenate([q_bk[n], m_r[n]], axis=0), w2[n]) for n in n_it]
    wc = [jnp.where(low_half,
                    jnp.broadcast_to(prep[d][c]["wtot_col"][j * LANES:j * LANES + hd, :], (hd, LANES)),
                    jnp.broadcast_to(prep[d][c]["wtot_col"][j * LANES + hd:(j + 1) * LANES, :], (hd, LANES)))
          for d, c, j in items]

    idx = {it: n for n, it in enumerate(items)}
    heads = [(d, j) for d in (0, 1) for j in range(RWKV_PAIRS)]
    st = {dj: s_ref[dj[0], dj[1]] for dj in heads}
    for k in range(nch):
        sel = [(d, (k if d == 0 else nch - 1 - k), j) for d, j in heads]
        sy = [dot(jnp.concatenate([r_t[idx[it]] + big[idx[it]][lc:, :LANES], big[idx[it]][:lc, :LANES]], axis=0),
                  _bdiag(st[(it[0], it[2])], bd2)) for it in sel]
        for m, (d, c, j) in enumerate(sel):
            out_refs[d][0, c * lc:(c + 1) * lc, pl_[j]] = sy[m][:lc] + big[idx[(d, c, j)]][lc:, LANES:]
        for m, (d, c, j) in enumerate(sel):
            n = idx[(d, c, j)]
            st[(d, j)] = wc[n] * st[(d, j)] + sy[m][lc:] + big[n][:lc, LANES:]
    for d, j in heads:
        s_ref[d, j] = st[(d, j)]


def rwkv_scan(u_rwkv, mix, w0, w2, a0, a2, g2, k_k, k_a, r_k, ncx_tokens):
    b, ta, _ = u_rwkv.shape
    lc = RWKV_BLOCK
    nc = ta // lc
    ncx = ncx_tokens // lc
    nb = lc // SUBLANES
    nblk = ta // SUBLANES
    ni = RWKV_INNER
    kern = functools.partial(_rwkv_kernel, ncx=ncx, nc=nc)
    const2 = lambda bi, i: (0, 0)
    const3 = lambda bi, i: (0, 0, 0)
    fwd = lambda i: i
    bwd = lambda i: _chunk_of(1, i, ncx, nc)

    def u_specs(ch):
        return [pl.BlockSpec((1, lc, RWKV_IN), lambda bi, i: (bi, ch(i), 0)),
                pl.BlockSpec((1, SUBLANES, RWKV_IN), lambda bi, i: (bi, jnp.maximum(ch(i) * nb - 1, 0), 0)),
                pl.BlockSpec((1, SUBLANES, RWKV_IN), lambda bi, i: (bi, jnp.minimum((ch(i) + 1) * nb, nblk - 1), 0))]

    return pl.pallas_call(
        kern,
        out_shape=(jax.ShapeDtypeStruct((b, ta, 3 * ni), F32), jax.ShapeDtypeStruct((b, ta, 2 * ni), F32)),
        grid=(b, nc),
        in_specs=u_specs(fwd) + u_specs(bwd) + [
            pl.BlockSpec((1, RWKV_IN), const2),
            pl.BlockSpec((2, 1, ni), const3),
            pl.BlockSpec((2, RWKV_DECAY_LORA, ni), const3),
            pl.BlockSpec((2, 1, ni), const3),
            pl.BlockSpec((RWKV_AAA_LORA, ni), const2),
            pl.BlockSpec((RWKV_GATE_LORA, ni), const2),
            pl.BlockSpec((1, ni), const2),
            pl.BlockSpec((1, ni), const2),
            pl.BlockSpec((1, ni), const2),
        ],
        out_specs=(pl.BlockSpec((1, lc, 3 * ni), lambda bi, i: (bi, i, 0)),
                   pl.BlockSpec((1, lc, 2 * ni), lambda bi, i: (bi, bwd(i), 0))),
        scratch_shapes=[pltpu.VMEM((2, RWKV_PAIRS, RWKV_HEAD_DIM, LANES), F32)],
        compiler_params=_cparams(2),
        name="rwkv_scan",
    )(u_rwkv, u_rwkv, u_rwkv, u_rwkv, u_rwkv, u_rwkv, mix, w0, w2, a0, a2, g2, k_k, k_a, r_k)


def _merge_kernel(ssd_ref, z_ref, retf_ref, retb_ref, rg_ref, rwf_ref, rwb_ref, gate_ref, x_ref, mod_ref, nw_ref,
                  ssdnw_ref, lnw_ref, lnb_ref, wso_ref, wro_ref, wwo_ref, wo_ref, o_ref):
    ys = (ssd_ref[0, 0] + ssd_ref[1, 0]) * _silu(z_ref[0][:, :SSD_INNER])
    parts = []
    for g in range(SSD_GROUPS):
        yg = ys[:, g * SSD_GROUP_W:(g + 1) * SSD_GROUP_W]
        parts.append(yg * lax.rsqrt(jnp.mean(yg * yg, axis=-1, keepdims=True) + EPS))
    ys = jnp.concatenate(parts, axis=1) * ssdnw_ref[...]
    o_ssd = jnp.dot(ys.astype(BF16), wso_ref[...], preferred_element_type=F32)

    yr = retf_ref[0] + retb_ref[0]
    parts = []
    for h in range(RET_HEADS):
        yh = yr[:, h * RET_VAL_DIM:(h + 1) * RET_VAL_DIM]
        yc = yh - jnp.mean(yh, axis=-1, keepdims=True)
        parts.append(yc * lax.rsqrt(jnp.mean(yc * yc, axis=-1, keepdims=True) + EPS))
    yr = jnp.concatenate(parts, axis=1) * _silu(rg_ref[0])
    o_ret = jnp.dot(yr.astype(BF16), wro_ref[...], preferred_element_type=F32)

    ni = RWKV_INNER
    rw0 = rwf_ref[0]
    rw1 = rwb_ref[0]
    yw = rw0[:, :ni] + rw1[:, :ni]
    pair_ones = _pair_ones()
    inv_hd = 1.0 / RWKV_HEAD_DIM
    yc = yw - _head_sums([yw], pair_ones)[0] * inv_hd
    var = _head_sums([yc * yc], pair_ones)[0] * inv_hd
    yw = yc * lax.rsqrt(var + RWKV_LN_EPS) * lnw_ref[...] + lnb_ref[...]
    yw = (yw + rw0[:, ni:2 * ni] + rw1[:, ni:2 * ni]) * rw0[:, 2 * ni:3 * ni]
    o_rw = jnp.dot(yw.astype(BF16), wwo_ref[...], preferred_element_type=F32)

    gate = gate_ref[0]
    merged = (_sigmoid(gate[:, :D_MODEL]) * o_ssd
              + _sigmoid(gate[:, D_MODEL:2 * D_MODEL]) * o_ret
              + _sigmoid(gate[:, 2 * D_MODEL:]) * o_rw)
    yx = jnp.dot(merged.astype(BF16), wo_ref[...], preferred_element_type=F32)
    g1 = mod_ref[0, 0, 2:3, :]
    o_ref[0] = x_ref[0] + g1 * _rms(yx, nw_ref[...])


def merge_out(ssd_y, u_zdt, ret_f, ret_b, u_ret, rw_f, rw_b, u_gate, x, mod, nw1, ssd_nw, ln_w, ln_b,
              w_ssd_out, w_ret_out, w_rwkv_out, w_out, n_ctx_tiles):
    b, ta, dm = x.shape
    tm = ROW_TILE
    row = lambda bi, i: (bi, i, 0)
    both = lambda bi, i: (0, bi, i, 0)
    const2 = lambda bi, i: (0, 0)
    return pl.pallas_call(
        _merge_kernel,
        out_shape=jax.ShapeDtypeStruct((b, ta, dm), F32),
        grid=(b, ta // tm),
        in_specs=[
            pl.BlockSpec((2, 1, tm, SSD_INNER), both),
            pl.BlockSpec((1, tm, u_zdt.shape[2]), row),
            pl.BlockSpec((1, tm, RET_INNER), row),
            pl.BlockSpec((1, tm, RET_INNER), row),
            pl.BlockSpec((1, tm, RET_INNER), lambda bi, i: (bi, i, 2)),
            pl.BlockSpec((1, tm, 3 * RWKV_INNER), row),
            pl.BlockSpec((1, tm, 2 * RWKV_INNER), row),
            pl.BlockSpec((1, tm, N_BRANCH * dm), row),
            pl.BlockSpec((1, tm, dm), row),
            pl.BlockSpec((1, 1, SUBLANES, dm), lambda bi, i: (bi, (i >= n_ctx_tiles).astype(jnp.int32), 0, 0)),
            pl.BlockSpec((1, dm), const2),
            pl.BlockSpec((1, SSD_INNER), const2),
            pl.BlockSpec((1, RWKV_INNER), const2),
            pl.BlockSpec((1, RWKV_INNER), const2),
            pl.BlockSpec((SSD_INNER, dm), const2),
            pl.BlockSpec((RET_INNER, dm), const2),
            pl.BlockSpec((RWKV_INNER, dm), const2),
            pl.BlockSpec((dm, dm), const2),
        ],
        out_specs=pl.BlockSpec((1, tm, dm), row),
        compiler_params=_cparams(2),
        name="merge_out",
    )(ssd_y, u_zdt, ret_f, ret_b, u_ret, rw_f, rw_b, u_gate, x, mod, nw1, ssd_nw, ln_w, ln_b,
      w_ssd_out, w_ret_out, w_rwkv_out, w_out)


def _mlp_kernel(x_ref, mod_ref, nw2_ref, nw3_ref, w1_ref, w2_ref, *rest, ff_tile, with_next):
    nb, tm, dm = x_ref.shape
    x = x_ref[...]
    mod = mod_ref[:, 0]
    y = _rms(x, nw2_ref[...])
    h = (y * (1.0 + mod[:, 4:5, :]) + mod[:, 3:4, :]).astype(BF16).reshape(nb * tm, dm)
    acc = jnp.zeros((nb * tm, dm), F32)
    for j in range(D_FF // ff_tile):
        hid = jnp.dot(h, w1_ref[:, j * ff_tile:(j + 1) * ff_tile], preferred_element_type=F32)
        hid = jnp.square(jnp.maximum(hid, 0.0)).astype(BF16)
        acc = acc + jnp.dot(hid, w2_ref[j * ff_tile:(j + 1) * ff_tile, :], preferred_element_type=F32)
    x_new = x + mod[:, 5:6, :] * _rms(acc.reshape(nb, tm, dm), nw3_ref[...])
    if with_next:
        nwn_ref, modn_ref, o_ref, h_ref = rest
        modn = modn_ref[:, 0]
        h_ref[...] = (_rms(x_new, nwn_ref[...]) * (1.0 + modn[:, 1:2, :]) + modn[:, 0:1, :]).astype(BF16)
    else:
        o_ref, = rest
    o_ref[...] = x_new


def mlp_block(x, mod, nw2, nw3, w1, w2, n_ctx_tiles, skip_tiles=0, next_norm=None, ff_tile=1024):
    b, ta, dm = x.shape
    tm = ROW_TILE
    rows = lambda i: (0, i, 0)
    const2 = lambda i: (0, 0)
    mod_spec = pl.BlockSpec((b, 1, SUBLANES, dm),
                            lambda i: (0, (i + skip_tiles >= n_ctx_tiles).astype(jnp.int32), 0, 0))
    resident = pl.Buffered(1)
    in_specs = [
        pl.BlockSpec((b, tm, dm), lambda i: (0, i + skip_tiles, 0)),
        mod_spec,
        pl.BlockSpec((1, dm), const2),
        pl.BlockSpec((1, dm), const2),
        pl.BlockSpec((dm, D_FF), const2, pipeline_mode=resident),
        pl.BlockSpec((D_FF, dm), const2, pipeline_mode=resident),
    ]
    args = [x, mod, nw2, nw3, w1, w2]
    out_rows = ta - skip_tiles * tm
    out_shape = jax.ShapeDtypeStruct((b, out_rows, dm), F32)
    out_specs = pl.BlockSpec((b, tm, dm), rows)
    if next_norm is not None:
        in_specs += [pl.BlockSpec((1, dm), const2), mod_spec]
        args += list(next_norm)
        out_shape = (out_shape, jax.ShapeDtypeStruct((b, out_rows, dm), BF16))
        out_specs = (out_specs, pl.BlockSpec((b, tm, dm), rows))
    return pl.pallas_call(
        functools.partial(_mlp_kernel, ff_tile=ff_tile, with_next=next_norm is not None),
        out_shape=out_shape,
        grid=(ta // tm - skip_tiles,),
        in_specs=in_specs,
        out_specs=out_specs,
        compiler_params=_cparams(1),
        name="mlp",
    )(*args)


def _rope_tables(n_ctx, n_lat):
    rows = n_lat // GRID_W
    row = np.repeat(np.arange(rows), GRID_W).astype(np.float32)
    col = np.tile(np.arange(GRID_W), rows).astype(np.float32)
    n_freq = RET_KEY_DIM // 4
    inv = jnp.power(ROPE_BASE, -jnp.arange(n_freq, dtype=F32) / n_freq)
    ang = jnp.concatenate([jnp.asarray(row)[:, None] * inv, jnp.asarray(col)[:, None] * inv], axis=-1)
    cos = jnp.concatenate([jnp.ones((n_ctx, RET_KEY_DIM // 2), F32), jnp.cos(ang)], axis=0)
    sin = jnp.concatenate([jnp.zeros((n_ctx, RET_KEY_DIM // 2), F32), jnp.sin(ang)], axis=0)
    cos_h = jnp.concatenate([cos, cos], axis=1)
    sin_h = jnp.concatenate([-sin, sin], axis=1)
    reps = 2 * RET_HEADS
    return jnp.tile(cos_h, (1, reps)), jnp.tile(sin_h, (1, reps))


def _pad_lanes(a, width=LANES):
    return jnp.pad(a, [(0, 0)] * (a.ndim - 1) + [(0, width - a.shape[-1])])


def kernel(x, c, ctx, c_ctx, norm_w, ada_w, ada_b, w_in, ssd_conv_w, ssd_conv_b, ssd_dt_bias, ssd_a_log,
           ssd_d, ssd_norm_w, ret_decay, rwkv_mix, rwkv_w0, rwkv_w2, rwkv_a0, rwkv_a2, rwkv_g2, rwkv_k_k,
           rwkv_k_a, rwkv_r_k, rwkv_lnx_w, rwkv_lnx_b, w_ssd_out, w_ret_out, w_rwkv_out, w_out, mlp_w1, mlp_w2):
    b, n_lat, dm = x.shape
    n_ctx = ctx.shape[1]
    ta = n_ctx + n_lat
    depth = norm_w.shape[0]
    n_ctx_tiles = n_ctx // ROW_TILE
    assert n_ctx % ROW_TILE == 0 and n_lat % ROW_TILE == 0 and b + 1 <= SUBLANES

    hexp = np.zeros((LANES, SSD_INNER), np.float32)
    for h in range(SSD_HEADS):
        hexp[h, h * SSD_HEAD_DIM:(h + 1) * SSD_HEAD_DIM] = 1.0
    hexp = jnp.asarray(hexp, BF16)
    cosx, sinx = _rope_tables(n_ctx, n_lat)

    cond = jnp.zeros((SUBLANES, dm), F32).at[:b].set(c).at[b].set(c_ctx)
    mod_all = modulation_all(cond, ada_w, ada_b)

    sizes = (N_BRANCH * dm, SSD_INNER, SSD_CONV_DIM, 2 * SSD_HEADS, RET_QK, RET_QK, RET_INNER, RET_INNER, RWKV_IN)
    offs = np.concatenate([[0], np.cumsum(sizes)])
    perm = np.concatenate([np.concatenate([np.arange(0, RET_KEY_DIM, 2), np.arange(1, RET_KEY_DIM, 2)]) + h * RET_KEY_DIM
                           for h in range(RET_HEADS)])

    xall = jnp.concatenate([ctx, x], axis=1)
    mods = []
    for l in range(depth):
        m = mod_all[l].reshape(SUBLANES, 6, dm)
        m = jnp.pad(m, ((0, 0), (0, SUBLANES - 6), (0, 0)))
        mods.append(jnp.stack([jnp.broadcast_to(m[b], (b,) + m.shape[1:]), m[:b]], axis=1))
    h = norm_modulate(xall, norm_w[0][0:1], mods[0], n_ctx_tiles)
    tm_mm = next(t for t in (1536, 1024, ROW_TILE) if (b * ta) % t == 0)
    for l in range(depth):
        wl = w_in[l]
        seg = [wl[:, offs[j]:offs[j + 1]] for j in range(len(sizes))]
        w_gate = seg[0].astype(BF16)
        w_zdt = jnp.concatenate([seg[1], _pad_lanes(seg[3][:, :SSD_HEADS]), _pad_lanes(seg[3][:, SSD_HEADS:])],
                                axis=1).astype(BF16)
        w_xbc = seg[2].astype(BF16)
        w_ret = jnp.concatenate([seg[4][:, perm], seg[5][:, perm], seg[6], seg[7]], axis=1).astype(BF16)
        w_rw = seg[8].astype(BF16)

        mod = mods[l]
        nw = norm_w[l]
        h = h.reshape(b * ta, dm)
        u_gate = matmul(h, w_gate, tm_mm, 1536).reshape(b, ta, -1)
        u_zdt = matmul(h, w_zdt, tm_mm, w_zdt.shape[1]).reshape(b, ta, -1)
        u_xbc = matmul(h, w_xbc, tm_mm, SSD_CONV_DIM).reshape(b, ta, -1)
        u_ret = matmul(h, w_ret, tm_mm, w_ret.shape[1]).reshape(b, ta, -1)
        u_rw = matmul(h, w_rw, tm_mm, RWKV_IN).reshape(b, ta, -1)

        conv_w = jnp.pad(ssd_conv_w[l], ((0, SUBLANES - SSD_CONV), (0, 0)))
        dskip = jnp.repeat(ssd_d[l], SSD_HEAD_DIM)[None, :]
        ssd_y = ssd_scan(u_xbc, u_zdt, conv_w, ssd_conv_b[l][None, :],
                         _pad_lanes(ssd_dt_bias[l])[:, None, :], _pad_lanes(ssd_a_log[l])[:, None, :],
                         dskip, hexp, n_ctx)
        ret_f, ret_b = ret_scan(u_ret, cosx, sinx, _pad_lanes(ret_decay[l])[:, None, :], n_ctx)
        rw_f, rw_b = rwkv_scan(u_rw, rwkv_mix[l][None, :], rwkv_w0[l][:, None, :], rwkv_w2[l], rwkv_a0[l][:, None, :],
                         rwkv_a2[l], rwkv_g2[l].astype(BF16), rwkv_k_k[l][None, :], rwkv_k_a[l][None, :],
                         rwkv_r_k[l].reshape(1, RWKV_INNER), n_ctx)

        xall = merge_out(ssd_y, u_zdt, ret_f, ret_b, u_ret, rw_f, rw_b, u_gate, xall, mod, nw[1:2],
                         ssd_norm_w[l][None, :], rwkv_lnx_w[l][None, :], rwkv_lnx_b[l][None, :],
                         w_ssd_out[l].astype(BF16), w_ret_out[l].astype(BF16), w_rwkv_out[l].astype(BF16),
                         w_out[l].astype(BF16), n_ctx_tiles)
        w1, w2 = mlp_w1[l].astype(BF16), mlp_w2[l].astype(BF16)
        if l + 1 < depth:
            xall, h = mlp_block(xall, mod, nw[2:3], nw[3:4], w1, w2, n_ctx_tiles,
                                next_norm=(norm_w[l + 1][0:1], mods[l + 1]))
        else:
            xall = mlp_block(xall, mod, nw[2:3], nw[3:4], w1, w2, n_ctx_tiles, skip_tiles=n_ctx_tiles)
    return xall
```
